```python
import math
import jax, jax.numpy as jnp
from jax import lax
import numpy as np

D_MODEL = 1024
BATCH = 8
SEQ = 2048
DEPTH = 2
DEC_BATCH = 128
DEC_SEQ = 1
PAST_LEN = 16384
PAGE_SIZE = 128

N_MIXERS = 2
N_RWKV = (DEPTH + 1) // 2
N_RET = DEPTH // 2
RW_HEAD = 64
RW_HEADS = D_MODEL // RW_HEAD
RW_DECAY_LORA = 64
RW_AAA_LORA = 64
RW_GATE_LORA = 128
RW_GN_EPS = 64e-5
RET_HEADS = 4
RET_DK = D_MODEL // RET_HEADS
RET_DV = 2 * RET_DK
RET_CHUNK = 128
RET_GN_EPS = 1e-6
ROPE_BASE = 10000.0
D_FF = 2816
FFN_RES = 0.5
RMS_EPS = 1e-6

kernel_name = 'rwkv7_retnet_macaron_hybrid_step'

RWKV_KEYS = ('rw_mu', 'rw_w0', 'rw_w1', 'rw_w2', 'rw_a0', 'rw_a1', 'rw_a2', 'rw_g1', 'rw_g2',
             'rw_kk', 'rw_ka', 'rw_rk', 'rw_wr', 'rw_wk', 'rw_wv', 'rw_wo', 'rw_lnx_g', 'rw_lnx_b')
RET_KEYS = ('ret_wq', 'ret_wk', 'ret_wv', 'ret_wg', 'ret_wo')


def _rmsnorm(x, g):
    xf = x.astype(jnp.float32)
    y = xf * lax.rsqrt(jnp.mean(xf * xf, axis=-1, keepdims=True) + RMS_EPS)
    return (y * g.astype(jnp.float32)).astype(x.dtype)


def _swiglu(x, wg, wu, wd):
    return (jax.nn.silu(x @ wg) * (x @ wu)) @ wd


def _head_norm(y, eps):
    mu = jnp.mean(y, axis=-1, keepdims=True)
    yc = y - mu
    return yc * lax.rsqrt(jnp.mean(yc * yc, axis=-1, keepdims=True) + eps)


def _rwkv7_mix(x, shift_prev, wkv0, mu, w0, w1, w2, a0, a1, a2, g1, g2, k_k, k_a, r_k,
               wr, wk, wv, wo, lnx_g, lnx_b):
    B, L, D = x.shape
    H, N = RW_HEADS, RW_HEAD
    f32 = jnp.float32
    x_prev = jnp.concatenate([shift_prev[:, None, :].astype(x.dtype), x[:, :-1]], axis=1)
    xx = x_prev - x
    xr, xw, xk, xv, xa, xg = (x + xx * mu[i] for i in range(6))
    r = xr @ wr
    k = xk @ wk
    v = xv @ wv
    w_log = -jax.nn.softplus(-(w0 + jnp.tanh(xw @ w1) @ w2)) - 0.5
    a = jax.nn.sigmoid(a0 + (xa @ a1) @ a2)
    g = jax.nn.sigmoid(xg @ g1) @ g2
    hd = lambda t: t.astype(f32).reshape(B, L, H, N)
    kk = hd(k * k_k)
    kk = kk / jnp.maximum(jnp.sqrt(jnp.sum(kk * kk, axis=-1, keepdims=True)), 1e-12)
    k = k * (1 + (a - 1) * k_a)
    r_h, k_h, v_h, a_h = hd(r), hd(k), hd(v), hd(a)
    decay = jnp.exp(-jnp.exp(hd(w_log)))

    def step(S, inp):
        r_t, d_t, k_t, v_t, kk_t, a_t = inp
        sa = jnp.einsum('bhvk,bhk->bhv', S, -kk_t)
        S = (S * d_t[:, :, None, :] + sa[..., None] * (kk_t * a_t)[:, :, None, :]
             + v_t[..., None] * k_t[:, :, None, :])
        return S, jnp.einsum('bhvk,bhk->bhv', S, r_t)

    tm = lambda t: jnp.swapaxes(t, 0, 1)
    S, y = lax.scan(step, wkv0.astype(f32),
                    (tm(r_h), tm(decay), tm(k_h), tm(v_h), tm(kk), tm(a_h)))
    y = tm(y)
    y = (_head_norm(y, RW_GN_EPS) * lnx_g.astype(f32).reshape(H, N)
         + lnx_b.astype(f32).reshape(H, N))
    y = y + jnp.sum(r_h * k_h * r_k.astype(f32), axis=-1, keepdims=True) * v_h
    out = (y.reshape(B, L, D).astype(x.dtype) * g) @ wo
    return out, x[:, -1], S


def _rotate(t, pos):
    half = t.shape[-1] // 2
    inv = ROPE_BASE ** (-jnp.arange(half, dtype=jnp.float32) / half)
    ang = pos.astype(jnp.float32)[:, None] * inv[None, :]
    cos = jnp.cos(ang)[None, :, None, :]
    sin = jnp.sin(ang)[None, :, None, :]
    t1, t2 = t[..., :half], t[..., half:]
    return jnp.concatenate([t1 * cos - t2 * sin, t1 * sin + t2 * cos], axis=-1)


def _retention_chunkwise(q, k, v, S0, chunk):
    B, L, H, dk = q.shape
    dv = v.shape[-1]
    nc = L // chunk
    log_g = jnp.log1p(-jnp.exp2(-5.0 - jnp.arange(H, dtype=jnp.float32)))
    idx = jnp.arange(chunk, dtype=jnp.float32)
    diff = idx[:, None] - idx[None, :]
    causal = diff >= 0
    dmat = jnp.where(causal[None], jnp.exp(log_g[:, None, None] * jnp.maximum(diff, 0.0)[None]), 0.0)
    q_dec = jnp.exp(log_g[None, :] * (idx[:, None] + 1.0))[None, :, :, None]
    k_dec = jnp.exp(log_g[None, :] * (chunk - 1.0 - idx[:, None]))[None, :, :, None]
    s_dec = jnp.exp(log_g * chunk)[None, :, None, None]
    cm = lambda t: jnp.swapaxes(t.reshape(B, nc, chunk, H, t.shape[-1]), 0, 1)

    def step(S, inp):
        qc, kc, vc = inp
        inner = jnp.einsum('bqhd,bkhd->bhqk', qc, kc) * dmat[None]
        o = (jnp.einsum('bhqk,bkhe->bqhe', inner, vc)
             + jnp.einsum('bqhd,bhde->bqhe', qc * q_dec, S))
        S = S * s_dec + jnp.einsum('bkhd,bkhe->bhde', kc * k_dec, vc)
        return S, o

    S, o = lax.scan(step, S0, (cm(q), cm(k), cm(v)))
    return jnp.swapaxes(o, 0, 1).reshape(B, L, H, dv), S


def _retnet_mix(x, S0, pos, wq, wk, wv, wg, wo):
    B, L, D = x.shape
    H = RET_HEADS
    f32 = jnp.float32
    q = (x @ wq).astype(f32).reshape(B, L, H, RET_DK)
    k = (x @ wk).astype(f32).reshape(B, L, H, RET_DK) * (RET_DK ** -0.5)
    v = (x @ wv).astype(f32).reshape(B, L, H, RET_DV)
    q, k = _rotate(q, pos), _rotate(k, pos)
    chunk = RET_CHUNK if L % RET_CHUNK == 0 else L
    o, S = _retention_chunkwise(q, k, v, S0.astype(f32), chunk)
    o = _head_norm(o, RET_GN_EPS).reshape(B, L, H * RET_DV).astype(x.dtype)
    out = (jax.nn.silu(x @ wg) * o) @ wo
    return out, S


def _trunk(x, pos, shift0, wkv0, ret0, w):
    new_shift, new_wkv, new_ret = [], [], []
    for i in range(DEPTH):
        h = _rmsnorm(x, w['ln_ffn1'][i])
        x = x + FFN_RES * _swiglu(h, w['ff1_wg'][i], w['ff1_wu'][i], w['ff1_wd'][i])
        h = _rmsnorm(x, w['ln_mix'][i])
        j = i // N_MIXERS
        if i % N_MIXERS == 0:
            out, sh, st = _rwkv7_mix(h, shift0[j], wkv0[j], *(w[n][j] for n in RWKV_KEYS))
            new_shift.append(sh.astype(shift0.dtype))
            new_wkv.append(st.astype(wkv0.dtype))
        else:
            out, st = _retnet_mix(h, ret0[j], pos, *(w[n][j] for n in RET_KEYS))
            new_ret.append(st.astype(ret0.dtype))
        x = x + out
        h = _rmsnorm(x, w['ln_ffn2'][i])
        x = x + FFN_RES * _swiglu(h, w['ff2_wg'][i], w['ff2_wu'][i], w['ff2_wd'][i])
    return _rmsnorm(x, w['ln_final']), jnp.stack(new_shift), jnp.stack(new_wkv), jnp.stack(new_ret)


def setup_inputs(seed: int = 0) -> dict:
    key = jax.random.key(seed)
    ks = iter(jax.random.split(key, 64))
    f32 = jnp.float32
    nrm = lambda shape, s: s * jax.random.normal(next(ks), shape, f32)
    uni = lambda shape, lo, hi: jax.random.uniform(next(ks), shape, f32, lo, hi)
    D, H, N = D_MODEL, RW_HEADS, RW_HEAD
    R = N_RWKV
    return {
        'x_prompt': nrm((BATCH, SEQ, D), 1.0),
        'x_sample': nrm((DEC_BATCH, DEC_SEQ, D), 1.0),
        'state_rwkv_shift': nrm((R, DEC_BATCH, D), 1.0),
        'state_rwkv_wkv': nrm((R, DEC_BATCH, H, N, N), 0.1),
        'state_ret': nrm((N_RET, DEC_BATCH, RET_HEADS, RET_DK, RET_DV), 0.3),
        'ln_ffn1': 1.0 + nrm((DEPTH, D), 0.05),
        'ff1_wg': nrm((DEPTH, D, D_FF), D ** -0.5),
        'ff1_wu': nrm((DEPTH, D, D_FF), D ** -0.5),
        'ff1_wd': nrm((DEPTH, D_FF, D), D_FF ** -0.5),
        'ln_mix': 1.0 + nrm((DEPTH, D), 0.05),
        'ln_ffn2': 1.0 + nrm((DEPTH, D), 0.05),
        'ff2_wg': nrm((DEPTH, D, D_FF), D ** -0.5),
        'ff2_wu': nrm((DEPTH, D, D_FF), D ** -0.5),
        'ff2_wd': nrm((DEPTH, D_FF, D), D_FF ** -0.5),
        'ln_final': 1.0 + nrm((D,), 0.05),
        'rw_mu': uni((R, 6, D), 0.0, 1.0),
        'rw_w0': uni((R, D), -6.0, -1.0),
        'rw_w1': nrm((R, D, RW_DECAY_LORA), D ** -0.5),
        'rw_w2': nrm((R, RW_DECAY_LORA, D), 0.1 * RW_DECAY_LORA ** -0.5),
        'rw_a0': nrm((R, D), 0.1),
        'rw_a1': nrm((R, D, RW_AAA_LORA), D ** -0.5),
        'rw_a2': nrm((R, RW_AAA_LORA, D), 0.1 * RW_AAA_LORA ** -0.5),
        'rw_g1': nrm((R, D, RW_GATE_LORA), D ** -0.5),
        'rw_g2': nrm((R, RW_GATE_LORA, D), RW_GATE_LORA ** -0.5),
        'rw_kk': 0.85 + nrm((R, D), 0.05),
        'rw_ka': 1.0 + nrm((R, D), 0.05),
        'rw_rk': nrm((R, H, N), 0.1),
        'rw_wr': nrm((R, D, D), D ** -0.5),
        'rw_wk': nrm((R, D, D), D ** -0.5),
        'rw_wv': nrm((R, D, D), D ** -0.5),
        'rw_wo': nrm((R, D, D), D ** -0.5),
        'rw_lnx_g': 1.0 + nrm((R, D), 0.05),
        'rw_lnx_b': nrm((R, D), 0.01),
        'ret_wq': nrm((N_RET, D, RET_HEADS * RET_DK), D ** -0.5),
        'ret_wk': nrm((N_RET, D, RET_HEADS * RET_DK), D ** -0.5),
        'ret_wv': nrm((N_RET, D, RET_HEADS * RET_DV), D ** -0.5),
        'ret_wg': nrm((N_RET, D, RET_HEADS * RET_DV), D ** -0.5),
        'ret_wo': nrm((N_RET, RET_HEADS * RET_DV, D), (RET_HEADS * RET_DV) ** -0.5),
    }


def reference(x_prompt, x_sample, state_rwkv_shift, state_rwkv_wkv, state_ret,
              ln_ffn1, ff1_wg, ff1_wu, ff1_wd, ln_mix, ln_ffn2, ff2_wg, ff2_wu, ff2_wd, ln_final,
              rw_mu, rw_w0, rw_w1, rw_w2, rw_a0, rw_a1, rw_a2, rw_g1, rw_g2, rw_kk, rw_ka, rw_rk,
              rw_wr, rw_wk, rw_wv, rw_wo, rw_lnx_g, rw_lnx_b,
              ret_wq, ret_wk, ret_wv, ret_wg, ret_wo):
    w = dict(ln_ffn1=ln_ffn1, ff1_wg=ff1_wg, ff1_wu=ff1_wu, ff1_wd=ff1_wd, ln_mix=ln_mix,
             ln_ffn2=ln_ffn2, ff2_wg=ff2_wg, ff2_wu=ff2_wu, ff2_wd=ff2_wd, ln_final=ln_final,
             rw_mu=rw_mu, rw_w0=rw_w0, rw_w1=rw_w1, rw_w2=rw_w2, rw_a0=rw_a0, rw_a1=rw_a1,
             rw_a2=rw_a2, rw_g1=rw_g1, rw_g2=rw_g2, rw_kk=rw_kk, rw_ka=rw_ka, rw_rk=rw_rk,
             rw_wr=rw_wr, rw_wk=rw_wk, rw_wv=rw_wv, rw_wo=rw_wo, rw_lnx_g=rw_lnx_g,
             rw_lnx_b=rw_lnx_b, ret_wq=ret_wq, ret_wk=ret_wk, ret_wv=ret_wv, ret_wg=ret_wg,
             ret_wo=ret_wo)
    B, L, D = x_prompt.shape
    pos_p = jnp.arange(L, dtype=jnp.float32)
    shift_p0 = jnp.zeros((N_RWKV, B, D), state_rwkv_shift.dtype)
    wkv_p0 = jnp.zeros((N_RWKV, B, RW_HEADS, RW_HEAD, RW_HEAD), state_rwkv_wkv.dtype)
    ret_p0 = jnp.zeros((N_RET, B, RET_HEADS, RET_DK, RET_DV), state_ret.dtype)
    y_prompt, p_shift, p_wkv, p_ret = _trunk(x_prompt, pos_p, shift_p0, wkv_p0, ret_p0, w)
    pos_s = PAST_LEN + jnp.arange(x_sample.shape[1], dtype=jnp.float32)
    y_sample, s_shift, s_wkv, s_ret = _trunk(x_sample, pos_s, state_rwkv_shift, state_rwkv_wkv,
                                             state_ret, w)
    return (y_prompt, y_sample, p_shift, p_wkv, p_ret, s_shift, s_wkv, s_ret)
```

```python
import functools
import math

import numpy as np
import jax
import jax.numpy as jnp
from jax import lax
from jax.experimental import pallas as pl
from jax.experimental.pallas import tpu as pltpu

F32 = jnp.float32
BF16 = jnp.bfloat16
HIGHEST = lax.Precision.HIGHEST

DEPTH = 2
N_MIXERS = 2
RW_HEADS = 16
RW_HEAD = 64
RW_GN_EPS = 64e-5
RET_HEADS = 4
RET_DK = 256
RET_DV = 512
RET_CHUNK = 128
RET_GN_EPS = 1e-6
ROPE_BASE = 10000.0
FFN_RES = 0.5
RMS_EPS = 1e-6
PAST_LEN = 16384

WKV_CHUNK = 64
V7X_VMEM_CAP_BYTES = 60000 * 1024
SUBLANES = 8

NT_DIMS = (((1,), (1,)), ((), ()))
TN_DIMS = (((0,), (0,)), ((), ()))


def _vmem_limit(block_bytes, scratch_bytes=0, temp_bytes=0):
    return int(min(V7X_VMEM_CAP_BYTES, 2 * block_bytes + scratch_bytes + temp_bytes + (4 << 20)))


def _nbytes(shape, dtype):
    return int(np.prod(shape)) * jnp.dtype(dtype).itemsize


def _rms(x, g):
    return x * lax.rsqrt(jnp.mean(x * x, axis=-1, keepdims=True) + RMS_EPS) * g


def _head_norm(y, eps):
    mu = jnp.mean(y, axis=-1, keepdims=True)
    yc = y - mu
    return yc * lax.rsqrt(jnp.mean(yc * yc, axis=-1, keepdims=True) + eps)


def _mm(a, b):
    return jnp.dot(a.astype(BF16), b.astype(BF16), preferred_element_type=F32)


def _mm_nt(a, b):
    return lax.dot_general(a.astype(BF16), b.astype(BF16), NT_DIMS, preferred_element_type=F32)


def _mm_tn(a, b):
    return lax.dot_general(a.astype(BF16), b.astype(BF16), TN_DIMS, preferred_element_type=F32)


def _ffn_body(*refs, n_ff_tiles, final):
    if final:
        x_ref, g_ref, wg_ref, wu_ref, wd_ref, gf_ref, o_ref, h_scr, acc_scr = refs
    else:
        x_ref, g_ref, wg_ref, wu_ref, wd_ref, o_ref, h_scr, acc_scr = refs
    j = pl.program_id(1)

    @pl.when(j == 0)
    def _():
        h_scr[...] = _rms(x_ref[...], g_ref[...]).astype(BF16)
        acc_scr[...] = jnp.zeros_like(acc_scr)

    h = h_scr[...]
    gate = jnp.dot(h, wg_ref[...], preferred_element_type=F32)
    up = jnp.dot(h, wu_ref[...], preferred_element_type=F32)
    act = gate * jax.nn.sigmoid(gate) * up
    acc_scr[...] += _mm(act, wd_ref[...])

    @pl.when(j == n_ff_tiles - 1)
    def _():
        y = x_ref[...] + FFN_RES * acc_scr[...]
        if final:
            y = _rms(y, gf_ref[...])
        o_ref[...] = y


def _ffn(x, ln_g, wg, wu, wd, ln_final=None, *, tm, tf):
    M, D = x.shape
    FF = wg.shape[1]
    final = ln_final is not None
    grid = (M // tm, FF // tf)
    row = lambda i, j: (i, 0)
    const = lambda i, j: (0, 0)
    in_specs = [pl.BlockSpec((tm, D), row), pl.BlockSpec((1, D), const),
                pl.BlockSpec((D, tf), lambda i, j: (0, j)), pl.BlockSpec((D, tf), lambda i, j: (0, j)),
                pl.BlockSpec((tf, D), lambda i, j: (j, 0))]
    args = [x, ln_g.reshape(1, D), wg, wu, wd]
    if final:
        in_specs.append(pl.BlockSpec((1, D), const))
        args.append(ln_final.reshape(1, D))
    blocks = 2 * _nbytes((tm, D), F32) + 3 * _nbytes((D, tf), BF16)
    scratch = _nbytes((tm, D), BF16) + _nbytes((tm, D), F32)
    temps = 4 * _nbytes((tm, tf), F32) + _nbytes((tm, D), F32)
    return pl.pallas_call(
        functools.partial(_ffn_body, n_ff_tiles=grid[1], final=final),
        grid=grid, in_specs=in_specs, out_specs=pl.BlockSpec((tm, D), row),
        out_shape=jax.ShapeDtypeStruct((M, D), F32),
        scratch_shapes=[pltpu.VMEM((tm, D), BF16), pltpu.VMEM((tm, D), F32)],
        compiler_params=pltpu.CompilerParams(dimension_semantics=("parallel", "arbitrary"),
                                             vmem_limit_bytes=_vmem_limit(blocks, scratch, temps)),
        name="ffn_final" if final else "ffn",
    )(*args)


def _rwkv_pre_body(x_ref, lng_ref, prev_ref, mu_ref, w0_ref, a0_ref, wr_ref, wk_ref, wv_ref,
                   w1_ref, w2_ref, a1_ref, a2_ref, g1_ref, g2_ref,
                   r_ref, k_ref, v_ref, lw_ref, a_ref, g_ref, so_ref, *scratch, seq):
    h = _rms(x_ref[...], lng_ref[...])
    if seq:
        (carry,) = scratch
        rows = h.shape[0]

        @pl.when(pl.program_id(1) == 0)
        def _():
            carry[...] = prev_ref[0]

        row = lax.broadcasted_iota(jnp.int32, h.shape, 0)
        xprev = jnp.where(row == 0, carry[...], pltpu.roll(h, 1, 0))
        last = h[rows - 1:rows, :]
        carry[...] = last
        so_ref[0] = last
    else:
        xprev = prev_ref[...]
        so_ref[...] = h
    xx = xprev - h
    xr, xw, xk, xv, xa, xg = (h + xx * mu_ref[i:i + 1, :] for i in range(6))
    r_ref[...] = _mm(xr, wr_ref[...])
    k_ref[...] = _mm(xk, wk_ref[...])
    v_ref[...] = _mm(xv, wv_ref[...])
    z = w0_ref[...] + _mm(jnp.tanh(_mm(xw, w1_ref[...])), w2_ref[...])
    lw_ref[...] = -math.exp(-0.5) * jax.nn.sigmoid(z)
    a_ref[...] = jax.nn.sigmoid(a0_ref[...] + _mm(_mm(xa, a1_ref[...]), a2_ref[...]))
    g_ref[...] = _mm(jax.nn.sigmoid(_mm(xg, g1_ref[...])), g2_ref[...])


def _rwkv_pre(x, prev, p, *, n_seq, tile):
    M, D = x.shape
    L = M // n_seq
    seq = L > 1
    vec = lambda a: a.reshape(1, D)
    weights = [p["wr"], p["wk"], p["wv"], p["w1"], p["w2"], p["a1"], p["a2"], p["g1"], p["g2"]]
    if seq:
        nl = L // tile
        grid = (n_seq, nl)
        row = lambda b, l: (b * nl + l, 0)
        const = lambda b, l: (0, 0)
        prev_arg = prev.reshape(n_seq, 1, D)
        prev_spec = pl.BlockSpec((1, 1, D), lambda b, l: (b, 0, 0))
        so_shape = jax.ShapeDtypeStruct((n_seq, 1, D), F32)
        so_spec = pl.BlockSpec((1, 1, D), lambda b, l: (b, 0, 0))
        scratch = [pltpu.VMEM((1, D), F32)]
        sem = ("parallel", "arbitrary")
    else:
        grid = (M // tile,)
        row = lambda i: (i, 0)
        const = lambda i: (0, 0)
        prev_arg = prev
        prev_spec = pl.BlockSpec((tile, D), row)
        so_shape = jax.ShapeDtypeStruct((M, D), F32)
        so_spec = pl.BlockSpec((tile, D), row)
        scratch = []
        sem = ("parallel",)
    tok = pl.BlockSpec((tile, D), row)
    full = lambda a: pl.BlockSpec(a.shape, const)
    in_specs = ([tok, full(vec(p["ln"])), prev_spec, full(p["mu"]), full(vec(p["w0"])), full(vec(p["a0"]))]
                + [full(w) for w in weights])
    blocks = 8 * _nbytes((tile, D), F32) + sum(_nbytes(w.shape, BF16) for w in weights)
    outs = pl.pallas_call(
        functools.partial(_rwkv_pre_body, seq=seq),
        grid=grid, in_specs=in_specs,
        out_specs=[tok] * 6 + [so_spec],
        out_shape=[jax.ShapeDtypeStruct((M, D), F32)] * 6 + [so_shape],
        scratch_shapes=scratch,
        compiler_params=pltpu.CompilerParams(dimension_semantics=sem,
                                             vmem_limit_bytes=_vmem_limit(blocks, 0, 10 * _nbytes((tile, D), F32))),
        name="rwkv_pre",
    )(x, vec(p["ln"]), prev_arg, p["mu"], vec(p["w0"]), vec(p["a0"]), *weights)
    r, k, v, lw, a, g, so = outs
    return r, k, v, lw, a, g, so.reshape(n_seq, D) if seq else so


def _wkv_chunk_body(r_ref, k_ref, v_ref, lw_ref, a_ref, kkw_ref, kaw_ref, rk_ref, lg_ref, lb_ref,
                    y_ref, s_ref):
    C = WKV_CHUNK
    N = RW_HEAD

    @pl.when(pl.program_id(1) == 0)
    def _():
        s_ref[...] = jnp.zeros_like(s_ref)

    r = r_ref[...]
    k = k_ref[...]
    v = v_ref[...]
    lw = lw_ref[...]
    a = a_ref[...]
    ii = lax.broadcasted_iota(jnp.int32, (C, C), 0)
    jj = lax.broadcasted_iota(jnp.int32, (C, C), 1)
    i2 = lax.broadcasted_iota(jnp.int32, (C, 2 * C), 0)
    j2 = lax.broadcasted_iota(jnp.int32, (C, 2 * C), 1)
    strict = ii > jj
    right_strict = (j2 >= C) & (i2 > j2 - C)
    both_incl = i2 >= jnp.where(j2 >= C, j2 - C, j2)
    cl = jnp.dot((ii >= jj).astype(F32), lw, precision=HIGHEST, preferred_element_type=F32)
    gl = cl[C - 1:C, :]
    e_cl = jnp.exp(cl)
    e_ncl = jnp.exp(-cl)
    e_ce = jnp.exp(cl - lw)
    e_g = jnp.exp(gl - cl)
    dg = jnp.exp(gl)
    kkraw = k * kkw_ref[...]
    kmod = k * (1.0 + (a - 1.0) * kaw_ref[...])
    rkk = r * kmod * rk_ref[...]
    rt = r * e_cl
    kt = kmod * e_ncl
    kg = kmod * e_g
    zeros = jnp.zeros((C, N), F32)
    for hd in range(RW_HEADS):
        sl = slice(hd * N, (hd + 1) * N)
        kkh = kkraw[:, sl]
        nrm = jnp.sqrt(jnp.sum(kkh * kkh, axis=-1, keepdims=True))
        kkn = kkh / jnp.maximum(nrm, 1e-12)
        bh = kkn * a[:, sl]
        vh = v[:, sl]
        lhs2 = jnp.concatenate([-kkn * e_ce[:, sl], rt[:, sl]], axis=0).astype(BF16)
        rhs2 = jnp.concatenate([bh * e_ncl[:, sl], kt[:, sl]], axis=0).astype(BF16)
        P = lax.dot_general(lhs2, rhs2, NT_DIMS, preferred_element_type=F32)
        S = s_ref[0, hd]
        LS = lax.dot_general(lhs2, S.astype(BF16), NT_DIMS, preferred_element_type=F32)
        ptop = P[:C]
        zv = jnp.concatenate([zeros, vh], axis=0)
        X = LS[:C] + _mm(jnp.where(right_strict, ptop, 0.0), zv)
        Pk = jnp.where(strict, ptop[:, :C], 0.0)
        span = 1
        while span < C:
            X = X + _mm(Pk, X)
            span *= 2
            if span < C:
                Pk = _mm(Pk, Pk)
        uv = jnp.concatenate([X, vh], axis=0)
        Y = LS[C:] + _mm(jnp.where(both_incl, P[C:], 0.0), uv)
        bk = jnp.concatenate([bh * e_g[:, sl], kg[:, sl]], axis=0)
        s_ref[0, hd] = S * dg[:, sl] + _mm_tn(uv, bk)
        yn = _head_norm(Y, RW_GN_EPS) * lg_ref[:, sl] + lb_ref[:, sl]
        y_ref[:, sl] = yn + jnp.sum(rkk[:, sl], axis=-1, keepdims=True) * vh


def _wkv_chunked(r, k, v, lw, a, p, *, n_seq):
    M, D = r.shape
    C = WKV_CHUNK
    nc = M // n_seq // C
    tok = pl.BlockSpec((C, D), lambda b, c: (b * nc + c, 0))
    par = pl.BlockSpec((1, D), lambda b, c: (0, 0))
    vec = lambda t: t.reshape(1, D)
    st_spec = pl.BlockSpec((1, RW_HEADS, RW_HEAD, RW_HEAD), lambda b, c: (b, 0, 0, 0))
    blocks = 6 * _nbytes((C, D), F32) + _nbytes((RW_HEADS, RW_HEAD, 128), F32)
    return pl.pallas_call(
        _wkv_chunk_body, grid=(n_seq, nc),
        in_specs=[tok] * 5 + [par] * 5,
        out_specs=[tok, st_spec],
        out_shape=[jax.ShapeDtypeStruct((M, D), F32),
                   jax.ShapeDtypeStruct((n_seq, RW_HEADS, RW_HEAD, RW_HEAD), F32)],
        compiler_params=pltpu.CompilerParams(dimension_semantics=("parallel", "arbitrary"),
                                             vmem_limit_bytes=_vmem_limit(blocks, 0, 24 * _nbytes((C, D), F32))),
        name="wkv_chunk",
    )(r, k, v, lw, a, vec(p["kk"]), vec(p["ka"]), vec(p["rk"]), vec(p["lnx_g"]), vec(p["lnx_b"]))


def _wkv_step_body(r_ref, k_ref, v_ref, lw_ref, a_ref, kkw_ref, kaw_ref, rk_ref, lg_ref, lb_ref, s_ref,
                   y_ref, so_ref):
    r = r_ref[0]
    k = k_ref[0]
    v = v_ref[0]
    a = a_ref[0]
    d = jnp.exp(lw_ref[0])
    kkh = k * kkw_ref[0]
    nrm = jnp.sqrt(jnp.sum(kkh * kkh, axis=-1, keepdims=True))
    kkn = kkh / jnp.maximum(nrm, 1e-12)
    kmod = k * (1.0 + (a - 1.0) * kaw_ref[0])
    bh = kkn * a
    bonus = jnp.sum(r * kmod * rk_ref[0], axis=-1, keepdims=True) * v
    bt = r.shape[0]
    rowid = lax.broadcasted_iota(jnp.int32, (SUBLANES, RW_HEAD), 0)
    for grp in range(bt // SUBLANES):
        rs = slice(grp * SUBLANES, (grp + 1) * SUBLANES)
        y8 = jnp.zeros((SUBLANES, RW_HEAD), F32)
        for j in range(SUBLANES):
            b = grp * SUBLANES + j
            m = rowid == j
            S = s_ref[b, 0]
            sa = _mm_nt(jnp.where(m, -kkn[rs], 0.0), S)
            lhs = jnp.concatenate([sa, jnp.where(m, v[rs], 0.0)], axis=0)
            rhs = jnp.concatenate([jnp.where(m, bh[rs], 0.0), jnp.where(m, kmod[rs], 0.0)], axis=0)
            upd = lax.dot_general(lhs, rhs, TN_DIMS, precision=HIGHEST, preferred_element_type=F32)
            s_new = S * d[b:b + 1, :] + upd
            so_ref[b, 0] = s_new
            y8 = y8 + _mm_nt(jnp.where(m, r[rs], 0.0), s_new)
        y_ref[0, rs, :] = _head_norm(y8, RW_GN_EPS) * lg_ref[0] + lb_ref[0] + bonus[rs]


def _wkv_step(r, k, v, lw, a, p, state, *, bt):
    B, D = r.shape
    H, N = RW_HEADS, RW_HEAD
    heads = lambda t: t.reshape(B, H, N).transpose(1, 0, 2)
    hpar = lambda t: t.reshape(H, 1, N)
    tok = pl.BlockSpec((1, bt, N), lambda i, h: (h, i, 0))
    par = pl.BlockSpec((1, 1, N), lambda i, h: (h, 0, 0))
    st = pl.BlockSpec((bt, 1, N, N), lambda i, h: (i, h, 0, 0))
    blocks = 2 * _nbytes((bt, N, 128), F32) + 6 * _nbytes((bt, 128), F32)
    y, s_new = pl.pallas_call(
        _wkv_step_body, grid=(B // bt, H),
        in_specs=[tok] * 5 + [par] * 5 + [st],
        out_specs=[tok, st],
        out_shape=[jax.ShapeDtypeStruct((H, B, N), F32), jax.ShapeDtypeStruct((B, H, N, N), F32)],
        compiler_params=pltpu.CompilerParams(dimension_semantics=("parallel", "parallel"),
                                             vmem_limit_bytes=_vmem_limit(blocks, 0, 8 << 20)),
        name="wkv_step",
    )(heads(r), heads(k), heads(v), heads(lw), heads(a),
      hpar(p["kk"]), hpar(p["ka"]), hpar(p["rk"]), hpar(p["lnx_g"]), hpar(p["lnx_b"]), state)
    return y.transpose(1, 0, 2).reshape(B, D), s_new


def _rwkv_post_body(x_ref, y_ref, g_ref, wo_ref, o_ref):
    o_ref[...] = x_ref[...] + _mm(y_ref[...] * g_ref[...], wo_ref[...])


def _rwkv_post(x, y, g, wo, *, tm):
    M, D = x.shape
    tok = pl.BlockSpec((tm, D), lambda i: (i, 0))
    blocks = 4 * _nbytes((tm, D), F32) + _nbytes(wo.shape, BF16)
    return pl.pallas_call(
        _rwkv_post_body, grid=(M // tm,),
        in_specs=[tok, tok, tok, pl.BlockSpec(wo.shape, lambda i: (0, 0))],
        out_specs=tok, out_shape=jax.ShapeDtypeStruct((M, D), F32),
        compiler_params=pltpu.CompilerParams(dimension_semantics=("parallel",),
                                             vmem_limit_bytes=_vmem_limit(blocks, 0, 3 * _nbytes((tm, D), F32))),
        name="rwkv_post",
    )(x, y, g, wo)


def _ret_pre_body(x_ref, lng_ref, cos_ref, sin_ref, wq_ref, wk_ref, wv_ref, wg_ref,
                  q_ref, k_ref, v_ref, gate_ref, h_scr):
    @pl.when(pl.program_id(1) == 0)
    def _():
        h_scr[...] = _rms(x_ref[...], lng_ref[...]).astype(BF16)

    h = h_scr[...]
    cos = cos_ref[...]
    sin = sin_ref[...]
    half = RET_DK // 2

    def rotate(t, out_ref):
        t1, t2 = t[:, :half], t[:, half:]
        out_ref[:, :half] = t1 * cos - t2 * sin
        out_ref[:, half:] = t1 * sin + t2 * cos

    rotate(jnp.dot(h, wq_ref[...], preferred_element_type=F32), q_ref)
    rotate(jnp.dot(h, wk_ref[...], preferred_element_type=F32) * (RET_DK ** -0.5), k_ref)
    v_ref[...] = jnp.dot(h, wv_ref[...], preferred_element_type=F32)
    gz = jnp.dot(h, wg_ref[...], preferred_element_type=F32)
    gate_ref[...] = gz * jax.nn.sigmoid(gz)


def _ret_pre(x, ln_g, cos, sin, p, *, tm):
    M, D = x.shape
    H, DK, DV = RET_HEADS, RET_DK, RET_DV
    n_tab = cos.shape[0] // tm
    tab = pl.BlockSpec((tm, DK // 2), lambda i, h: (i % n_tab, 0))
    col = lambda w: pl.BlockSpec((D, w), lambda i, h: (0, h))
    out = lambda w: pl.BlockSpec((tm, w), lambda i, h: (i, h))
    blocks = (_nbytes((tm, D), F32) + 2 * _nbytes((tm, DK // 2), F32) + 2 * _nbytes((D, DK), BF16)
              + 2 * _nbytes((D, DV), BF16) + 2 * _nbytes((tm, DK), F32) + 2 * _nbytes((tm, DV), F32))
    return pl.pallas_call(
        _ret_pre_body, grid=(M // tm, H),
        in_specs=[pl.BlockSpec((tm, D), lambda i, h: (i, 0)), pl.BlockSpec((1, D), lambda i, h: (0, 0)),
                  tab, tab, col(DK), col(DK), col(DV), col(DV)],
        out_specs=[out(DK), out(DK), out(DV), out(DV)],
        out_shape=[jax.ShapeDtypeStruct((M, H * DK), F32), jax.ShapeDtypeStruct((M, H * DK), F32),
                   jax.ShapeDtypeStruct((M, H * DV), F32), jax.ShapeDtypeStruct((M, H * DV), F32)],
        scratch_shapes=[pltpu.VMEM((tm, D), BF16)],
        compiler_params=pltpu.CompilerParams(
            dimension_semantics=("parallel", "arbitrary"),
            vmem_limit_bytes=_vmem_limit(blocks, _nbytes((tm, D), BF16), 4 * _nbytes((tm, DV), F32))),
        name="ret_pre",
    )(x, ln_g.reshape(1, D), cos, sin, p["wq"], p["wk"], p["wv"], p["wg"])


def _ret_log_gamma():
    return jnp.log1p(-jnp.exp2(-5.0 - jnp.arange(RET_HEADS, dtype=F32)))


def _ret_chunk_body(lg_ref, q_ref, k_ref, v_ref, o_ref, s_ref):
    C = RET_CHUNK
    lg = lg_ref[pl.program_id(1)]

    @pl.when(pl.program_id(2) == 0)
    def _():
        s_ref[...] = jnp.zeros_like(s_ref)

    q = q_ref[...]
    k = k_ref[...]
    v = v_ref[...].astype(BF16)
    ii = lax.broadcasted_iota(jnp.int32, (C, C), 0)
    jj = lax.broadcasted_iota(jnp.int32, (C, C), 1)
    diff = (ii - jj).astype(F32)
    dmat = jnp.where(ii >= jj, jnp.exp(lg * jnp.maximum(diff, 0.0)), 0.0)
    idx = lax.broadcasted_iota(jnp.int32, (C, 1), 0).astype(F32)
    q_dec = jnp.exp(lg * (idx + 1.0))
    k_dec = jnp.exp(lg * (C - 1.0 - idx))
    s_dec = jnp.exp(jnp.full((1, RET_DV), lg * C, F32))
    S = s_ref[0, 0]
    inner = _mm_nt(q, k) * dmat
    o_ref[...] = (jnp.dot(inner.astype(BF16), v, preferred_element_type=F32)
                  + _mm(q * q_dec, S))
    s_ref[0, 0] = S * s_dec + lax.dot_general((k * k_dec).astype(BF16), v, TN_DIMS,
                                              preferred_element_type=F32)


def _ret_chunked(q, k, v, *, n_seq):
    M = q.shape[0]
    H, DK, DV, C = RET_HEADS, RET_DK, RET_DV, RET_CHUNK
    nc = M // n_seq // C
    tok = lambda w: pl.BlockSpec((C, w), lambda b, h, c: (b * nc + c, h))
    st = pl.BlockSpec((1, 1, DK, DV), lambda b, h, c: (b, h, 0, 0))
    blocks = 2 * _nbytes((C, DK), F32) + 2 * _nbytes((C, DV), F32) + _nbytes((DK, DV), F32)
    return pl.pallas_call(
        _ret_chunk_body, grid=(n_seq, H, nc),
        in_specs=[pl.BlockSpec(memory_space=pltpu.SMEM), tok(DK), tok(DK), tok(DV)],
        out_specs=[tok(DV), st],
        out_shape=[jax.ShapeDtypeStruct((M, H * DV), F32), jax.ShapeDtypeStruct((n_seq, H, DK, DV), F32)],
        compiler_params=pltpu.CompilerParams(dimension_semantics=("parallel", "parallel", "arbitrary"),
                                             vmem_limit_bytes=_vmem_limit(blocks, 0, 4 * _nbytes((DK, DV), F32))),
        name="ret_chunk",
    )(_ret_log_gamma(), q, k, v)


def _ret_step_body(lg_ref, q_ref, k_ref, v_ref, s_ref, o_ref, so_ref):
    lg = lg_ref[pl.program_id(1)]
    q = q_ref[0]
    k = k_ref[0]
    v = v_ref[0]
    bt = q.shape[0]
    gam_k = jnp.exp(jnp.full((1, RET_DK), lg, F32))
    gam_v = jnp.exp(jnp.full((1, RET_DV), lg, F32))
    qg = q * gam_k
    vb = v.astype(BF16)
    rowid = lax.broadcasted_iota(jnp.int32, (SUBLANES, RET_DK), 0)
    inner = jnp.sum(q * k, axis=-1, keepdims=True)
    for grp in range(bt // SUBLANES):
        rs = slice(grp * SUBLANES, (grp + 1) * SUBLANES)
        o8 = inner[rs] * v[rs]
        for j in range(SUBLANES):
            b = grp * SUBLANES + j
            m = rowid == j
            S = s_ref[b, 0]
            o8 = o8 + _mm(jnp.where(m, qg[rs], 0.0), S)
            so_ref[b, 0] = S * gam_v + lax.dot_general(jnp.where(m, k[rs], 0.0).astype(BF16), vb[rs],
                                                       TN_DIMS, preferred_element_type=F32)
        o_ref[0, rs, :] = o8


def _ret_step(q, k, v, state, *, bt):
    B = q.shape[0]
    H, DK, DV = RET_HEADS, RET_DK, RET_DV
    heads = lambda t, w: t.reshape(B, H, w).transpose(1, 0, 2)
    tok = lambda w: pl.BlockSpec((1, bt, w), lambda i, h: (h, i, 0))
    st = pl.BlockSpec((bt, 1, DK, DV), lambda i, h: (i, h, 0, 0))
    blocks = 2 * _nbytes((bt, DK, DV), F32) + 4 * _nbytes((bt, DV), F32)
    o, s_new = pl.pallas_call(
        _ret_step_body, grid=(B // bt, H),
        in_specs=[pl.BlockSpec(memory_space=pltpu.SMEM), tok(DK), tok(DK), tok(DV), st],
        out_specs=[tok(DV), st],
        out_shape=[jax.ShapeDtypeStruct((H, B, DV), F32), jax.ShapeDtypeStruct((B, H, DK, DV), F32)],
        compiler_params=pltpu.CompilerParams(dimension_semantics=("parallel", "parallel"),
                                             vmem_limit_bytes=_vmem_limit(blocks, 0, 4 * _nbytes((DK, DV), F32))),
        name="ret_step",
    )(_ret_log_gamma(), heads(q, DK), heads(k, DK), heads(v, DV), state)
    return o.transpose(1, 0, 2).reshape(B, H * DV), s_new


def _ret_post_body(x_ref, o_ref, gate_ref, wo_ref, out_ref):
    acc = x_ref[...]
    for hd in range(RET_HEADS):
        sl = slice(hd * RET_DV, (hd + 1) * RET_DV)
        acc = acc + _mm(gate_ref[:, sl] * _head_norm(o_ref[:, sl], RET_GN_EPS), wo_ref[sl, :])
    out_ref[...] = acc


def _ret_post(x, o, gate, wo, *, tm):
    M, D = x.shape
    W = o.shape[1]
    tok = lambda w: pl.BlockSpec((tm, w), lambda i: (i, 0))
    blocks = 2 * _nbytes((tm, D), F32) + 2 * _nbytes((tm, W), F32) + _nbytes(wo.shape, BF16)
    return pl.pallas_call(
        _ret_post_body, grid=(M // tm,),
        in_specs=[tok(D), tok(W), tok(W), pl.BlockSpec(wo.shape, lambda i: (0, 0))],
        out_specs=tok(D), out_shape=jax.ShapeDtypeStruct((M, D), F32),
        compiler_params=pltpu.CompilerParams(dimension_semantics=("parallel",),
                                             vmem_limit_bytes=_vmem_limit(blocks, 0, 3 * _nbytes((tm, W), F32))),
        name="ret_post",
    )(x, o, gate, wo)


def _rope_tables(pos):
    half = RET_DK // 2
    inv = ROPE_BASE ** (-jnp.arange(half, dtype=F32) / half)
    ang = pos.astype(F32)[:, None] * inv[None, :]
    return jnp.cos(ang), jnp.sin(ang)


def _trunk(x, n_seq, pos, shift0, wkv0, ret0, w, tiles):
    M, D = x.shape
    L = M // n_seq
    cos, sin = _rope_tables(pos)
    if L < tiles["ret_pre"]:
        cos, sin = jnp.tile(cos, (M // L, 1)), jnp.tile(sin, (M // L, 1))
    new_shift, new_wkv, new_ret = [], [], []
    for i in range(DEPTH):
        x = _ffn(x, w["ln_ffn1"][i], w["ff1_wg"][i], w["ff1_wu"][i], w["ff1_wd"][i],
                 tm=tiles["ffn_m"], tf=tiles["ffn_f"])
        j = i // N_MIXERS
        if i % N_MIXERS == 0:
            p = {n: w["rw_" + n][j] for n in ("mu", "w0", "w1", "w2", "a0", "a1", "a2", "g1", "g2", "kk", "ka",
                                              "rk", "wr", "wk", "wv", "wo", "lnx_g", "lnx_b")}
            p["ln"] = w["ln_mix"][i]
            prev = jnp.zeros((n_seq, D), F32) if shift0 is None else shift0[j]
            r, k, v, lw, a, g, sh = _rwkv_pre(x, prev, p, n_seq=n_seq, tile=tiles["rwkv_pre"])
            if L > 1:
                y, st = _wkv_chunked(r, k, v, lw, a, p, n_seq=n_seq)
            else:
                y, st = _wkv_step(r, k, v, lw, a, p, wkv0[j], bt=tiles["wkv_step"])
            x = _rwkv_post(x, y, g, p["wo"], tm=tiles["post"])
            new_shift.append(sh)
            new_wkv.append(st)
        else:
            p = {n: w["ret_" + n][j] for n in ("wq", "wk", "wv", "wg", "wo")}
            q, k, v, gate = _ret_pre(x, w["ln_mix"][i], cos, sin, p, tm=tiles["ret_pre"])
            if L > 1:
                o, st = _ret_chunked(q, k, v, n_seq=n_seq)
            else:
                o, st = _ret_step(q, k, v, ret0[j], bt=tiles["ret_step"])
            x = _ret_post(x, o, gate, p["wo"], tm=tiles["post"])
            new_ret.append(st)
        x = _ffn(x, w["ln_ffn2"][i], w["ff2_wg"][i], w["ff2_wu"][i], w["ff2_wd"][i],
                 w["ln_final"] if i == DEPTH - 1 else None, tm=tiles["ffn_m"], tf=tiles["ffn_f"])
    return x, jnp.stack(new_shift), jnp.stack(new_wkv), jnp.stack(new_ret)


PROMPT_TILES = dict(ffn_m=1024, ffn_f=256, rwkv_pre=256, post=256, ret_pre=512)
SAMPLE_TILES = dict(ffn_m=128, ffn_f=1408, rwkv_pre=128, post=128, ret_pre=128, wkv_step=32, ret_step=8)

MATMUL_WEIGHTS = ("ff1_wg", "ff1_wu", "ff1_wd", "ff2_wg", "ff2_wu", "ff2_wd", "rw_w1", "rw_w2", "rw_a1", "rw_a2",
                  "rw_g1", "rw_g2", "rw_wr", "rw_wk", "rw_wv", "rw_wo", "ret_wq", "ret_wk", "ret_wv", "ret_wg",
                  "ret_wo")


def kernel(x_prompt, x_sample, state_rwkv_shift, state_rwkv_wkv, state_ret, ln_ffn1, ff1_wg, ff1_wu, ff1_wd, ln_mix, ln_ffn2, ff2_wg, ff2_wu, ff2_wd, ln_final, rw_mu, rw_w0, rw_w1, rw_w2, rw_a0, rw_a1, rw_a2, rw_g1, rw_g2, rw_kk, rw_ka, rw_rk, rw_wr, rw_wk, rw_wv, rw_wo, rw_lnx_g, rw_lnx_b, ret_wq, ret_wk, ret_wv, ret_wg, ret_wo):
    w = dict(ln_ffn1=ln_ffn1, ff1_wg=ff1_wg, ff1_wu=ff1_wu, ff1_wd=ff1_wd, ln_mix=ln_mix,
             ln_ffn2=ln_ffn2, ff2_wg=ff2_wg, ff2_wu=ff2_wu, ff2_wd=ff2_wd, ln_final=ln_final,
             rw_mu=rw_mu, rw_w0=rw_w0, rw_w1=rw_w1, rw_w2=rw_w2, rw_a0=rw_a0, rw_a1=rw_a1,
             rw_a2=rw_a2, rw_g1=rw_g1, rw_g2=rw_g2, rw_kk=rw_kk, rw_ka=rw_ka, rw_rk=rw_rk,
             rw_wr=rw_wr, rw_wk=rw_wk, rw_wv=rw_wv, rw_wo=rw_wo, rw_lnx_g=rw_lnx_g,
             rw_lnx_b=rw_lnx_b, ret_wq=ret_wq, ret_wk=ret_wk, ret_wv=ret_wv, ret_wg=ret_wg,
             ret_wo=ret_wo)
    for name in MATMUL_WEIGHTS:
        w[name] = w[name].astype(BF16)
    B, L, D = x_prompt.shape
    Bs, Ls, _ = x_sample.shape
    y_p, p_shift, p_wkv, p_ret = _trunk(x_prompt.reshape(B * L, D), B, jnp.arange(L, dtype=F32),
                                        None, None, None, w, PROMPT_TILES)
    pos_s = PAST_LEN + jnp.arange(Ls, dtype=F32)
    y_s, s_shift, s_wkv, s_ret = _trunk(x_sample.reshape(Bs * Ls, D), Bs, pos_s,
                                        state_rwkv_shift, state_rwkv_wkv, state_ret, w, SAMPLE_TILES)
    return (y_p.reshape(B, L, D), y_s.reshape(Bs, Ls, D), p_shift, p_wkv, p_ret, s_shift, s_wkv, s_ret)
```

```python
import functools
import math

import numpy as np
import jax
import jax.numpy as jnp
from jax import lax
from jax.experimental import pallas as pl
from jax.experimental.pallas import tpu as pltpu

F32 = jnp.float32
BF16 = jnp.bfloat16
HIGHEST = lax.Precision.HIGHEST

DEPTH = 2
N_MIXERS = 2
RW_HEADS = 16
RW_HEAD = 64
RW_GN_EPS = 64e-5
RET_HEADS = 4
RET_DK = 256
RET_DV = 512
RET_CHUNK = 128
RET_GN_EPS = 1e-6
ROPE_BASE = 10000.0
FFN_RES = 0.5
RMS_EPS = 1e-6
PAST_LEN = 16384

WKV_CHUNK = 64
V7X_VMEM_CAP_BYTES = 60000 * 1024
SUBLANES = 8

NT_DIMS = (((1,), (1,)), ((), ()))
TN_DIMS = (((0,), (0,)), ((), ()))


def _vmem_limit(block_bytes, scratch_bytes=0, temp_bytes=0):
    return int(min(V7X_VMEM_CAP_BYTES, 2 * block_bytes + scratch_bytes + temp_bytes + (4 << 20)))


def _nbytes(shape, dtype):
    return int(np.prod(shape)) * jnp.dtype(dtype).itemsize


def _rms(x, g):
    return x * lax.rsqrt(jnp.mean(x * x, axis=-1, keepdims=True) + RMS_EPS) * g


def _head_norm(y, eps):
    mu = jnp.mean(y, axis=-1, keepdims=True)
    yc = y - mu
    return yc * lax.rsqrt(jnp.mean(yc * yc, axis=-1, keepdims=True) + eps)


def _mm(a, b):
    return jnp.dot(a.astype(BF16), b.astype(BF16), preferred_element_type=F32)


def _mm_nt(a, b):
    return lax.dot_general(a.astype(BF16), b.astype(BF16), NT_DIMS, preferred_element_type=F32)


def _mm_tn(a, b):
    return lax.dot_general(a.astype(BF16), b.astype(BF16), TN_DIMS, preferred_element_type=F32)


def _ffn_body(*refs, n_ff_tiles, final):
    if final:
        x_ref, g_ref, wg_ref, wu_ref, wd_ref, gf_ref, o_ref, h_scr, acc_scr = refs
    else:
        x_ref, g_ref, wg_ref, wu_ref, wd_ref, o_ref, h_scr, acc_scr = refs
    j = pl.program_id(1)

    @pl.when(j == 0)
    def _():
        h_scr[...] = _rms(x_ref[...], g_ref[...]).astype(BF16)
        acc_scr[...] = jnp.zeros_like(acc_scr)

    h = h_scr[...]
    gate = jnp.dot(h, wg_ref[...], preferred_element_type=F32)
    up = jnp.dot(h, wu_ref[...], preferred_element_type=F32)
    act = gate * jax.nn.sigmoid(gate) * up
    acc_scr[...] += _mm(act, wd_ref[...])

    @pl.when(j == n_ff_tiles - 1)
    def _():
        y = x_ref[...] + FFN_RES * acc_scr[...]
        if final:
            y = _rms(y, gf_ref[...])
        o_ref[...] = y


def _ffn(x, ln_g, wg, wu, wd, ln_final=None, *, tm, tf):
    M, D = x.shape
    FF = wg.shape[1]
    final = ln_final is not None
    grid = (M // tm, FF // tf)
    row = lambda i, j: (i, 0)
    const = lambda i, j: (0, 0)
    in_specs = [pl.BlockSpec((tm, D), row), pl.BlockSpec((1, D), const),
                pl.BlockSpec((D, tf), lambda i, j: (0, j)), pl.BlockSpec((D, tf), lambda i, j: (0, j)),
                pl.BlockSpec((tf, D), lambda i, j: (j, 0))]
    args = [x, ln_g.reshape(1, D), wg, wu, wd]
    if final:
        in_specs.append(pl.BlockSpec((1, D), const))
        args.append(ln_final.reshape(1, D))
    blocks = 2 * _nbytes((tm, D), F32) + 3 * _nbytes((D, tf), BF16)
    scratch = _nbytes((tm, D), BF16) + _nbytes((tm, D), F32)
    temps = 4 * _nbytes((tm, tf), F32) + _nbytes((tm, D), F32)
    return pl.pallas_call(
        functools.partial(_ffn_body, n_ff_tiles=grid[1], final=final),
        grid=grid, in_specs=in_specs, out_specs=pl.BlockSpec((tm, D), row),
        out_shape=jax.ShapeDtypeStruct((M, D), F32),
        scratch_shapes=[pltpu.VMEM((tm, D), BF16), pltpu.VMEM((tm, D), F32)],
        compiler_params=pltpu.CompilerParams(dimension_semantics=("parallel", "arbitrary"),
                                             vmem_limit_bytes=_vmem_limit(blocks, scratch, temps)),
        name="ffn_final" if final else "ffn",
    )(*args)


def _rwkv_pre_body(x_ref, lng_ref, prev_ref, mu_ref, w0_ref, a0_ref, wr_ref, wk_ref, wv_ref,
                   w1_ref, w2_ref, a1_ref, a2_ref, g1_ref, g2_ref,
                   r_ref, k_ref, v_ref, lw_ref, a_ref, g_ref, so_ref, *scratch, seq):
    h = _rms(x_ref[...], lng_ref[...])
    if seq:
        (carry,) = scratch
        rows = h.shape[0]

        @pl.when(pl.program_id(1) == 0)
        def _():
            carry[...] = prev_ref[0]

        row = lax.broadcasted_iota(jnp.int32, h.shape, 0)
        xprev = jnp.where(row == 0, carry[...], pltpu.roll(h, 1, 0))
        last = h[rows - 1:rows, :]
        carry[...] = last
        so_ref[0] = last
    else:
        xprev = prev_ref[...]
        so_ref[...] = h
    xx = xprev - h
    xr, xw, xk, xv, xa, xg = (h + xx * mu_ref[i:i + 1, :] for i in range(6))
    r_ref[...] = _mm(xr, wr_ref[...])
    k_ref[...] = _mm(xk, wk_ref[...])
    v_ref[...] = _mm(xv, wv_ref[...])
    z = w0_ref[...] + _mm(jnp.tanh(_mm(xw, w1_ref[...])), w2_ref[...])
    lw_ref[...] = -math.exp(-0.5) * jax.nn.sigmoid(z)
    a_ref[...] = jax.nn.sigmoid(a0_ref[...] + _mm(_mm(xa, a1_ref[...]), a2_ref[...]))
    g_ref[...] = _mm(jax.nn.sigmoid(_mm(xg, g1_ref[...])), g2_ref[...])


def _rwkv_pre(x, prev, p, *, n_seq, tile):
    M, D = x.shape
    L = M // n_seq
    seq = L > 1
    vec = lambda a: a.reshape(1, D)
    weights = [p["wr"], p["wk"], p["wv"], p["w1"], p["w2"], p["a1"], p["a2"], p["g1"], p["g2"]]
    if seq:
        nl = L // tile
        grid = (n_seq, nl)
        row = lambda b, l: (b * nl + l, 0)
        const = lambda b, l: (0, 0)
        prev_arg = prev.reshape(n_seq, 1, D)
        prev_spec = pl.BlockSpec((1, 1, D), lambda b, l: (b, 0, 0))
        so_shape = jax.ShapeDtypeStruct((n_seq, 1, D), F32)
        so_spec = pl.BlockSpec((1, 1, D), lambda b, l: (b, 0, 0))
        scratch = [pltpu.VMEM((1, D), F32)]
        sem = ("parallel", "arbitrary")
    else:
        grid = (M // tile,)
        row = lambda i: (i, 0)
        const = lambda i: (0, 0)
        prev_arg = prev
        prev_spec = pl.BlockSpec((tile, D), row)
        so_shape = jax.ShapeDtypeStruct((M, D), F32)
        so_spec = pl.BlockSpec((tile, D), row)
        scratch = []
        sem = ("parallel",)
    tok = pl.BlockSpec((tile, D), row)
    full = lambda a: pl.BlockSpec(a.shape, const)
    in_specs = ([tok, full(vec(p["ln"])), prev_spec, full(p["mu"]), full(vec(p["w0"])), full(vec(p["a0"]))]
                + [full(w) for w in weights])
    blocks = 8 * _nbytes((tile, D), F32) + sum(_nbytes(w.shape, BF16) for w in weights)
    outs = pl.pallas_call(
        functools.partial(_rwkv_pre_body, seq=seq),
        grid=grid, in_specs=in_specs,
        out_specs=[tok] * 6 + [so_spec],
        out_shape=[jax.ShapeDtypeStruct((M, D), F32)] * 6 + [so_shape],
        scratch_shapes=scratch,
        compiler_params=pltpu.CompilerParams(dimension_semantics=sem,
                                             vmem_limit_bytes=_vmem_limit(blocks, 0, 10 * _nbytes((tile, D), F32))),
        name="rwkv_pre",
    )(x, vec(p["ln"]), prev_arg, p["mu"], vec(p["w0"]), vec(p["a0"]), *weights)
    r, k, v, lw, a, g, so = outs
    return r, k, v, lw, a, g, so.reshape(n_seq, D) if seq else so


def _wkv_chunk_body(r_ref, k_ref, v_ref, lw_ref, a_ref, kkw_ref, kaw_ref, rk_ref, lg_ref, lb_ref,
                    y_ref, s_ref):
    C = WKV_CHUNK
    N = RW_HEAD

    @pl.when(pl.program_id(1) == 0)
    def _():
        s_ref[...] = jnp.zeros_like(s_ref)

    r = r_ref[...]
    k = k_ref[...]
    v = v_ref[...]
    lw = lw_ref[...]
    a = a_ref[...]
    ii = lax.broadcasted_iota(jnp.int32, (C, C), 0)
    jj = lax.broadcasted_iota(jnp.int32, (C, C), 1)
    i2 = lax.broadcasted_iota(jnp.int32, (C, 2 * C), 0)
    j2 = lax.broadcasted_iota(jnp.int32, (C, 2 * C), 1)
    strict = ii > jj
    right_strict = (j2 >= C) & (i2 > j2 - C)
    both_incl = i2 >= jnp.where(j2 >= C, j2 - C, j2)
    cl = jnp.dot((ii >= jj).astype(F32), lw, precision=HIGHEST, preferred_element_type=F32)
    gl = cl[C - 1:C, :]
    e_cl = jnp.exp(cl)
    e_ncl = jnp.exp(-cl)
    e_ce = jnp.exp(cl - lw)
    e_g = jnp.exp(gl - cl)
    dg = jnp.exp(gl)
    kkraw = k * kkw_ref[...]
    kmod = k * (1.0 + (a - 1.0) * kaw_ref[...])
    rkk = r * kmod * rk_ref[...]
    rt = r * e_cl
    kt = kmod * e_ncl
    kg = kmod * e_g
    zeros = jnp.zeros((C, N), F32)
    lnx_g = lg_ref[...]
    lnx_b = lb_ref[...]
    heads = range(RW_HEADS)
    sls = [slice(hd * N, (hd + 1) * N) for hd in heads]
    S = [s_ref[0, hd] for hd in heads]
    vh = [v[:, sl] for sl in sls]
    bh, lhs2, rhs2 = [], [], []
    for sl in sls:
        kkh = kkraw[:, sl]
        nrm = jnp.sqrt(jnp.sum(kkh * kkh, axis=-1, keepdims=True))
        kkn = kkh / jnp.maximum(nrm, 1e-12)
        bh.append(kkn * a[:, sl])
        lhs2.append(jnp.concatenate([-kkn * e_ce[:, sl], rt[:, sl]], axis=0).astype(BF16))
        rhs2.append(jnp.concatenate([bh[-1] * e_ncl[:, sl], kt[:, sl]], axis=0).astype(BF16))
    P = [lax.dot_general(lhs2[hd], rhs2[hd], NT_DIMS, preferred_element_type=F32) for hd in heads]
    LS = [lax.dot_general(lhs2[hd], S[hd].astype(BF16), NT_DIMS, preferred_element_type=F32) for hd in heads]
    X = [LS[hd][:C] + _mm(jnp.where(right_strict, P[hd][:C], 0.0), jnp.concatenate([zeros, vh[hd]], axis=0))
         for hd in heads]
    Pk = [jnp.where(strict, P[hd][:C, :C], 0.0) for hd in heads]
    span = 1
    while span < C:
        X = [X[hd] + _mm(Pk[hd], X[hd]) for hd in heads]
        span *= 2
        if span < C:
            Pk = [_mm(Pk[hd], Pk[hd]) for hd in heads]
    uv = [jnp.concatenate([X[hd], vh[hd]], axis=0) for hd in heads]
    Y = [LS[hd][C:] + _mm(jnp.where(both_incl, P[hd][C:], 0.0), uv[hd]) for hd in heads]
    s_new = [S[hd] * dg[:, sls[hd]]
             + _mm_tn(uv[hd], jnp.concatenate([bh[hd] * e_g[:, sls[hd]], kg[:, sls[hd]]], axis=0))
             for hd in heads]
    y_out = [_head_norm(Y[hd], RW_GN_EPS) * lnx_g[:, sls[hd]] + lnx_b[:, sls[hd]]
             + jnp.sum(rkk[:, sls[hd]], axis=-1, keepdims=True) * vh[hd] for hd in heads]
    for hd in heads:
        s_ref[0, hd] = s_new[hd]
    y_ref[...] = jnp.concatenate(y_out, axis=-1)


def _wkv_chunked(r, k, v, lw, a, p, *, n_seq):
    M, D = r.shape
    C = WKV_CHUNK
    nc = M // n_seq // C
    tok = pl.BlockSpec((C, D), lambda b, c: (b * nc + c, 0))
    par = pl.BlockSpec((1, D), lambda b, c: (0, 0))
    vec = lambda t: t.reshape(1, D)
    st_spec = pl.BlockSpec((1, RW_HEADS, RW_HEAD, RW_HEAD), lambda b, c: (b, 0, 0, 0))
    blocks = 6 * _nbytes((C, D), F32) + _nbytes((RW_HEADS, RW_HEAD, 128), F32)
    return pl.pallas_call(
        _wkv_chunk_body, grid=(n_seq, nc),
        in_specs=[tok] * 5 + [par] * 5,
        out_specs=[tok, st_spec],
        out_shape=[jax.ShapeDtypeStruct((M, D), F32),
                   jax.ShapeDtypeStruct((n_seq, RW_HEADS, RW_HEAD, RW_HEAD), F32)],
        compiler_params=pltpu.CompilerParams(dimension_semantics=("parallel", "arbitrary"),
                                             vmem_limit_bytes=_vmem_limit(blocks, 0, 24 * _nbytes((C, D), F32))),
        name="wkv_chunk",
    )(r, k, v, lw, a, vec(p["kk"]), vec(p["ka"]), vec(p["rk"]), vec(p["lnx_g"]), vec(p["lnx_b"]))


def _wkv_step_body(r_ref, k_ref, v_ref, lw_ref, a_ref, kkw_ref, kaw_ref, rk_ref, lg_ref, lb_ref, s_ref,
                   y_ref, so_ref):
    r = r_ref[0]
    k = k_ref[0]
    v = v_ref[0]
    a = a_ref[0]
    d = jnp.exp(lw_ref[0])
    kkh = k * kkw_ref[0]
    nrm = jnp.sqrt(jnp.sum(kkh * kkh, axis=-1, keepdims=True))
    kkn = kkh / jnp.maximum(nrm, 1e-12)
    kmod = k * (1.0 + (a - 1.0) * kaw_ref[0])
    bh = kkn * a
    bonus = jnp.sum(r * kmod * rk_ref[0], axis=-1, keepdims=True) * v
    bt = r.shape[0]
    rowid = lax.broadcasted_iota(jnp.int32, (SUBLANES, RW_HEAD), 0)
    rows = range(bt)
    rs = [slice(b // SUBLANES * SUBLANES, (b // SUBLANES + 1) * SUBLANES) for b in rows]
    m = [rowid == b % SUBLANES for b in rows]
    S = [s_ref[b, 0] for b in rows]
    sa = [_mm_nt(jnp.where(m[b], -kkn[rs[b]], 0.0), S[b]) for b in rows]
    upd = [lax.dot_general(
        jnp.concatenate([sa[b], jnp.where(m[b], v[rs[b]], 0.0)], axis=0),
        jnp.concatenate([jnp.where(m[b], bh[rs[b]], 0.0), jnp.where(m[b], kmod[rs[b]], 0.0)], axis=0),
        TN_DIMS, precision=HIGHEST, preferred_element_type=F32) for b in rows]
    s_new = [S[b] * d[b:b + 1, :] + upd[b] for b in rows]
    ys = [_mm_nt(jnp.where(m[b], r[rs[b]], 0.0), s_new[b]) for b in rows]
    for b in rows:
        so_ref[b, 0] = s_new[b]
    for grp in range(bt // SUBLANES):
        y8 = functools.reduce(lambda acc, t: acc + t, ys[grp * SUBLANES:(grp + 1) * SUBLANES])
        gs = rs[grp * SUBLANES]
        y_ref[0, gs, :] = _head_norm(y8, RW_GN_EPS) * lg_ref[0] + lb_ref[0] + bonus[gs]


def _wkv_step(r, k, v, lw, a, p, state, *, bt):
    B, D = r.shape
    H, N = RW_HEADS, RW_HEAD
    heads = lambda t: t.reshape(B, H, N).transpose(1, 0, 2)
    hpar = lambda t: t.reshape(H, 1, N)
    tok = pl.BlockSpec((1, bt, N), lambda i, h: (h, i, 0))
    par = pl.BlockSpec((1, 1, N), lambda i, h: (h, 0, 0))
    st = pl.BlockSpec((bt, 1, N, N), lambda i, h: (i, h, 0, 0))
    blocks = 2 * _nbytes((bt, N, 128), F32) + 6 * _nbytes((bt, 128), F32)
    y, s_new = pl.pallas_call(
        _wkv_step_body, grid=(B // bt, H),
        in_specs=[tok] * 5 + [par] * 5 + [st],
        out_specs=[tok, st],
        out_shape=[jax.ShapeDtypeStruct((H, B, N), F32), jax.ShapeDtypeStruct((B, H, N, N), F32)],
        compiler_params=pltpu.CompilerParams(dimension_semantics=("parallel", "parallel"),
                                             vmem_limit_bytes=_vmem_limit(blocks, 0, 8 << 20)),
        name="wkv_step",
    )(heads(r), heads(k), heads(v), heads(lw), heads(a),
      hpar(p["kk"]), hpar(p["ka"]), hpar(p["rk"]), hpar(p["lnx_g"]), hpar(p["lnx_b"]), state)
    return y.transpose(1, 0, 2).reshape(B, D), s_new


def _rwkv_post_body(x_ref, y_ref, g_ref, wo_ref, o_ref):
    o_ref[...] = x_ref[...] + _mm(y_ref[...] * g_ref[...], wo_ref[...])


def _rwkv_post(x, y, g, wo, *, tm):
    M, D = x.shape
    tok = pl.BlockSpec((tm, D), lambda i: (i, 0))
    blocks = 4 * _nbytes((tm, D), F32) + _nbytes(wo.shape, BF16)
    return pl.pallas_call(
        _rwkv_post_body, grid=(M // tm,),
        in_specs=[tok, tok, tok, pl.BlockSpec(wo.shape, lambda i: (0, 0))],
        out_specs=tok, out_shape=jax.ShapeDtypeStruct((M, D), F32),
        compiler_params=pltpu.CompilerParams(dimension_semantics=("parallel",),
                                             vmem_limit_bytes=_vmem_limit(blocks, 0, 3 * _nbytes((tm, D), F32))),
        name="rwkv_post",
    )(x, y, g, wo)


def _ret_pre_body(x_ref, lng_ref, cos_ref, sin_ref, wq_ref, wk_ref, wv_ref, wg_ref,
                  q_ref, k_ref, v_ref, gate_ref, h_scr):
    @pl.when(pl.program_id(1) == 0)
    def _():
        h_scr[...] = _rms(x_ref[...], lng_ref[...]).astype(BF16)

    h = h_scr[...]
    cos = cos_ref[...]
    sin = sin_ref[...]
    half = RET_DK // 2

    def rotate(t, out_ref):
        t1, t2 = t[:, :half], t[:, half:]
        out_ref[:, :half] = t1 * cos - t2 * sin
        out_ref[:, half:] = t1 * sin + t2 * cos

    rotate(jnp.dot(h, wq_ref[...], preferred_element_type=F32), q_ref)
    rotate(jnp.dot(h, wk_ref[...], preferred_element_type=F32) * (RET_DK ** -0.5), k_ref)
    v_ref[...] = jnp.dot(h, wv_ref[...], preferred_element_type=F32)
    gz = jnp.dot(h, wg_ref[...], preferred_element_type=F32)
    gate_ref[...] = gz * jax.nn.sigmoid(gz)


def _ret_pre(x, ln_g, cos, sin, p, *, tm):
    M, D = x.shape
    H, DK, DV = RET_HEADS, RET_DK, RET_DV
    n_tab = cos.shape[0] // tm
    tab = pl.BlockSpec((tm, DK // 2), lambda i, h: (i % n_tab, 0))
    col = lambda w: pl.BlockSpec((D, w), lambda i, h: (0, h))
    out = lambda w: pl.BlockSpec((tm, w), lambda i, h: (i, h))
    blocks = (_nbytes((tm, D), F32) + 2 * _nbytes((tm, DK // 2), F32) + 2 * _nbytes((D, DK), BF16)
              + 2 * _nbytes((D, DV), BF16) + 2 * _nbytes((tm, DK), F32) + 2 * _nbytes((tm, DV), F32))
    return pl.pallas_call(
        _ret_pre_body, grid=(M // tm, H),
        in_specs=[pl.BlockSpec((tm, D), lambda i, h: (i, 0)), pl.BlockSpec((1, D), lambda i, h: (0, 0)),
                  tab, tab, col(DK), col(DK), col(DV), col(DV)],
        out_specs=[out(DK), out(DK), out(DV), out(DV)],
        out_shape=[jax.ShapeDtypeStruct((M, H * DK), F32), jax.ShapeDtypeStruct((M, H * DK), F32),
                   jax.ShapeDtypeStruct((M, H * DV), F32), jax.ShapeDtypeStruct((M, H * DV), F32)],
        scratch_shapes=[pltpu.VMEM((tm, D), BF16)],
        compiler_params=pltpu.CompilerParams(
            dimension_semantics=("parallel", "arbitrary"),
            vmem_limit_bytes=_vmem_limit(blocks, _nbytes((tm, D), BF16), 4 * _nbytes((tm, DV), F32))),
        name="ret_pre",
    )(x, ln_g.reshape(1, D), cos, sin, p["wq"], p["wk"], p["wv"], p["wg"])


def _ret_log_gamma():
    return jnp.log1p(-jnp.exp2(-5.0 - jnp.arange(RET_HEADS, dtype=F32)))


def _ret_chunk_body(lg_ref, q_ref, k_ref, v_ref, o_ref, s_ref):
    C = RET_CHUNK
    lg = lg_ref[pl.program_id(1)]

    @pl.when(pl.program_id(2) == 0)
    def _():
        s_ref[...] = jnp.zeros_like(s_ref)

    q = q_ref[...]
    k = k_ref[...]
    v = v_ref[...].astype(BF16)
    ii = lax.broadcasted_iota(jnp.int32, (C, C), 0)
    jj = lax.broadcasted_iota(jnp.int32, (C, C), 1)
    diff = (ii - jj).astype(F32)
    dmat = jnp.where(ii >= jj, jnp.exp(lg * jnp.maximum(diff, 0.0)), 0.0)
    idx = lax.broadcasted_iota(jnp.int32, (C, 1), 0).astype(F32)
    q_dec = jnp.exp(lg * (idx + 1.0))
    k_dec = jnp.exp(lg * (C - 1.0 - idx))
    s_dec = jnp.exp(jnp.full((1, RET_DV), lg * C, F32))
    S = s_ref[0, 0]
    inner = _mm_nt(q, k) * dmat
    o_ref[...] = (jnp.dot(inner.astype(BF16), v, preferred_element_type=F32)
                  + _mm(q * q_dec, S))
    s_ref[0, 0] = S * s_dec + lax.dot_general((k * k_dec).astype(BF16), v, TN_DIMS,
                                              preferred_element_type=F32)


def _ret_chunked(q, k, v, *, n_seq):
    M = q.shape[0]
    H, DK, DV, C = RET_HEADS, RET_DK, RET_DV, RET_CHUNK
    nc = M // n_seq // C
    tok = lambda w: pl.BlockSpec((C, w), lambda b, h, c: (b * nc + c, h))
    st = pl.BlockSpec((1, 1, DK, DV), lambda b, h, c: (b, h, 0, 0))
    blocks = 2 * _nbytes((C, DK), F32) + 2 * _nbytes((C, DV), F32) + _nbytes((DK, DV), F32)
    return pl.pallas_call(
        _ret_chunk_body, grid=(n_seq, H, nc),
        in_specs=[pl.BlockSpec(memory_space=pltpu.SMEM), tok(DK), tok(DK), tok(DV)],
        out_specs=[tok(DV), st],
        out_shape=[jax.ShapeDtypeStruct((M, H * DV), F32), jax.ShapeDtypeStruct((n_seq, H, DK, DV), F32)],
        compiler_params=pltpu.CompilerParams(dimension_semantics=("parallel", "parallel", "arbitrary"),
                                             vmem_limit_bytes=_vmem_limit(blocks, 0, 4 * _nbytes((DK, DV), F32))),
        name="ret_chunk",
    )(_ret_log_gamma(), q, k, v)


def _ret_step_body(lg_ref, q_ref, k_ref, v_ref, s_ref, o_ref, so_ref):
    lg = lg_ref[pl.program_id(1)]
    q = q_ref[0]
    k = k_ref[0]
    v = v_ref[0]
    bt = q.shape[0]
    gam_k = jnp.exp(jnp.full((1, RET_DK), lg, F32))
    gam_v = jnp.exp(jnp.full((1, RET_DV), lg, F32))
    qg = q * gam_k
    vb = v.astype(BF16)
    rowid = lax.broadcasted_iota(jnp.int32, (SUBLANES, RET_DK), 0)
    inner = jnp.sum(q * k, axis=-1, keepdims=True)
    for grp in range(bt // SUBLANES):
        rs = slice(grp * SUBLANES, (grp + 1) * SUBLANES)
        o8 = inner[rs] * v[rs]
        for j in range(SUBLANES):
            b = grp * SUBLANES + j
            m = rowid == j
            S = s_ref[b, 0]
            o8 = o8 + _mm(jnp.where(m, qg[rs], 0.0), S)
            so_ref[b, 0] = S * gam_v + lax.dot_general(jnp.where(m, k[rs], 0.0).astype(BF16), vb[rs],
                                                       TN_DIMS, preferred_element_type=F32)
        o_ref[0, rs, :] = o8


def _ret_step(q, k, v, state, *, bt):
    B = q.shape[0]
    H, DK, DV = RET_HEADS, RET_DK, RET_DV
    heads = lambda t, w: t.reshape(B, H, w).transpose(1, 0, 2)
    tok = lambda w: pl.BlockSpec((1, bt, w), lambda i, h: (h, i, 0))
    st = pl.BlockSpec((bt, 1, DK, DV), lambda i, h: (i, h, 0, 0))
    blocks = 2 * _nbytes((bt, DK, DV), F32) + 4 * _nbytes((bt, DV), F32)
    o, s_new = pl.pallas_call(
        _ret_step_body, grid=(B // bt, H),
        in_specs=[pl.BlockSpec(memory_space=pltpu.SMEM), tok(DK), tok(DK), tok(DV), st],
        out_specs=[tok(DV), st],
        out_shape=[jax.ShapeDtypeStruct((H, B, DV), F32), jax.ShapeDtypeStruct((B, H, DK, DV), F32)],
        compiler_params=pltpu.CompilerParams(dimension_semantics=("parallel", "parallel"),
                                             vmem_limit_bytes=_vmem_limit(blocks, 0, 4 * _nbytes((DK, DV), F32))),
        name="ret_step",
    )(_ret_log_gamma(), heads(q, DK), heads(k, DK), heads(v, DV), state)
    return o.transpose(1, 0, 2).reshape(B, H * DV), s_new


def _ret_post_body(x_ref, o_ref, gate_ref, wo_ref, out_ref):
    acc = x_ref[...]
    for hd in range(RET_HEADS):
        sl = slice(hd * RET_DV, (hd + 1) * RET_DV)
        acc = acc + _mm(gate_ref[:, sl] * _head_norm(o_ref[:, sl], RET_GN_EPS), wo_ref[sl, :])
    out_ref[...] = acc


def _ret_post(x, o, gate, wo, *, tm):
    M, D = x.shape
    W = o.shape[1]
    tok = lambda w: pl.BlockSpec((tm, w), lambda i: (i, 0))
    blocks = 2 * _nbytes((tm, D), F32) + 2 * _nbytes((tm, W), F32) + _nbytes(wo.shape, BF16)
    return pl.pallas_call(
        _ret_post_body, grid=(M // tm,),
        in_specs=[tok(D), tok(W), tok(W), pl.BlockSpec(wo.shape, lambda i: (0, 0))],
        out_specs=tok(D), out_shape=jax.ShapeDtypeStruct((M, D), F32),
        compiler_params=pltpu.CompilerParams(dimension_semantics=("parallel",),
                                             vmem_limit_bytes=_vmem_limit(blocks, 0, 3 * _nbytes((tm, W), F32))),
        name="ret_post",
    )(x, o, gate, wo)


def _rope_tables(pos):
    half = RET_DK // 2
    inv = ROPE_BASE ** (-jnp.arange(half, dtype=F32) / half)
    ang = pos.astype(F32)[:, None] * inv[None, :]
    return jnp.cos(ang), jnp.sin(ang)


def _trunk(x, n_seq, pos, shift0, wkv0, ret0, w, tiles):
    M, D = x.shape
    L = M // n_seq
    cos, sin = _rope_tables(pos)
    if L < tiles["ret_pre"]:
        cos, sin = jnp.tile(cos, (M // L, 1)), jnp.tile(sin, (M // L, 1))
    new_shift, new_wkv, new_ret = [], [], []
    for i in range(DEPTH):
        x = _ffn(x, w["ln_ffn1"][i], w["ff1_wg"][i], w["ff1_wu"][i], w["ff1_wd"][i],
                 tm=tiles["ffn_m"], tf=tiles["ffn_f"])
        j = i // N_MIXERS
        if i % N_MIXERS == 0:
            p = {n: w["rw_" + n][j] for n in ("mu", "w0", "w1", "w2", "a0", "a1", "a2", "g1", "g2", "kk", "ka",
                                              "rk", "wr", "wk", "wv", "wo", "lnx_g", "lnx_b")}
            p["ln"] = w["ln_mix"][i]
            prev = jnp.zeros((n_seq, D), F32) if shift0 is None else shift0[j]
            r, k, v, lw, a, g, sh = _rwkv_pre(x, prev, p, n_seq=n_seq, tile=tiles["rwkv_pre"])
            if L > 1:
                y, st = _wkv_chunked(r, k, v, lw, a, p, n_seq=n_seq)
            else:
                y, st = _wkv_step(r, k, v, lw, a, p, wkv0[j], bt=tiles["wkv_step"])
            x = _rwkv_post(x, y, g, p["wo"], tm=tiles["post"])
            new_shift.append(sh)
            new_wkv.append(st)
        else:
            p = {n: w["ret_" + n][j] for n in ("wq", "wk", "wv", "wg", "wo")}
            q, k, v, gate = _ret_pre(x, w["ln_mix"][i], cos, sin, p, tm=tiles["ret_pre"])
            if L > 1:
                o, st = _ret_chunked(q, k, v, n_seq=n_seq)
            else:
                o, st = _ret_step(q, k, v, ret0[j], bt=tiles["ret_step"])
            x = _ret_post(x, o, gate, p["wo"], tm=tiles["post"])
            new_ret.append(st)
        x = _ffn(x, w["ln_ffn2"][i], w["ff2_wg"][i], w["ff2_wu"][i], w["ff2_wd"][i],
                 w["ln_final"] if i == DEPTH - 1 else None, tm=tiles["ffn_m"], tf=tiles["ffn_f"])
    return x, jnp.stack(new_shift), jnp.stack(new_wkv), jnp.stack(new_ret)


PROMPT_TILES = dict(ffn_m=1024, ffn_f=256, rwkv_pre=256, post=256, ret_pre=512)
SAMPLE_TILES = dict(ffn_m=128, ffn_f=1408, rwkv_pre=128, post=128, ret_pre=128, wkv_step=32, ret_step=8)

MATMUL_WEIGHTS = ("ff1_wg", "ff1_wu", "ff1_wd", "ff2_wg", "ff2_wu", "ff2_wd", "rw_w1", "rw_w2", "rw_a1", "rw_a2",
                  "rw_g1", "rw_g2", "rw_wr", "rw_wk", "rw_wv", "rw_wo", "ret_wq", "ret_wk", "ret_wv", "ret_wg",
                  "ret_wo")


def kernel(x_prompt, x_sample, state_rwkv_shift, state_rwkv_wkv, state_ret, ln_ffn1, ff1_wg, ff1_wu, ff1_wd, ln_mix, ln_ffn2, ff2_wg, ff2_wu, ff2_wd, ln_final, rw_mu, rw_w0, rw_w1, rw_w2, rw_a0, rw_a1, rw_a2, rw_g1, rw_g2, rw_kk, rw_ka, rw_rk, rw_wr, rw_wk, rw_wv, rw_wo, rw_lnx_g, rw_lnx_b, ret_wq, ret_wk, ret_wv, ret_wg, ret_wo):
    w = dict(ln_ffn1=ln_ffn1, ff1_wg=ff1_wg, ff1_wu=ff1_wu, ff1_wd=ff1_wd, ln_mix=ln_mix,
             ln_ffn2=ln_ffn2, ff2_wg=ff2_wg, ff2_wu=ff2_wu, ff2_wd=ff2_wd, ln_final=ln_final,
             rw_mu=rw_mu, rw_w0=rw_w0, rw_w1=rw_w1, rw_w2=rw_w2, rw_a0=rw_a0, rw_a1=rw_a1,
             rw_a2=rw_a2, rw_g1=rw_g1, rw_g2=rw_g2, rw_kk=rw_kk, rw_ka=rw_ka, rw_rk=rw_rk,
             rw_wr=rw_wr, rw_wk=rw_wk, rw_wv=rw_wv, rw_wo=rw_wo, rw_lnx_g=rw_lnx_g,
             rw_lnx_b=rw_lnx_b, ret_wq=ret_wq, ret_wk=ret_wk, ret_wv=ret_wv, ret_wg=ret_wg,
             ret_wo=ret_wo)
    for name in MATMUL_WEIGHTS:
        w[name] = w[name].astype(BF16)
    B, L, D = x_prompt.shape
    Bs, Ls, _ = x_sample.shape
    y_p, p_shift, p_wkv, p_ret = _trunk(x_prompt.reshape(B * L, D), B, jnp.arange(L, dtype=F32),
                                        None, None, None, w, PROMPT_TILES)
    pos_s = PAST_LEN + jnp.arange(Ls, dtype=F32)
    y_s, s_shift, s_wkv, s_ret = _trunk(x_sample.reshape(Bs * Ls, D), Bs, pos_s,
                                        state_rwkv_shift, state_rwkv_wkv, state_ret, w, SAMPLE_TILES)
    return (y_p.reshape(B, L, D), y_s.reshape(Bs, Ls, D), p_shift, p_wkv, p_ret, s_shift, s_wkv, s_ret)
```

```python
import functools
import math

import numpy as np
import jax
import jax.numpy as jnp
from jax import lax
from jax.experimental import pallas as pl
from jax.experimental.pallas import tpu as pltpu

F32 = jnp.float32
BF16 = jnp.bfloat16
HIGHEST = lax.Precision.HIGHEST

DEPTH = 2
N_MIXERS = 2
RW_HEADS = 16
RW_HEAD = 64
RW_GN_EPS = 64e-5
RET_HEADS = 4
RET_DK = 256
RET_DV = 512
RET_CHUNK = 128
RET_GN_EPS = 1e-6
ROPE_BASE = 10000.0
FFN_RES = 0.5
RMS_EPS = 1e-6
PAST_LEN = 16384

WKV_CHUNK = 64
V7X_VMEM_CAP_BYTES = 60000 * 1024
SUBLANES = 8

NT_DIMS = (((1,), (1,)), ((), ()))
TN_DIMS = (((0,), (0,)), ((), ()))


def _vmem_limit(block_bytes, scratch_bytes=0, temp_bytes=0):
    return int(min(V7X_VMEM_CAP_BYTES, 2 * block_bytes + scratch_bytes + temp_bytes + (4 << 20)))


def _nbytes(shape, dtype):
    return int(np.prod(shape)) * jnp.dtype(dtype).itemsize


def _resident(shape, index_map):
    return pl.BlockSpec(shape, index_map, pipeline_mode=pl.Buffered(1))


def _rms(x, g):
    return x * lax.rsqrt(jnp.mean(x * x, axis=-1, keepdims=True) + RMS_EPS) * g


def _head_norm(y, eps):
    mu = jnp.mean(y, axis=-1, keepdims=True)
    yc = y - mu
    return yc * lax.rsqrt(jnp.mean(yc * yc, axis=-1, keepdims=True) + eps)


def _mm(a, b):
    return jnp.dot(a.astype(BF16), b.astype(BF16), preferred_element_type=F32)


def _mm_nt(a, b):
    return lax.dot_general(a.astype(BF16), b.astype(BF16), NT_DIMS, preferred_element_type=F32)


def _mm_tn(a, b):
    return lax.dot_general(a.astype(BF16), b.astype(BF16), TN_DIMS, preferred_element_type=F32)


def _ffn_body(*refs, ff_tile, final):
    if final:
        x_ref, g_ref, wg_ref, wu_ref, wd_ref, gf_ref, o_ref = refs
    else:
        x_ref, g_ref, wg_ref, wu_ref, wd_ref, o_ref = refs
    x = x_ref[...]
    h = _rms(x, g_ref[...]).astype(BF16)
    n_tiles = wg_ref.shape[1] // ff_tile

    def activation(j):
        sl = slice(j * ff_tile, (j + 1) * ff_tile)
        gate = jnp.dot(h, wg_ref[:, sl], preferred_element_type=F32)
        up = jnp.dot(h, wu_ref[:, sl], preferred_element_type=F32)
        return (gate * jax.nn.sigmoid(gate) * up).astype(BF16)

    act = activation(0)
    acc = None
    for j in range(n_tiles):
        nxt = activation(j + 1) if j + 1 < n_tiles else None
        part = jnp.dot(act, wd_ref[j * ff_tile:(j + 1) * ff_tile, :], preferred_element_type=F32)
        acc = part if acc is None else acc + part
        act = nxt
    y = x + FFN_RES * acc
    if final:
        y = _rms(y, gf_ref[...])
    o_ref[...] = y


def _ffn(x, ln_g, wg, wu, wd, ln_final=None, *, tm, tf):
    M, D = x.shape
    FF = wg.shape[1]
    final = ln_final is not None
    row = lambda i: (i, 0)
    const = lambda i: (0, 0)
    in_specs = [pl.BlockSpec((tm, D), row), _resident((1, D), const), _resident((D, FF), const),
                _resident((D, FF), const), _resident((FF, D), const)]
    args = [x, ln_g.reshape(1, D), wg, wu, wd]
    if final:
        in_specs.append(_resident((1, D), const))
        args.append(ln_final.reshape(1, D))
    streamed = 2 * _nbytes((tm, D), F32)
    resident = 3 * _nbytes((D, FF), BF16)
    temps = 6 * _nbytes((tm, tf), F32) + 3 * _nbytes((tm, D), F32)
    return pl.pallas_call(
        functools.partial(_ffn_body, ff_tile=tf, final=final),
        grid=(M // tm,), in_specs=in_specs, out_specs=pl.BlockSpec((tm, D), row),
        out_shape=jax.ShapeDtypeStruct((M, D), F32),
        compiler_params=pltpu.CompilerParams(dimension_semantics=("parallel",),
                                             vmem_limit_bytes=_vmem_limit(streamed, resident, temps)),
        name="ffn_final" if final else "ffn",
    )(*args)


def _rwkv_pre_body(x_ref, lng_ref, prev_ref, mu_ref, w0_ref, a0_ref, wr_ref, wk_ref, wv_ref,
                   w1_ref, w2_ref, a1_ref, a2_ref, g1_ref, g2_ref,
                   r_ref, k_ref, v_ref, lw_ref, a_ref, g_ref, so_ref, *scratch, seq):
    h = _rms(x_ref[...], lng_ref[...])
    if seq:
        (carry,) = scratch
        rows = h.shape[0]

        @pl.when(pl.program_id(1) == 0)
        def _():
            carry[...] = prev_ref[0]

        row = lax.broadcasted_iota(jnp.int32, h.shape, 0)
        xprev = jnp.where(row == 0, carry[...], pltpu.roll(h, 1, 0))
        last = h[rows - 1:rows, :]
        carry[...] = last
        so_ref[0] = last
    else:
        xprev = prev_ref[...]
        so_ref[...] = h
    xx = xprev - h
    xr, xw, xk, xv, xa, xg = (h + xx * mu_ref[i:i + 1, :] for i in range(6))
    r_ref[...] = _mm(xr, wr_ref[...])
    k_ref[...] = _mm(xk, wk_ref[...])
    v_ref[...] = _mm(xv, wv_ref[...])
    z = w0_ref[...] + _mm(jnp.tanh(_mm(xw, w1_ref[...])), w2_ref[...])
    lw_ref[...] = -math.exp(-0.5) * jax.nn.sigmoid(z)
    a_ref[...] = jax.nn.sigmoid(a0_ref[...] + _mm(_mm(xa, a1_ref[...]), a2_ref[...]))
    g_ref[...] = _mm(jax.nn.sigmoid(_mm(xg, g1_ref[...])), g2_ref[...])


def _rwkv_pre(x, prev, p, *, n_seq, tile):
    M, D = x.shape
    L = M // n_seq
    seq = L > 1
    vec = lambda a: a.reshape(1, D)
    weights = [p["wr"], p["wk"], p["wv"], p["w1"], p["w2"], p["a1"], p["a2"], p["g1"], p["g2"]]
    if seq:
        nl = L // tile
        grid = (n_seq, nl)
        row = lambda b, l: (b * nl + l, 0)
        const = lambda b, l: (0, 0)
        prev_arg = prev.reshape(n_seq, 1, D)
        prev_spec = pl.BlockSpec((1, 1, D), lambda b, l: (b, 0, 0))
        so_shape = jax.ShapeDtypeStruct((n_seq, 1, D), F32)
        so_spec = pl.BlockSpec((1, 1, D), lambda b, l: (b, 0, 0))
        scratch = [pltpu.VMEM((1, D), F32)]
        sem = ("parallel", "arbitrary")
    else:
        grid = (M // tile,)
        row = lambda i: (i, 0)
        const = lambda i: (0, 0)
        prev_arg = prev
        prev_spec = pl.BlockSpec((tile, D), row)
        so_shape = jax.ShapeDtypeStruct((M, D), F32)
        so_spec = pl.BlockSpec((tile, D), row)
        scratch = []
        sem = ("parallel",)
    tok = pl.BlockSpec((tile, D), row)
    full = lambda a: _resident(a.shape, const)
    in_specs = ([tok, full(vec(p["ln"])), prev_spec, full(p["mu"]), full(vec(p["w0"])), full(vec(p["a0"]))]
                + [full(w) for w in weights])
    streamed = 8 * _nbytes((tile, D), F32)
    resident = sum(_nbytes(w.shape, BF16) for w in weights)
    outs = pl.pallas_call(
        functools.partial(_rwkv_pre_body, seq=seq),
        grid=grid, in_specs=in_specs,
        out_specs=[tok] * 6 + [so_spec],
        out_shape=[jax.ShapeDtypeStruct((M, D), F32)] * 6 + [so_shape],
        scratch_shapes=scratch,
        compiler_params=pltpu.CompilerParams(
            dimension_semantics=sem,
            vmem_limit_bytes=_vmem_limit(streamed, resident, 10 * _nbytes((tile, D), F32))),
        name="rwkv_pre",
    )(x, vec(p["ln"]), prev_arg, p["mu"], vec(p["w0"]), vec(p["a0"]), *weights)
    r, k, v, lw, a, g, so = outs
    return r, k, v, lw, a, g, so.reshape(n_seq, D) if seq else so


def _wkv_chunk_body(r_ref, k_ref, v_ref, lw_ref, a_ref, kkw_ref, kaw_ref, rk_ref, lg_ref, lb_ref,
                    y_ref, s_ref):
    C = WKV_CHUNK
    N = RW_HEAD

    @pl.when(pl.program_id(1) == 0)
    def _():
        s_ref[...] = jnp.zeros_like(s_ref)

    r = r_ref[...]
    k = k_ref[...]
    v = v_ref[...]
    lw = lw_ref[...]
    a = a_ref[...]
    ii = lax.broadcasted_iota(jnp.int32, (C, C), 0)
    jj = lax.broadcasted_iota(jnp.int32, (C, C), 1)
    i2 = lax.broadcasted_iota(jnp.int32, (C, 2 * C), 0)
    j2 = lax.broadcasted_iota(jnp.int32, (C, 2 * C), 1)
    strict = ii > jj
    right_strict = (j2 >= C) & (i2 > j2 - C)
    both_incl = i2 >= jnp.where(j2 >= C, j2 - C, j2)
    cl = jnp.dot((ii >= jj).astype(F32), lw, precision=HIGHEST, preferred_element_type=F32)
    gl = cl[C - 1:C, :]
    e_cl = jnp.exp(cl)
    e_ncl = jnp.exp(-cl)
    e_ce = jnp.exp(cl - lw)
    e_g = jnp.exp(gl - cl)
    dg = jnp.exp(gl)
    kkraw = k * kkw_ref[...]
    kmod = k * (1.0 + (a - 1.0) * kaw_ref[...])
    rkk = r * kmod * rk_ref[...]
    rt = r * e_cl
    kt = kmod * e_ncl
    kg = kmod * e_g
    zeros = jnp.zeros((C, N), F32)
    lnx_g = lg_ref[...]
    lnx_b = lb_ref[...]
    heads = range(RW_HEADS)
    sls = [slice(hd * N, (hd + 1) * N) for hd in heads]
    S = [s_ref[0, hd] for hd in heads]
    vh = [v[:, sl] for sl in sls]
    bh, lhs2, rhs2 = [], [], []
    for sl in sls:
        kkh = kkraw[:, sl]
        nrm = jnp.sqrt(jnp.sum(kkh * kkh, axis=-1, keepdims=True))
        kkn = kkh / jnp.maximum(nrm, 1e-12)
        bh.append(kkn * a[:, sl])
        lhs2.append(jnp.concatenate([-kkn * e_ce[:, sl], rt[:, sl]], axis=0).astype(BF16))
        rhs2.append(jnp.concatenate([bh[-1] * e_ncl[:, sl], kt[:, sl]], axis=0).astype(BF16))
    P = [lax.dot_general(lhs2[hd], rhs2[hd], NT_DIMS, preferred_element_type=F32) for hd in heads]
    LS = [lax.dot_general(lhs2[hd], S[hd].astype(BF16), NT_DIMS, preferred_element_type=F32) for hd in heads]
    X = [LS[hd][:C] + _mm(jnp.where(right_strict, P[hd][:C], 0.0), jnp.concatenate([zeros, vh[hd]], axis=0))
         for hd in heads]
    Pk = [jnp.where(strict, P[hd][:C, :C], 0.0) for hd in heads]
    span = 1
    while span < C:
        X = [X[hd] + _mm(Pk[hd], X[hd]) for hd in heads]
        span *= 2
        if span < C:
            Pk = [_mm(Pk[hd], Pk[hd]) for hd in heads]
    uv = [jnp.concatenate([X[hd], vh[hd]], axis=0) for hd in heads]
    Y = [LS[hd][C:] + _mm(jnp.where(both_incl, P[hd][C:], 0.0), uv[hd]) for hd in heads]
    s_new = [S[hd] * dg[:, sls[hd]]
             + _mm_tn(uv[hd], jnp.concatenate([bh[hd] * e_g[:, sls[hd]], kg[:, sls[hd]]], axis=0))
             for hd in heads]
    y_out = [_head_norm(Y[hd], RW_GN_EPS) * lnx_g[:, sls[hd]] + lnx_b[:, sls[hd]]
             + jnp.sum(rkk[:, sls[hd]], axis=-1, keepdims=True) * vh[hd] for hd in heads]
    for hd in heads:
        s_ref[0, hd] = s_new[hd]
    y_ref[...] = jnp.concatenate(y_out, axis=-1)


def _wkv_chunked(r, k, v, lw, a, p, *, n_seq):
    M, D = r.shape
    C = WKV_CHUNK
    nc = M // n_seq // C
    tok = pl.BlockSpec((C, D), lambda b, c: (b * nc + c, 0))
    par = _resident((1, D), lambda b, c: (0, 0))
    vec = lambda t: t.reshape(1, D)
    st_spec = pl.BlockSpec((1, RW_HEADS, RW_HEAD, RW_HEAD), lambda b, c: (b, 0, 0, 0))
    blocks = 6 * _nbytes((C, D), F32) + _nbytes((RW_HEADS, RW_HEAD, 128), F32)
    return pl.pallas_call(
        _wkv_chunk_body, grid=(n_seq, nc),
        in_specs=[tok] * 5 + [par] * 5,
        out_specs=[tok, st_spec],
        out_shape=[jax.ShapeDtypeStruct((M, D), F32),
                   jax.ShapeDtypeStruct((n_seq, RW_HEADS, RW_HEAD, RW_HEAD), F32)],
        compiler_params=pltpu.CompilerParams(dimension_semantics=("parallel", "arbitrary"),
                                             vmem_limit_bytes=_vmem_limit(blocks, 0, 24 * _nbytes((C, D), F32))),
        name="wkv_chunk",
    )(r, k, v, lw, a, vec(p["kk"]), vec(p["ka"]), vec(p["rk"]), vec(p["lnx_g"]), vec(p["lnx_b"]))


def _wkv_step_body(r_ref, k_ref, v_ref, lw_ref, a_ref, kkw_ref, kaw_ref, rk_ref, lg_ref, lb_ref, s_ref,
                   y_ref, so_ref, y_scr):
    r = r_ref[0]
    k = k_ref[0]
    v = v_ref[0]
    a = a_ref[0]
    d = jnp.exp(lw_ref[0])
    kkh = k * kkw_ref[0]
    nrm = jnp.sqrt(jnp.sum(kkh * kkh, axis=0, keepdims=True))
    kkn = kkh / jnp.maximum(nrm, 1e-12)
    kmod = k * (1.0 + (a - 1.0) * kaw_ref[0])
    bh = kkn * a

    def value_channel(i, carry):
        S = s_ref[0, i]
        sa = -jnp.sum(S * kkn, axis=0, keepdims=True)
        s_new = S * d + sa * bh + v_ref[0, pl.ds(i, 1), :] * kmod
        so_ref[0, i] = s_new
        y_scr[pl.ds(i, 1), :] = jnp.sum(s_new * r, axis=0, keepdims=True)
        return carry

    lax.fori_loop(0, RW_HEAD, value_channel, 0, unroll=8)
    y = y_scr[...]
    mu = jnp.mean(y, axis=0, keepdims=True)
    yc = y - mu
    yn = yc * lax.rsqrt(jnp.mean(yc * yc, axis=0, keepdims=True) + RW_GN_EPS)
    y_ref[0] = yn * lg_ref[0] + lb_ref[0] + jnp.sum(r * kmod * rk_ref[0], axis=0, keepdims=True) * v


def _wkv_step(r, k, v, lw, a, p, state):
    B, D = r.shape
    H, N = RW_HEADS, RW_HEAD
    lanes = lambda t: t.T.reshape(H, N, B)
    par = lambda t: jnp.broadcast_to(t.reshape(H, N, 1), (H, N, B))
    vec = pl.BlockSpec((1, N, B), lambda h: (h, 0, 0))
    st = pl.BlockSpec((1, N, N, B), lambda h: (h, 0, 0, 0))
    blocks = 2 * _nbytes((N, N, B), F32) + 11 * _nbytes((N, B), F32)
    y, s_new = pl.pallas_call(
        _wkv_step_body, grid=(H,),
        in_specs=[vec] * 10 + [st],
        out_specs=[vec, st],
        out_shape=[jax.ShapeDtypeStruct((H, N, B), F32), jax.ShapeDtypeStruct((H, N, N, B), F32)],
        scratch_shapes=[pltpu.VMEM((N, B), F32)],
        compiler_params=pltpu.CompilerParams(dimension_semantics=("parallel",),
                                             vmem_limit_bytes=_vmem_limit(blocks, _nbytes((N, B), F32), 4 << 20)),
        name="wkv_step",
    )(lanes(r), lanes(k), lanes(v), lanes(lw), lanes(a),
      par(p["kk"]), par(p["ka"]), par(p["rk"]), par(p["lnx_g"]), par(p["lnx_b"]),
      jnp.transpose(state, (1, 2, 3, 0)))
    return y.reshape(D, B).T, jnp.transpose(s_new, (3, 0, 1, 2))


def _rwkv_post_body(x_ref, y_ref, g_ref, wo_ref, o_ref):
    o_ref[...] = x_ref[...] + _mm(y_ref[...] * g_ref[...], wo_ref[...])


def _rwkv_post(x, y, g, wo, *, tm):
    M, D = x.shape
    tok = pl.BlockSpec((tm, D), lambda i: (i, 0))
    streamed = 4 * _nbytes((tm, D), F32)
    return pl.pallas_call(
        _rwkv_post_body, grid=(M // tm,),
        in_specs=[tok, tok, tok, _resident(wo.shape, lambda i: (0, 0))],
        out_specs=tok, out_shape=jax.ShapeDtypeStruct((M, D), F32),
        compiler_params=pltpu.CompilerParams(
            dimension_semantics=("parallel",),
            vmem_limit_bytes=_vmem_limit(streamed, _nbytes(wo.shape, BF16), 3 * _nbytes((tm, D), F32))),
        name="rwkv_post",
    )(x, y, g, wo)


def _ret_pre_body(x_ref, lng_ref, cos_ref, sin_ref, wq_ref, wk_ref, wv_ref, wg_ref,
                  q_ref, k_ref, v_ref, gate_ref, h_scr):
    @pl.when(pl.program_id(1) == 0)
    def _():
        h_scr[...] = _rms(x_ref[...], lng_ref[...]).astype(BF16)

    h = h_scr[...]
    cos = cos_ref[...]
    sin = sin_ref[...]
    half = RET_DK // 2

    def rotate(t, out_ref):
        t1, t2 = t[:, :half], t[:, half:]
        out_ref[:, :half] = t1 * cos - t2 * sin
        out_ref[:, half:] = t1 * sin + t2 * cos

    rotate(jnp.dot(h, wq_ref[...], preferred_element_type=F32), q_ref)
    rotate(jnp.dot(h, wk_ref[...], preferred_element_type=F32) * (RET_DK ** -0.5), k_ref)
    v_ref[...] = jnp.dot(h, wv_ref[...], preferred_element_type=F32)
    gz = jnp.dot(h, wg_ref[...], preferred_element_type=F32)
    gate_ref[...] = gz * jax.nn.sigmoid(gz)


def _ret_pre(x, ln_g, cos, sin, p, *, tm):
    M, D = x.shape
    H, DK, DV = RET_HEADS, RET_DK, RET_DV
    n_tab = cos.shape[0] // tm
    tab = pl.BlockSpec((tm, DK // 2), lambda i, h: (i % n_tab, 0))
    col = lambda w: pl.BlockSpec((D, w), lambda i, h: (0, h))
    out = lambda w: pl.BlockSpec((tm, w), lambda i, h: (i, h))
    blocks = (_nbytes((tm, D), F32) + 2 * _nbytes((tm, DK // 2), F32) + 2 * _nbytes((D, DK), BF16)
              + 2 * _nbytes((D, DV), BF16) + 2 * _nbytes((tm, DK), F32) + 2 * _nbytes((tm, DV), F32))
    return pl.pallas_call(
        _ret_pre_body, grid=(M // tm, H),
        in_specs=[pl.BlockSpec((tm, D), lambda i, h: (i, 0)), pl.BlockSpec((1, D), lambda i, h: (0, 0)),
                  tab, tab, col(DK), col(DK), col(DV), col(DV)],
        out_specs=[out(DK), out(DK), out(DV), out(DV)],
        out_shape=[jax.ShapeDtypeStruct((M, H * DK), F32), jax.ShapeDtypeStruct((M, H * DK), F32),
                   jax.ShapeDtypeStruct((M, H * DV), F32), jax.ShapeDtypeStruct((M, H * DV), F32)],
        scratch_shapes=[pltpu.VMEM((tm, D), BF16)],
        compiler_params=pltpu.CompilerParams(
            dimension_semantics=("parallel", "arbitrary"),
            vmem_limit_bytes=_vmem_limit(blocks, _nbytes((tm, D), BF16), 4 * _nbytes((tm, DV), F32))),
        name="ret_pre",
    )(x, ln_g.reshape(1, D), cos, sin, p["wq"], p["wk"], p["wv"], p["wg"])


def _ret_log_gamma():
    return jnp.log1p(-jnp.exp2(-5.0 - jnp.arange(RET_HEADS, dtype=F32)))


def _ret_chunk_body(lg_ref, q_ref, k_ref, v_ref, o_ref, s_ref, *, n_sub):
    C, DK, DV = RET_CHUNK, RET_DK, RET_DV

    @pl.when(pl.program_id(1) == 0)
    def _():
        s_ref[...] = jnp.zeros_like(s_ref)

    ii = lax.broadcasted_iota(jnp.int32, (C, C), 0)
    jj = lax.broadcasted_iota(jnp.int32, (C, C), 1)
    diff = jnp.maximum((ii - jj).astype(F32), 0.0)
    idx = lax.broadcasted_iota(jnp.int32, (C, 1), 0).astype(F32)
    heads = range(RET_HEADS)
    units = [(hd, cc) for hd in heads for cc in range(n_sub)]
    lg = [lg_ref[hd] for hd in heads]
    dmat = [jnp.where(ii >= jj, jnp.exp(lg[hd] * diff), 0.0) for hd in heads]
    q_dec = [jnp.exp(lg[hd] * (idx + 1.0)) for hd in heads]
    k_dec = [jnp.exp(lg[hd] * (C - 1.0 - idx)) for hd in heads]
    s_dec = [jnp.exp(jnp.full((1, DV), lg[hd] * C, F32)) for hd in heads]
    rows = lambda cc: slice(cc * C, (cc + 1) * C)
    q = {u: q_ref[rows(u[1]), u[0] * DK:(u[0] + 1) * DK] for u in units}
    k = {u: k_ref[rows(u[1]), u[0] * DK:(u[0] + 1) * DK] for u in units}
    v = {u: v_ref[rows(u[1]), u[0] * DV:(u[0] + 1) * DV].astype(BF16) for u in units}
    inner = {u: _mm_nt(q[u], k[u]) * dmat[u[0]] for u in units}
    kv = {u: lax.dot_general((k[u] * k_dec[u[0]]).astype(BF16), v[u], TN_DIMS, preferred_element_type=F32)
          for u in units}
    state = {}
    for hd in heads:
        S = s_ref[0, hd]
        for cc in range(n_sub):
            state[(hd, cc)] = S
            S = S * s_dec[hd] + kv[(hd, cc)]
        s_ref[0, hd] = S
    for u in units:
        hd, cc = u
        o_ref[rows(cc), hd * DV:(hd + 1) * DV] = (
            jnp.dot(inner[u].astype(BF16), v[u], preferred_element_type=F32)
            + _mm(q[u] * q_dec[hd], state[u]))


def _ret_chunked(q, k, v, *, n_seq, n_sub):
    M = q.shape[0]
    H, DK, DV, C = RET_HEADS, RET_DK, RET_DV, RET_CHUNK
    ns = M // n_seq // (C * n_sub)
    tok = lambda w: pl.BlockSpec((C * n_sub, w), lambda b, c: (b * ns + c, 0))
    st = pl.BlockSpec((1, H, DK, DV), lambda b, c: (b, 0, 0, 0))
    blocks = (2 * _nbytes((C * n_sub, H * DK), F32) + 2 * _nbytes((C * n_sub, H * DV), F32)
              + _nbytes((H, DK, DV), F32))
    return pl.pallas_call(
        functools.partial(_ret_chunk_body, n_sub=n_sub), grid=(n_seq, ns),
        in_specs=[pl.BlockSpec(memory_space=pltpu.SMEM), tok(H * DK), tok(H * DK), tok(H * DV)],
        out_specs=[tok(H * DV), st],
        out_shape=[jax.ShapeDtypeStruct((M, H * DV), F32), jax.ShapeDtypeStruct((n_seq, H, DK, DV), F32)],
        compiler_params=pltpu.CompilerParams(dimension_semantics=("parallel", "arbitrary"),
                                             vmem_limit_bytes=_vmem_limit(blocks, 0, 6 * _nbytes((H, DK, DV), F32))),
        name="ret_chunk",
    )(_ret_log_gamma(), q, k, v)


def _ret_step_body(lg_ref, q_ref, k_ref, v_ref, s_ref, o_ref, so_ref):
    lg = lg_ref[pl.program_id(1)]
    q = q_ref[0]
    k = k_ref[0]
    v = v_ref[0]
    bt = q.shape[0]
    gam_k = jnp.exp(jnp.full((1, RET_DK), lg, F32))
    gam_v = jnp.exp(jnp.full((1, RET_DV), lg, F32))
    qg = q * gam_k
    vb = v.astype(BF16)
    rowid = lax.broadcasted_iota(jnp.int32, (SUBLANES, RET_DK), 0)
    inner = jnp.sum(q * k, axis=-1, keepdims=True)
    for grp in range(bt // SUBLANES):
        rs = slice(grp * SUBLANES, (grp + 1) * SUBLANES)
        o8 = inner[rs] * v[rs]
        for j in range(SUBLANES):
            b = grp * SUBLANES + j
            m = rowid == j
            S = s_ref[b, 0]
            o8 = o8 + _mm(jnp.where(m, qg[rs], 0.0), S)
            so_ref[b, 0] = S * gam_v + lax.dot_general(jnp.where(m, k[rs], 0.0).astype(BF16), vb[rs],
                                                       TN_DIMS, preferred_element_type=F32)
        o_ref[0, rs, :] = o8


def _ret_step(q, k, v, state, *, bt):
    B = q.shape[0]
    H, DK, DV = RET_HEADS, RET_DK, RET_DV
    heads = lambda t, w: t.reshape(B, H, w).transpose(1, 0, 2)
    tok = lambda w: pl.BlockSpec((1, bt, w), lambda i, h: (h, i, 0))
    st = pl.BlockSpec((bt, 1, DK, DV), lambda i, h: (i, h, 0, 0))
    blocks = 2 * _nbytes((bt, DK, DV), F32) + 4 * _nbytes((bt, DV), F32)
    o, s_new = pl.pallas_call(
        _ret_step_body, grid=(B // bt, H),
        in_specs=[pl.BlockSpec(memory_space=pltpu.SMEM), tok(DK), tok(DK), tok(DV), st],
        out_specs=[tok(DV), st],
        out_shape=[jax.ShapeDtypeStruct((H, B, DV), F32), jax.ShapeDtypeStruct((B, H, DK, DV), F32)],
        compiler_params=pltpu.CompilerParams(dimension_semantics=("parallel", "parallel"),
                                             vmem_limit_bytes=_vmem_limit(blocks, 0, 4 * _nbytes((DK, DV), F32))),
        name="ret_step",
    )(_ret_log_gamma(), heads(q, DK), heads(k, DK), heads(v, DV), state)
    return o.transpose(1, 0, 2).reshape(B, H * DV), s_new


def _ret_post_body(x_ref, o_ref, gate_ref, wo_ref, out_ref):
    acc = x_ref[...]
    for hd in range(RET_HEADS):
        sl = slice(hd * RET_DV, (hd + 1) * RET_DV)
        acc = acc + _mm(gate_ref[:, sl] * _head_norm(o_ref[:, sl], RET_GN_EPS), wo_ref[sl, :])
    out_ref[...] = acc


def _ret_post(x, o, gate, wo, *, tm):
    M, D = x.shape
    W = o.shape[1]
    tok = lambda w: pl.BlockSpec((tm, w), lambda i: (i, 0))
    streamed = 2 * _nbytes((tm, D), F32) + 2 * _nbytes((tm, W), F32)
    return pl.pallas_call(
        _ret_post_body, grid=(M // tm,),
        in_specs=[tok(D), tok(W), tok(W), _resident(wo.shape, lambda i: (0, 0))],
        out_specs=tok(D), out_shape=jax.ShapeDtypeStruct((M, D), F32),
        compiler_params=pltpu.CompilerParams(
            dimension_semantics=("parallel",),
            vmem_limit_bytes=_vmem_limit(streamed, _nbytes(wo.shape, BF16), 3 * _nbytes((tm, W), F32))),
        name="ret_post",
    )(x, o, gate, wo)


def _rope_tables(pos):
    half = RET_DK // 2
    inv = ROPE_BASE ** (-jnp.arange(half, dtype=F32) / half)
    ang = pos.astype(F32)[:, None] * inv[None, :]
    return jnp.cos(ang), jnp.sin(ang)


def _trunk(x, n_seq, pos, shift0, wkv0, ret0, w, tiles):
    M, D = x.shape
    L = M // n_seq
    cos, sin = _rope_tables(pos)
    if L < tiles["ret_pre"]:
        cos, sin = jnp.tile(cos, (M // L, 1)), jnp.tile(sin, (M // L, 1))
    new_shift, new_wkv, new_ret = [], [], []
    for i in range(DEPTH):
        x = _ffn(x, w["ln_ffn1"][i], w["ff1_wg"][i], w["ff1_wu"][i], w["ff1_wd"][i],
                 tm=tiles["ffn_m"], tf=tiles["ffn_f"])
        j = i // N_MIXERS
        if i % N_MIXERS == 0:
            p = {n: w["rw_" + n][j] for n in ("mu", "w0", "w1", "w2", "a0", "a1", "a2", "g1", "g2", "kk", "ka",
                                              "rk", "wr", "wk", "wv", "wo", "lnx_g", "lnx_b")}
            p["ln"] = w["ln_mix"][i]
            prev = jnp.zeros((n_seq, D), F32) if shift0 is None else shift0[j]
            r, k, v, lw, a, g, sh = _rwkv_pre(x, prev, p, n_seq=n_seq, tile=tiles["rwkv_pre"])
            if L > 1:
                y, st = _wkv_chunked(r, k, v, lw, a, p, n_seq=n_seq)
            else:
                y, st = _wkv_step(r, k, v, lw, a, p, wkv0[j])
            x = _rwkv_post(x, y, g, p["wo"], tm=tiles["post"])
            new_shift.append(sh)
            new_wkv.append(st)
        else:
            p = {n: w["ret_" + n][j] for n in ("wq", "wk", "wv", "wg", "wo")}
            q, k, v, gate = _ret_pre(x, w["ln_mix"][i], cos, sin, p, tm=tiles["ret_pre"])
            if L > 1:
                o, st = _ret_chunked(q, k, v, n_seq=n_seq, n_sub=tiles["ret_sub"])
            else:
                o, st = _ret_step(q, k, v, ret0[j], bt=tiles["ret_step"])
            x = _ret_post(x, o, gate, p["wo"], tm=tiles["post"])
            new_ret.append(st)
        x = _ffn(x, w["ln_ffn2"][i], w["ff2_wg"][i], w["ff2_wu"][i], w["ff2_wd"][i],
                 w["ln_final"] if i == DEPTH - 1 else None, tm=tiles["ffn_m"], tf=tiles["ffn_f"])
    return x, jnp.stack(new_shift), jnp.stack(new_wkv), jnp.stack(new_ret)


PROMPT_TILES = dict(ffn_m=1024, ffn_f=256, rwkv_pre=256, post=256, ret_pre=512, ret_sub=2)
SAMPLE_TILES = dict(ffn_m=128, ffn_f=256, rwkv_pre=128, post=128, ret_pre=128, ret_step=8)

MATMUL_WEIGHTS = ("ff1_wg", "ff1_wu", "ff1_wd", "ff2_wg", "ff2_wu", "ff2_wd", "rw_w1", "rw_w2", "rw_a1", "rw_a2",
                  "rw_g1", "rw_g2", "rw_wr", "rw_wk", "rw_wv", "rw_wo", "ret_wq", "ret_wk", "ret_wv", "ret_wg",
                  "ret_wo")


def kernel(x_prompt, x_sample, state_rwkv_shift, state_rwkv_wkv, state_ret, ln_ffn1, ff1_wg, ff1_wu, ff1_wd, ln_mix, ln_ffn2, ff2_wg, ff2_wu, ff2_wd, ln_final, rw_mu, rw_w0, rw_w1, rw_w2, rw_a0, rw_a1, rw_a2, rw_g1, rw_g2, rw_kk, rw_ka, rw_rk, rw_wr, rw_wk, rw_wv, rw_wo, rw_lnx_g, rw_lnx_b, ret_wq, ret_wk, ret_wv, ret_wg, ret_wo):
    w = dict(ln_ffn1=ln_ffn1, ff1_wg=ff1_wg, ff1_wu=ff1_wu, ff1_wd=ff1_wd, ln_mix=ln_mix,
             ln_ffn2=ln_ffn2, ff2_wg=ff2_wg, ff2_wu=ff2_wu, ff2_wd=ff2_wd, ln_final=ln_final,
             rw_mu=rw_mu, rw_w0=rw_w0, rw_w1=rw_w1, rw_w2=rw_w2, rw_a0=rw_a0, rw_a1=rw_a1,
             rw_a2=rw_a2, rw_g1=rw_g1, rw_g2=rw_g2, rw_kk=rw_kk, rw_ka=rw_ka, rw_rk=rw_rk,
             rw_wr=rw_wr, rw_wk=rw_wk, rw_wv=rw_wv, rw_wo=rw_wo, rw_lnx_g=rw_lnx_g,
             rw_lnx_b=rw_lnx_b, ret_wq=ret_wq, ret_wk=ret_wk, ret_wv=ret_wv, ret_wg=ret_wg,
             ret_wo=ret_wo)
    for name in MATMUL_WEIGHTS:
        w[name] = [w[name][i].astype(BF16) for i in range(w[name].shape[0])]
    B, L, D = x_prompt.shape
    Bs, Ls, _ = x_sample.shape
    y_p, p_shift, p_wkv, p_ret = _trunk(x_prompt.reshape(B * L, D), B, jnp.arange(L, dtype=F32),
                                        None, None, None, w, PROMPT_TILES)
    pos_s = PAST_LEN + jnp.arange(Ls, dtype=F32)
    y_s, s_shift, s_wkv, s_ret = _trunk(x_sample.reshape(Bs * Ls, D), Bs, pos_s,
                                        state_rwkv_shift, state_rwkv_wkv, state_ret, w, SAMPLE_TILES)
    return (y_p.reshape(B, L, D), y_s.reshape(Bs, Ls, D), p_shift, p_wkv, p_ret, s_shift, s_wkv, s_ret)
```

```python
import functools
import math

import numpy as np
import jax
import jax.numpy as jnp
from jax import lax
from jax.experimental import pallas as pl
from jax.experimental.pallas import tpu as pltpu

F32 = jnp.float32
BF16 = jnp.bfloat16
HIGHEST = lax.Precision.HIGHEST

DEPTH = 2
N_MIXERS = 2
RW_HEADS = 16
RW_HEAD = 64
RW_GN_EPS = 64e-5
RET_HEADS = 4
RET_DK = 256
RET_DV = 512
RET_CHUNK = 128
RET_GN_EPS = 1e-6
ROPE_BASE = 10000.0
FFN_RES = 0.5
RMS_EPS = 1e-6
PAST_LEN = 16384

WKV_CHUNK = 64
V7X_VMEM_CAP_BYTES = 60000 * 1024
SUBLANES = 8

NT_DIMS = (((1,), (1,)), ((), ()))
TN_DIMS = (((0,), (0,)), ((), ()))


def _vmem_limit(block_bytes, scratch_bytes=0, temp_bytes=0):
    return int(min(V7X_VMEM_CAP_BYTES, 2 * block_bytes + scratch_bytes + temp_bytes + (4 << 20)))


def _nbytes(shape, dtype):
    return int(np.prod(shape)) * jnp.dtype(dtype).itemsize


def _resident(shape, index_map):
    return pl.BlockSpec(shape, index_map, pipeline_mode=pl.Buffered(1))


def _rms(x, g):
    return x * lax.rsqrt(jnp.mean(x * x, axis=-1, keepdims=True) + RMS_EPS) * g


def _head_norm(y, eps):
    mu = jnp.mean(y, axis=-1, keepdims=True)
    yc = y - mu
    return yc * lax.rsqrt(jnp.mean(yc * yc, axis=-1, keepdims=True) + eps)


def _mm(a, b):
    return jnp.dot(a.astype(BF16), b.astype(BF16), preferred_element_type=F32)


def _mm_nt(a, b):
    return lax.dot_general(a.astype(BF16), b.astype(BF16), NT_DIMS, preferred_element_type=F32)


def _mm_tn(a, b):
    return lax.dot_general(a.astype(BF16), b.astype(BF16), TN_DIMS, preferred_element_type=F32)


def _ffn_body(*refs, ff_tile, final):
    if final:
        x_ref, g_ref, wg_ref, wu_ref, wd_ref, gf_ref, o_ref = refs
    else:
        x_ref, g_ref, wg_ref, wu_ref, wd_ref, o_ref = refs
    x = x_ref[...]
    h = _rms(x, g_ref[...]).astype(BF16)
    n_tiles = wg_ref.shape[1] // ff_tile

    def activation(j):
        sl = slice(j * ff_tile, (j + 1) * ff_tile)
        gate = jnp.dot(h, wg_ref[:, sl], preferred_element_type=F32)
        up = jnp.dot(h, wu_ref[:, sl], preferred_element_type=F32)
        return (gate * jax.nn.sigmoid(gate) * up).astype(BF16)

    act = activation(0)
    acc = None
    for j in range(n_tiles):
        nxt = activation(j + 1) if j + 1 < n_tiles else None
        part = jnp.dot(act, wd_ref[j * ff_tile:(j + 1) * ff_tile, :], preferred_element_type=F32)
        acc = part if acc is None else acc + part
        act = nxt
    y = x + FFN_RES * acc
    if final:
        y = _rms(y, gf_ref[...])
    o_ref[...] = y


def _ffn(x, ln_g, wg, wu, wd, ln_final=None, *, tm, tf):
    M, D = x.shape
    FF = wg.shape[1]
    final = ln_final is not None
    row = lambda i: (i, 0)
    const = lambda i: (0, 0)
    in_specs = [pl.BlockSpec((tm, D), row), _resident((1, D), const), _resident((D, FF), const),
                _resident((D, FF), const), _resident((FF, D), const)]
    args = [x, ln_g.reshape(1, D), wg, wu, wd]
    if final:
        in_specs.append(_resident((1, D), const))
        args.append(ln_final.reshape(1, D))
    streamed = 2 * _nbytes((tm, D), F32)
    resident = 3 * _nbytes((D, FF), BF16)
    temps = 6 * _nbytes((tm, tf), F32) + 3 * _nbytes((tm, D), F32)
    return pl.pallas_call(
        functools.partial(_ffn_body, ff_tile=tf, final=final),
        grid=(M // tm,), in_specs=in_specs, out_specs=pl.BlockSpec((tm, D), row),
        out_shape=jax.ShapeDtypeStruct((M, D), F32),
        compiler_params=pltpu.CompilerParams(dimension_semantics=("parallel",),
                                             vmem_limit_bytes=_vmem_limit(streamed, resident, temps)),
        name="ffn_final" if final else "ffn",
    )(*args)


def _rwkv_pre_body(x_ref, lng_ref, prev_ref, mu_ref, w0_ref, a0_ref, wr_ref, wk_ref, wv_ref,
                   w1_ref, w2_ref, a1_ref, a2_ref, g1_ref, g2_ref,
                   r_ref, k_ref, v_ref, lw_ref, a_ref, g_ref, so_ref, *scratch, seq):
    h = _rms(x_ref[...], lng_ref[...])
    if seq:
        (carry,) = scratch
        rows = h.shape[0]

        @pl.when(pl.program_id(1) == 0)
        def _():
            carry[...] = prev_ref[0]

        row = lax.broadcasted_iota(jnp.int32, h.shape, 0)
        xprev = jnp.where(row == 0, carry[...], pltpu.roll(h, 1, 0))
        last = h[rows - 1:rows, :]
        carry[...] = last
        so_ref[0] = last
    else:
        xprev = prev_ref[...]
        so_ref[...] = h
    xx = xprev - h
    xr, xw, xk, xv, xa, xg = (h + xx * mu_ref[i:i + 1, :] for i in range(6))
    r_ref[...] = _mm(xr, wr_ref[...])
    k_ref[...] = _mm(xk, wk_ref[...])
    v_ref[...] = _mm(xv, wv_ref[...])
    z = w0_ref[...] + _mm(jnp.tanh(_mm(xw, w1_ref[...])), w2_ref[...])
    lw_ref[...] = -math.exp(-0.5) * jax.nn.sigmoid(z)
    a_ref[...] = jax.nn.sigmoid(a0_ref[...] + _mm(_mm(xa, a1_ref[...]), a2_ref[...]))
    g_ref[...] = _mm(jax.nn.sigmoid(_mm(xg, g1_ref[...])), g2_ref[...])


def _rwkv_pre(x, prev, p, *, n_seq, tile):
    M, D = x.shape
    L = M // n_seq
    seq = L > 1
    vec = lambda a: a.reshape(1, D)
    weights = [p["wr"], p["wk"], p["wv"], p["w1"], p["w2"], p["a1"], p["a2"], p["g1"], p["g2"]]
    if seq:
        nl = L // tile
        grid = (n_seq, nl)
        row = lambda b, l: (b * nl + l, 0)
        const = lambda b, l: (0, 0)
        prev_arg = prev.reshape(n_seq, 1, D)
        prev_spec = pl.BlockSpec((1, 1, D), lambda b, l: (b, 0, 0))
        so_shape = jax.ShapeDtypeStruct((n_seq, 1, D), F32)
        so_spec = pl.BlockSpec((1, 1, D), lambda b, l: (b, 0, 0))
        scratch = [pltpu.VMEM((1, D), F32)]
        sem = ("parallel", "arbitrary")
    else:
        grid = (M // tile,)
        row = lambda i: (i, 0)
        const = lambda i: (0, 0)
        prev_arg = prev
        prev_spec = pl.BlockSpec((tile, D), row)
        so_shape = jax.ShapeDtypeStruct((M, D), F32)
        so_spec = pl.BlockSpec((tile, D), row)
        scratch = []
        sem = ("parallel",)
    tok = pl.BlockSpec((tile, D), row)
    full = lambda a: _resident(a.shape, const)
    in_specs = ([tok, full(vec(p["ln"])), prev_spec, full(p["mu"]), full(vec(p["w0"])), full(vec(p["a0"]))]
                + [full(w) for w in weights])
    streamed = 8 * _nbytes((tile, D), F32)
    resident = sum(_nbytes(w.shape, BF16) for w in weights)
    outs = pl.pallas_call(
        functools.partial(_rwkv_pre_body, seq=seq),
        grid=grid, in_specs=in_specs,
        out_specs=[tok] * 6 + [so_spec],
        out_shape=[jax.ShapeDtypeStruct((M, D), F32)] * 6 + [so_shape],
        scratch_shapes=scratch,
        compiler_params=pltpu.CompilerParams(
            dimension_semantics=sem,
            vmem_limit_bytes=_vmem_limit(streamed, resident, 10 * _nbytes((tile, D), F32))),
        name="rwkv_pre",
    )(x, vec(p["ln"]), prev_arg, p["mu"], vec(p["w0"]), vec(p["a0"]), *weights)
    r, k, v, lw, a, g, so = outs
    return r, k, v, lw, a, g, so.reshape(n_seq, D) if seq else so


def _wkv_chunk_body(r_ref, k_ref, v_ref, lw_ref, a_ref, kkw_ref, kaw_ref, rk_ref, lg_ref, lb_ref,
                    y_ref, s_ref, sbd_scr, *, n_chunks):
    C = WKV_CHUNK
    N = RW_HEAD
    W = 2 * N
    n_s = r_ref.shape[0]
    T = n_s * C
    c = pl.program_id(1)

    @pl.when(c == 0)
    def _():
        sbd_scr[...] = jnp.zeros_like(sbd_scr)

    rows_of = lambda ref: ref[...].reshape(T, ref.shape[-1])
    r = rows_of(r_ref)
    k = rows_of(k_ref)
    v = rows_of(v_ref)
    lw = rows_of(lw_ref)
    a = rows_of(a_ref)
    ii = lax.broadcasted_iota(jnp.int32, (T, T), 0)
    jj = lax.broadcasted_iota(jnp.int32, (T, T), 1)
    shift = C.bit_length() - 1
    tri = (ii >= jj) & (jnp.right_shift(ii, shift) == jnp.right_shift(jj, shift))
    cl = jnp.dot(tri.astype(F32), lw, precision=HIGHEST, preferred_element_type=F32)
    gls = [cl[s * C + C - 1:s * C + C, :] for s in range(n_s)]
    gl = jnp.concatenate([jnp.broadcast_to(g, (C, g.shape[-1])) for g in gls], axis=0)
    e_cl = jnp.exp(cl)
    e_ncl = jnp.exp(-cl)
    e_ce = jnp.exp(cl - lw)
    e_g = jnp.exp(gl - cl)
    dgs = [jnp.exp(g) for g in gls]
    kkraw = k * kkw_ref[...]
    kmod = k * (1.0 + (a - 1.0) * kaw_ref[...])
    rkk = r * kmod * rk_ref[...]
    rt = r * e_cl
    kt = kmod * e_ncl
    kg = kmod * e_g
    lnx_g = lg_ref[...]
    lnx_b = lb_ref[...]
    row = lax.broadcasted_iota(jnp.int32, (C, W), 0)
    lane = lax.broadcasted_iota(jnp.int32, (C, W), 1)
    left = lane < N
    tok = jnp.where(left, lane, lane - N)
    strict = row > tok
    incl = row >= tok
    same_head = ((lax.broadcasted_iota(jnp.int32, (W, W), 0) < N)
                 == (lax.broadcasted_iota(jnp.int32, (W, W), 1) < N))

    def seg_sum(x):
        s_l = jnp.sum(jnp.where(left, x, 0.0), axis=-1, keepdims=True)
        s_r = jnp.sum(jnp.where(left, 0.0, x), axis=-1, keepdims=True)
        return jnp.where(left, s_l, s_r)

    def bd(x):
        xb = x.astype(BF16)
        zero = jnp.zeros_like(xb)
        return jnp.concatenate([jnp.where(left, xb, zero), jnp.where(left, zero, xb)], axis=0)

    units = [(s, p) for s in range(n_s) for p in range(RW_HEADS // 2)]
    at = {u: (slice(u[0] * C, (u[0] + 1) * C), slice(u[1] * W, (u[1] + 1) * W)) for u in units}
    S = {u: sbd_scr[u[0], u[1]] for u in units}
    V2 = {u: v[at[u]] for u in units}
    b2, lhs2, rhs_nt = {}, {}, {}
    for u in units:
        kk2 = kkraw[at[u]]
        kkn = kk2 / jnp.maximum(jnp.sqrt(seg_sum(kk2 * kk2)), 1e-12)
        b2[u] = kkn * a[at[u]]
        lhs2[u] = jnp.concatenate([-kkn * e_ce[at[u]], rt[at[u]]], axis=0).astype(BF16)
        rhs_nt[u] = jnp.concatenate([bd(b2[u] * e_ncl[at[u]]), bd(kt[at[u]])], axis=0)
    P = {u: lax.dot_general(lhs2[u], rhs_nt[u], NT_DIMS, preferred_element_type=F32) for u in units}
    LS = {u: lax.dot_general(lhs2[u], S[u].astype(BF16), NT_DIMS, preferred_element_type=F32) for u in units}
    bd_v = {u: bd(V2[u]) for u in units}
    X = {u: LS[u][:C] + jnp.dot(jnp.where(strict, P[u][:C, W:], 0.0).astype(BF16), bd_v[u],
                                preferred_element_type=F32) for u in units}
    Pk = {u: jnp.where(strict, P[u][:C, :W], 0.0) for u in units}
    span = 1
    while span < C:
        span *= 2
        if span < C:
            Z = {u: jnp.dot(Pk[u].astype(BF16), jnp.concatenate([bd(X[u]), bd(Pk[u])], axis=1),
                            preferred_element_type=F32) for u in units}
            X = {u: X[u] + Z[u][:, :W] for u in units}
            Pk = {u: Z[u][:, W:] for u in units}
        else:
            X = {u: X[u] + jnp.dot(Pk[u].astype(BF16), bd(X[u]), preferred_element_type=F32) for u in units}
    incl2 = jnp.concatenate([incl, incl], axis=1)
    Y = {u: LS[u][C:] + jnp.dot(jnp.where(incl2, P[u][C:], 0.0).astype(BF16),
                                jnp.concatenate([bd(X[u]), bd_v[u]], axis=0), preferred_element_type=F32)
         for u in units}
    s_new = {u: S[u] * dgs[u[0]][:, at[u][1]]
             + jnp.where(same_head,
                         _mm_tn(jnp.concatenate([X[u], V2[u]], axis=0),
                                jnp.concatenate([b2[u] * e_g[at[u]], kg[at[u]]], axis=0)), 0.0)
             for u in units}
    for u in units:
        sl = at[u][1]
        yc = Y[u] - seg_sum(Y[u]) * (1.0 / N)
        yn = yc * lax.rsqrt(seg_sum(yc * yc) * (1.0 / N) + RW_GN_EPS)
        y_ref[u[0], :, sl] = yn * lnx_g[:, sl] + lnx_b[:, sl] + seg_sum(rkk[at[u]]) * V2[u]
    for u in units:
        sbd_scr[u[0], u[1]] = s_new[u]

    @pl.when(c == n_chunks - 1)
    def _():
        for s, p in units:
            s_ref[s, 2 * p] = s_new[(s, p)][:N, :N]
            s_ref[s, 2 * p + 1] = s_new[(s, p)][N:, N:]


def _wkv_chunked(r, k, v, lw, a, p, *, n_seq, n_s):
    M, D = r.shape
    C = WKV_CHUNK
    L = M // n_seq
    nc = L // C
    seqs = lambda t: t.reshape(n_seq, L, D)
    tok = pl.BlockSpec((n_s, C, D), lambda b, c: (b, c, 0))
    par = _resident((1, D), lambda b, c: (0, 0))
    vec = lambda t: t.reshape(1, D)
    st_spec = pl.BlockSpec((n_s, RW_HEADS, RW_HEAD, RW_HEAD), lambda b, c: (b, 0, 0, 0))
    pair_state = (n_s, RW_HEADS // 2, 2 * RW_HEAD, 2 * RW_HEAD)
    blocks = 6 * _nbytes((n_s, C, D), F32) + _nbytes((n_s, RW_HEADS, RW_HEAD, 128), F32)
    y, state = pl.pallas_call(
        functools.partial(_wkv_chunk_body, n_chunks=nc), grid=(n_seq // n_s, nc),
        in_specs=[tok] * 5 + [par] * 5,
        out_specs=[tok, st_spec],
        out_shape=[jax.ShapeDtypeStruct((n_seq, L, D), F32),
                   jax.ShapeDtypeStruct((n_seq, RW_HEADS, RW_HEAD, RW_HEAD), F32)],
        scratch_shapes=[pltpu.VMEM(pair_state, F32)],
        compiler_params=pltpu.CompilerParams(
            dimension_semantics=("parallel", "arbitrary"),
            vmem_limit_bytes=_vmem_limit(blocks, _nbytes(pair_state, F32), 24 * _nbytes((n_s, C, D), F32))),
        name="wkv_chunk",
    )(seqs(r), seqs(k), seqs(v), seqs(lw), seqs(a),
      vec(p["kk"]), vec(p["ka"]), vec(p["rk"]), vec(p["lnx_g"]), vec(p["lnx_b"]))
    return y.reshape(M, D), state


def _wkv_step_body(r_ref, k_ref, v_ref, lw_ref, a_ref, kkw_ref, kaw_ref, rk_ref, lg_ref, lb_ref, s_ref,
                   y_ref, so_ref, y_scr):
    r = r_ref[0]
    k = k_ref[0]
    v = v_ref[0]
    a = a_ref[0]
    d = jnp.exp(lw_ref[0])
    kkh = k * kkw_ref[0]
    nrm = jnp.sqrt(jnp.sum(kkh * kkh, axis=0, keepdims=True))
    kkn = kkh / jnp.maximum(nrm, 1e-12)
    kmod = k * (1.0 + (a - 1.0) * kaw_ref[0])
    bh = kkn * a

    def value_channel(i, carry):
        S = s_ref[0, i]
        sa = -jnp.sum(S * kkn, axis=0, keepdims=True)
        s_new = S * d + sa * bh + v_ref[0, pl.ds(i, 1), :] * kmod
        so_ref[0, i] = s_new
        y_scr[pl.ds(i, 1), :] = jnp.sum(s_new * r, axis=0, keepdims=True)
        return carry

    lax.fori_loop(0, RW_HEAD, value_channel, 0, unroll=8)
    y = y_scr[...]
    mu = jnp.mean(y, axis=0, keepdims=True)
    yc = y - mu
    yn = yc * lax.rsqrt(jnp.mean(yc * yc, axis=0, keepdims=True) + RW_GN_EPS)
    y_ref[0] = yn * lg_ref[0] + lb_ref[0] + jnp.sum(r * kmod * rk_ref[0], axis=0, keepdims=True) * v


def _wkv_step(r, k, v, lw, a, p, state):
    B, D = r.shape
    H, N = RW_HEADS, RW_HEAD
    lanes = lambda t: t.T.reshape(H, N, B)
    par = lambda t: jnp.broadcast_to(t.reshape(H, N, 1), (H, N, B))
    vec = pl.BlockSpec((1, N, B), lambda h: (h, 0, 0))
    st = pl.BlockSpec((1, N, N, B), lambda h: (h, 0, 0, 0))
    blocks = 2 * _nbytes((N, N, B), F32) + 11 * _nbytes((N, B), F32)
    y, s_new = pl.pallas_call(
        _wkv_step_body, grid=(H,),
        in_specs=[vec] * 10 + [st],
        out_specs=[vec, st],
        out_shape=[jax.ShapeDtypeStruct((H, N, B), F32), jax.ShapeDtypeStruct((H, N, N, B), F32)],
        scratch_shapes=[pltpu.VMEM((N, B), F32)],
        compiler_params=pltpu.CompilerParams(dimension_semantics=("parallel",),
                                             vmem_limit_bytes=_vmem_limit(blocks, _nbytes((N, B), F32), 4 << 20)),
        name="wkv_step",
    )(lanes(r), lanes(k), lanes(v), lanes(lw), lanes(a),
      par(p["kk"]), par(p["ka"]), par(p["rk"]), par(p["lnx_g"]), par(p["lnx_b"]),
      jnp.transpose(state, (1, 2, 3, 0)))
    return y.reshape(D, B).T, jnp.transpose(s_new, (3, 0, 1, 2))


def _rwkv_post_body(x_ref, y_ref, g_ref, wo_ref, o_ref):
    o_ref[...] = x_ref[...] + _mm(y_ref[...] * g_ref[...], wo_ref[...])


def _rwkv_post(x, y, g, wo, *, tm):
    M, D = x.shape
    tok = pl.BlockSpec((tm, D), lambda i: (i, 0))
    streamed = 4 * _nbytes((tm, D), F32)
    return pl.pallas_call(
        _rwkv_post_body, grid=(M // tm,),
        in_specs=[tok, tok, tok, _resident(wo.shape, lambda i: (0, 0))],
        out_specs=tok, out_shape=jax.ShapeDtypeStruct((M, D), F32),
        compiler_params=pltpu.CompilerParams(
            dimension_semantics=("parallel",),
            vmem_limit_bytes=_vmem_limit(streamed, _nbytes(wo.shape, BF16), 3 * _nbytes((tm, D), F32))),
        name="rwkv_post",
    )(x, y, g, wo)


def _ret_pre_body(x_ref, lng_ref, cos_ref, sin_ref, wq_ref, wk_ref, wv_ref, wg_ref,
                  q_ref, k_ref, v_ref, gate_ref, h_scr):
    @pl.when(pl.program_id(1) == 0)
    def _():
        h_scr[...] = _rms(x_ref[...], lng_ref[...]).astype(BF16)

    h = h_scr[...]
    cos = cos_ref[...]
    sin = sin_ref[...]
    half = RET_DK // 2

    def rotate(t, out_ref):
        t1, t2 = t[:, :half], t[:, half:]
        out_ref[:, :half] = t1 * cos - t2 * sin
        out_ref[:, half:] = t1 * sin + t2 * cos

    rotate(jnp.dot(h, wq_ref[...], preferred_element_type=F32), q_ref)
    rotate(jnp.dot(h, wk_ref[...], preferred_element_type=F32) * (RET_DK ** -0.5), k_ref)
    v_ref[...] = jnp.dot(h, wv_ref[...], preferred_element_type=F32)
    gz = jnp.dot(h, wg_ref[...], preferred_element_type=F32)
    gate_ref[...] = gz * jax.nn.sigmoid(gz)


def _ret_pre(x, ln_g, cos, sin, p, *, tm):
    M, D = x.shape
    H, DK, DV = RET_HEADS, RET_DK, RET_DV
    n_tab = cos.shape[0] // tm
    tab = pl.BlockSpec((tm, DK // 2), lambda i, h: (i % n_tab, 0))
    col = lambda w: pl.BlockSpec((D, w), lambda i, h: (0, h))
    out = lambda w: pl.BlockSpec((tm, w), lambda i, h: (i, h))
    blocks = (_nbytes((tm, D), F32) + 2 * _nbytes((tm, DK // 2), F32) + 2 * _nbytes((D, DK), BF16)
              + 2 * _nbytes((D, DV), BF16) + 2 * _nbytes((tm, DK), F32) + 2 * _nbytes((tm, DV), F32))
    return pl.pallas_call(
        _ret_pre_body, grid=(M // tm, H),
        in_specs=[pl.BlockSpec((tm, D), lambda i, h: (i, 0)), pl.BlockSpec((1, D), lambda i, h: (0, 0)),
                  tab, tab, col(DK), col(DK), col(DV), col(DV)],
        out_specs=[out(DK), out(DK), out(DV), out(DV)],
        out_shape=[jax.ShapeDtypeStruct((M, H * DK), F32), jax.ShapeDtypeStruct((M, H * DK), F32),
                   jax.ShapeDtypeStruct((M, H * DV), F32), jax.ShapeDtypeStruct((M, H * DV), F32)],
        scratch_shapes=[pltpu.VMEM((tm, D), BF16)],
        compiler_params=pltpu.CompilerParams(
            dimension_semantics=("parallel", "arbitrary"),
            vmem_limit_bytes=_vmem_limit(blocks, _nbytes((tm, D), BF16), 4 * _nbytes((tm, DV), F32))),
        name="ret_pre",
    )(x, ln_g.reshape(1, D), cos, sin, p["wq"], p["wk"], p["wv"], p["wg"])


def _ret_log_gamma():
    return jnp.log1p(-jnp.exp2(-5.0 - jnp.arange(RET_HEADS, dtype=F32)))


def _ret_chunk_body(lg_ref, q_ref, k_ref, v_ref, o_ref, s_ref, *, n_sub):
    C, DK, DV = RET_CHUNK, RET_DK, RET_DV

    @pl.when(pl.program_id(1) == 0)
    def _():
        s_ref[...] = jnp.zeros_like(s_ref)

    ii = lax.broadcasted_iota(jnp.int32, (C, C), 0)
    jj = lax.broadcasted_iota(jnp.int32, (C, C), 1)
    diff = jnp.maximum((ii - jj).astype(F32), 0.0)
    idx = lax.broadcasted_iota(jnp.int32, (C, 1), 0).astype(F32)
    heads = range(RET_HEADS)
    units = [(hd, cc) for hd in heads for cc in range(n_sub)]
    lg = [lg_ref[hd] for hd in heads]
    dmat = [jnp.where(ii >= jj, jnp.exp(lg[hd] * diff), 0.0) for hd in heads]
    q_dec = [jnp.exp(lg[hd] * (idx + 1.0)) for hd in heads]
    k_dec = [jnp.exp(lg[hd] * (C - 1.0 - idx)) for hd in heads]
    s_dec = [jnp.exp(jnp.full((1, DV), lg[hd] * C, F32)) for hd in heads]
    rows = lambda cc: slice(cc * C, (cc + 1) * C)
    q = {u: q_ref[rows(u[1]), u[0] * DK:(u[0] + 1) * DK] for u in units}
    k = {u: k_ref[rows(u[1]), u[0] * DK:(u[0] + 1) * DK] for u in units}
    v = {u: v_ref[rows(u[1]), u[0] * DV:(u[0] + 1) * DV].astype(BF16) for u in units}
    inner = {u: _mm_nt(q[u], k[u]) * dmat[u[0]] for u in units}
    kv = {u: lax.dot_general((k[u] * k_dec[u[0]]).astype(BF16), v[u], TN_DIMS, preferred_element_type=F32)
          for u in units}
    state = {}
    for hd in heads:
        S = s_ref[0, hd]
        for cc in range(n_sub):
            state[(hd, cc)] = S
            S = S * s_dec[hd] + kv[(hd, cc)]
        s_ref[0, hd] = S
    for u in units:
        hd, cc = u
        o_ref[rows(cc), hd * DV:(hd + 1) * DV] = (
            jnp.dot(inner[u].astype(BF16), v[u], preferred_element_type=F32)
            + _mm(q[u] * q_dec[hd], state[u]))


def _ret_chunked(q, k, v, *, n_seq, n_sub):
    M = q.shape[0]
    H, DK, DV, C = RET_HEADS, RET_DK, RET_DV, RET_CHUNK
    ns = M // n_seq // (C * n_sub)
    tok = lambda w: pl.BlockSpec((C * n_sub, w), lambda b, c: (b * ns + c, 0))
    st = pl.BlockSpec((1, H, DK, DV), lambda b, c: (b, 0, 0, 0))
    blocks = (2 * _nbytes((C * n_sub, H * DK), F32) + 2 * _nbytes((C * n_sub, H * DV), F32)
              + _nbytes((H, DK, DV), F32))
    return pl.pallas_call(
        functools.partial(_ret_chunk_body, n_sub=n_sub), grid=(n_seq, ns),
        in_specs=[pl.BlockSpec(memory_space=pltpu.SMEM), tok(H * DK), tok(H * DK), tok(H * DV)],
        out_specs=[tok(H * DV), st],
        out_shape=[jax.ShapeDtypeStruct((M, H * DV), F32), jax.ShapeDtypeStruct((n_seq, H, DK, DV), F32)],
        compiler_params=pltpu.CompilerParams(dimension_semantics=("parallel", "arbitrary"),
                                             vmem_limit_bytes=_vmem_limit(blocks, 0, 6 * _nbytes((H, DK, DV), F32))),
        name="ret_chunk",
    )(_ret_log_gamma(), q, k, v)


def _ret_step_body(lg_ref, q_ref, k_ref, v_ref, s_ref, o_ref, so_ref):
    lg = lg_ref[pl.program_id(1)]
    q = q_ref[0]
    k = k_ref[0]
    v = v_ref[0]
    bt = q.shape[0]
    gam_k = jnp.exp(jnp.full((1, RET_DK), lg, F32))
    gam_v = jnp.exp(jnp.full((1, RET_DV), lg, F32))
    qg = q * gam_k
    vb = v.astype(BF16)
    rowid = lax.broadcasted_iota(jnp.int32, (SUBLANES, RET_DK), 0)
    inner = jnp.sum(q * k, axis=-1, keepdims=True)
    for grp in range(bt // SUBLANES):
        rs = slice(grp * SUBLANES, (grp + 1) * SUBLANES)
        o8 = inner[rs] * v[rs]
        for j in range(SUBLANES):
            b = grp * SUBLANES + j
            m = rowid == j
            S = s_ref[b, 0]
            o8 = o8 + _mm(jnp.where(m, qg[rs], 0.0), S)
            so_ref[b, 0] = S * gam_v + lax.dot_general(jnp.where(m, k[rs], 0.0).astype(BF16), vb[rs],
                                                       TN_DIMS, preferred_element_type=F32)
        o_ref[0, rs, :] = o8


def _ret_step(q, k, v, state, *, bt):
    B = q.shape[0]
    H, DK, DV = RET_HEADS, RET_DK, RET_DV
    heads = lambda t, w: t.reshape(B, H, w).transpose(1, 0, 2)
    tok = lambda w: pl.BlockSpec((1, bt, w), lambda i, h: (h, i, 0))
    st = pl.BlockSpec((bt, 1, DK, DV), lambda i, h: (i, h, 0, 0))
    blocks = 2 * _nbytes((bt, DK, DV), F32) + 4 * _nbytes((bt, DV), F32)
    o, s_new = pl.pallas_call(
        _ret_step_body, grid=(B // bt, H),
        in_specs=[pl.BlockSpec(memory_space=pltpu.SMEM), tok(DK), tok(DK), tok(DV), st],
        out_specs=[tok(DV), st],
        out_shape=[jax.ShapeDtypeStruct((H, B, DV), F32), jax.ShapeDtypeStruct((B, H, DK, DV), F32)],
        compiler_params=pltpu.CompilerParams(dimension_semantics=("parallel", "parallel"),
                                             vmem_limit_bytes=_vmem_limit(blocks, 0, 4 * _nbytes((DK, DV), F32))),
        name="ret_step",
    )(_ret_log_gamma(), heads(q, DK), heads(k, DK), heads(v, DV), state)
    return o.transpose(1, 0, 2).reshape(B, H * DV), s_new


def _ret_post_body(x_ref, o_ref, gate_ref, wo_ref, out_ref):
    acc = x_ref[...]
    for hd in range(RET_HEADS):
        sl = slice(hd * RET_DV, (hd + 1) * RET_DV)
        acc = acc + _mm(gate_ref[:, sl] * _head_norm(o_ref[:, sl], RET_GN_EPS), wo_ref[sl, :])
    out_ref[...] = acc


def _ret_post(x, o, gate, wo, *, tm):
    M, D = x.shape
    W = o.shape[1]
    tok = lambda w: pl.BlockSpec((tm, w), lambda i: (i, 0))
    streamed = 2 * _nbytes((tm, D), F32) + 2 * _nbytes((tm, W), F32)
    return pl.pallas_call(
        _ret_post_body, grid=(M // tm,),
        in_specs=[tok(D), tok(W), tok(W), _resident(wo.shape, lambda i: (0, 0))],
        out_specs=tok(D), out_shape=jax.ShapeDtypeStruct((M, D), F32),
        compiler_params=pltpu.CompilerParams(
            dimension_semantics=("parallel",),
            vmem_limit_bytes=_vmem_limit(streamed, _nbytes(wo.shape, BF16), 3 * _nbytes((tm, W), F32))),
        name="ret_post",
    )(x, o, gate, wo)


def _rope_tables(pos):
    half = RET_DK // 2
    inv = ROPE_BASE ** (-jnp.arange(half, dtype=F32) / half)
    ang = pos.astype(F32)[:, None] * inv[None, :]
    return jnp.cos(ang), jnp.sin(ang)


def _trunk(x, n_seq, pos, shift0, wkv0, ret0, w, tiles):
    M, D = x.shape
    L = M // n_seq
    cos, sin = _rope_tables(pos)
    if L < tiles["ret_pre"]:
        cos, sin = jnp.tile(cos, (M // L, 1)), jnp.tile(sin, (M // L, 1))
    new_shift, new_wkv, new_ret = [], [], []
    for i in range(DEPTH):
        x = _ffn(x, w["ln_ffn1"][i], w["ff1_wg"][i], w["ff1_wu"][i], w["ff1_wd"][i],
                 tm=tiles["ffn_m"], tf=tiles["ffn_f"])
        j = i // N_MIXERS
        if i % N_MIXERS == 0:
            p = {n: w["rw_" + n][j] for n in ("mu", "w0", "w1", "w2", "a0", "a1", "a2", "g1", "g2", "kk", "ka",
                                              "rk", "wr", "wk", "wv", "wo", "lnx_g", "lnx_b")}
            p["ln"] = w["ln_mix"][i]
            prev = jnp.zeros((n_seq, D), F32) if shift0 is None else shift0[j]
            r, k, v, lw, a, g, sh = _rwkv_pre(x, prev, p, n_seq=n_seq, tile=tiles["rwkv_pre"])
            if L > 1:
                y, st = _wkv_chunked(r, k, v, lw, a, p, n_seq=n_seq, n_s=tiles["wkv_seqs"])
            else:
                y, st = _wkv_step(r, k, v, lw, a, p, wkv0[j])
            x = _rwkv_post(x, y, g, p["wo"], tm=tiles["post"])
            new_shift.append(sh)
            new_wkv.append(st)
        else:
            p = {n: w["ret_" + n][j] for n in ("wq", "wk", "wv", "wg", "wo")}
            q, k, v, gate = _ret_pre(x, w["ln_mix"][i], cos, sin, p, tm=tiles["ret_pre"])
            if L > 1:
                o, st = _ret_chunked(q, k, v, n_seq=n_seq, n_sub=tiles["ret_sub"])
            else:
                o, st = _ret_step(q, k, v, ret0[j], bt=tiles["ret_step"])
            x = _ret_post(x, o, gate, p["wo"], tm=tiles["post"])
            new_ret.append(st)
        x = _ffn(x, w["ln_ffn2"][i], w["ff2_wg"][i], w["ff2_wu"][i], w["ff2_wd"][i],
                 w["ln_final"] if i == DEPTH - 1 else None, tm=tiles["ffn_m"], tf=tiles["ffn_f"])
    return x, jnp.stack(new_shift), jnp.stack(new_wkv), jnp.stack(new_ret)


PROMPT_TILES = dict(ffn_m=1024, ffn_f=256, rwkv_pre=256, post=256, ret_pre=512, ret_sub=2, wkv_seqs=2)
SAMPLE_TILES = dict(ffn_m=128, ffn_f=256, rwkv_pre=128, post=128, ret_pre=128, ret_step=8)

MATMUL_WEIGHTS = ("ff1_wg", "ff1_wu", "ff1_wd", "ff2_wg", "ff2_wu", "ff2_wd", "rw_w1", "rw_w2", "rw_a1", "rw_a2",
                  "rw_g1", "rw_g2", "rw_wr", "rw_wk", "rw_wv", "rw_wo", "ret_wq", "ret_wk", "ret_wv", "ret_wg",
                  "ret_wo")


def kernel(x_prompt, x_sample, state_rwkv_shift, state_rwkv_wkv, state_ret, ln_ffn1, ff1_wg, ff1_wu, ff1_wd, ln_mix, ln_ffn2, ff2_wg, ff2_wu, ff2_wd, ln_final, rw_mu, rw_w0, rw_w1, rw_w2, rw_a0, rw_a1, rw_a2, rw_g1, rw_g2, rw_kk, rw_ka, rw_rk, rw_wr, rw_wk, rw_wv, rw_wo, rw_lnx_g, rw_lnx_b, ret_wq, ret_wk, ret_wv, ret_wg, ret_wo):
    w = dict(ln_ffn1=ln_ffn1, ff1_wg=ff1_wg, ff1_wu=ff1_wu, ff1_wd=ff1_wd, ln_mix=ln_mix,
             ln_ffn2=ln_ffn2, ff2_wg=ff2_wg, ff2_wu=ff2_wu, ff2_wd=ff2_wd, ln_final=ln_final,
             rw_mu=rw_mu, rw_w0=rw_w0, rw_w1=rw_w1, rw_w2=rw_w2, rw_a0=rw_a0, rw_a1=rw_a1,
             rw_a2=rw_a2, rw_g1=rw_g1, rw_g2=rw_g2, rw_kk=rw_kk, rw_ka=rw_ka, rw_rk=rw_rk,
             rw_wr=rw_wr, rw_wk=rw_wk, rw_wv=rw_wv, rw_wo=rw_wo, rw_lnx_g=rw_lnx_g,
             rw_lnx_b=rw_lnx_b, ret_wq=ret_wq, ret_wk=ret_wk, ret_wv=ret_wv, ret_wg=ret_wg,
             ret_wo=ret_wo)
    for name in MATMUL_WEIGHTS:
        w[name] = [w[name][i].astype(BF16) for i in range(w[name].shape[0])]
    B, L, D = x_prompt.shape
    Bs, Ls, _ = x_sample.shape
    y_p, p_shift, p_wkv, p_ret = _trunk(x_prompt.reshape(B * L, D), B, jnp.arange(L, dtype=F32),
                                        None, None, None, w, PROMPT_TILES)
    pos_s = PAST_LEN + jnp.arange(Ls, dtype=F32)
    y_s, s_shift, s_wkv, s_ret = _trunk(x_sample.reshape(Bs * Ls, D), Bs, pos_s,
                                        state_rwkv_shift, state_rwkv_wkv, state_ret, w, SAMPLE_TILES)
    return (y_p.reshape(B, L, D), y_s.reshape(Bs, Ls, D), p_shift, p_wkv, p_ret, s_shift, s_wkv, s_ret)
```

```python
import functools
import math

import numpy as np
import jax
import jax.numpy as jnp
from jax import lax
from jax.experimental import pallas as pl
from jax.experimental.pallas import tpu as pltpu

F32 = jnp.float32
BF16 = jnp.bfloat16
HIGHEST = lax.Precision.HIGHEST

DEPTH = 2
N_MIXERS = 2
RW_HEADS = 16
RW_HEAD = 64
RW_GN_EPS = 64e-5
RET_HEADS = 4
RET_DK = 256
RET_DV = 512
RET_CHUNK = 128
RET_GN_EPS = 1e-6
ROPE_BASE = 10000.0
FFN_RES = 0.5
RMS_EPS = 1e-6
PAST_LEN = 16384

WKV_CHUNK = 64
V7X_VMEM_CAP_BYTES = 60000 * 1024
SUBLANES = 8

NT_DIMS = (((1,), (1,)), ((), ()))
TN_DIMS = (((0,), (0,)), ((), ()))


def _vmem_limit(block_bytes, scratch_bytes=0, temp_bytes=0):
    return int(min(V7X_VMEM_CAP_BYTES, 2 * block_bytes + scratch_bytes + temp_bytes + (4 << 20)))


def _nbytes(shape, dtype):
    return int(np.prod(shape)) * jnp.dtype(dtype).itemsize


def _resident(shape, index_map):
    return pl.BlockSpec(shape, index_map, pipeline_mode=pl.Buffered(1))


def _rms(x, g):
    return x * lax.rsqrt(jnp.mean(x * x, axis=-1, keepdims=True) + RMS_EPS) * g


def _head_norm(y, eps):
    mu = jnp.mean(y, axis=-1, keepdims=True)
    yc = y - mu
    return yc * lax.rsqrt(jnp.mean(yc * yc, axis=-1, keepdims=True) + eps)


def _mm(a, b):
    return jnp.dot(a.astype(BF16), b.astype(BF16), preferred_element_type=F32)


def _mm_nt(a, b):
    return lax.dot_general(a.astype(BF16), b.astype(BF16), NT_DIMS, preferred_element_type=F32)


def _mm_tn(a, b):
    return lax.dot_general(a.astype(BF16), b.astype(BF16), TN_DIMS, preferred_element_type=F32)


def _ffn_body(*refs, ff_tile, final):
    if final:
        x_ref, g_ref, wg_ref, wu_ref, wd_ref, gf_ref, o_ref = refs
    else:
        x_ref, g_ref, wg_ref, wu_ref, wd_ref, o_ref = refs
    x = x_ref[...]
    h = _rms(x, g_ref[...]).astype(BF16)
    n_tiles = wg_ref.shape[1] // ff_tile

    def activation(j):
        sl = slice(j * ff_tile, (j + 1) * ff_tile)
        gate = jnp.dot(h, wg_ref[:, sl], preferred_element_type=F32)
        up = jnp.dot(h, wu_ref[:, sl], preferred_element_type=F32)
        return (gate * jax.nn.sigmoid(gate) * up).astype(BF16)

    act = activation(0)
    acc = None
    for j in range(n_tiles):
        nxt = activation(j + 1) if j + 1 < n_tiles else None
        part = jnp.dot(act, wd_ref[j * ff_tile:(j + 1) * ff_tile, :], preferred_element_type=F32)
        acc = part if acc is None else acc + part
        act = nxt
    y = x + FFN_RES * acc
    if final:
        y = _rms(y, gf_ref[...])
    o_ref[...] = y


def _ffn(x, ln_g, wg, wu, wd, ln_final=None, *, tm, tf):
    M, D = x.shape
    FF = wg.shape[1]
    final = ln_final is not None
    row = lambda i: (i, 0)
    const = lambda i: (0, 0)
    in_specs = [pl.BlockSpec((tm, D), row), _resident((1, D), const), _resident((D, FF), const),
                _resident((D, FF), const), _resident((FF, D), const)]
    args = [x, ln_g.reshape(1, D), wg, wu, wd]
    if final:
        in_specs.append(_resident((1, D), const))
        args.append(ln_final.reshape(1, D))
    streamed = 2 * _nbytes((tm, D), F32)
    resident = 3 * _nbytes((D, FF), BF16)
    temps = 6 * _nbytes((tm, tf), F32) + 3 * _nbytes((tm, D), F32)
    return pl.pallas_call(
        functools.partial(_ffn_body, ff_tile=tf, final=final),
        grid=(M // tm,), in_specs=in_specs, out_specs=pl.BlockSpec((tm, D), row),
        out_shape=jax.ShapeDtypeStruct((M, D), F32),
        compiler_params=pltpu.CompilerParams(dimension_semantics=("parallel",),
                                             vmem_limit_bytes=_vmem_limit(streamed, resident, temps)),
        name="ffn_final" if final else "ffn",
    )(*args)


def _rwkv_pre_body(x_ref, lng_ref, prev_ref, mu_ref, w0_ref, a0_ref, wr_ref, wk_ref, wv_ref,
                   w1_ref, w2_ref, a1_ref, a2_ref, g1_ref, g2_ref,
                   r_ref, k_ref, v_ref, lw_ref, a_ref, g_ref, so_ref, *scratch, seq):
    h = _rms(x_ref[...], lng_ref[...])
    if seq:
        (carry,) = scratch
        rows = h.shape[0]

        @pl.when(pl.program_id(1) == 0)
        def _():
            carry[...] = prev_ref[0]

        row = lax.broadcasted_iota(jnp.int32, h.shape, 0)
        xprev = jnp.where(row == 0, carry[...], pltpu.roll(h, 1, 0))
        last = h[rows - 1:rows, :]
        carry[...] = last
        so_ref[0] = last
    else:
        xprev = prev_ref[...]
        so_ref[...] = h
    xx = xprev - h
    xr, xw, xk, xv, xa, xg = (h + xx * mu_ref[i:i + 1, :] for i in range(6))
    r_ref[...] = _mm(xr, wr_ref[...])
    k_ref[...] = _mm(xk, wk_ref[...])
    v_ref[...] = _mm(xv, wv_ref[...])
    z = w0_ref[...] + _mm(jnp.tanh(_mm(xw, w1_ref[...])), w2_ref[...])
    lw_ref[...] = -math.exp(-0.5) * jax.nn.sigmoid(z)
    a_ref[...] = jax.nn.sigmoid(a0_ref[...] + _mm(_mm(xa, a1_ref[...]), a2_ref[...]))
    g_ref[...] = _mm(jax.nn.sigmoid(_mm(xg, g1_ref[...])), g2_ref[...])


def _rwkv_pre(x, prev, p, *, n_seq, tile):
    M, D = x.shape
    L = M // n_seq
    seq = L > 1
    vec = lambda a: a.reshape(1, D)
    weights = [p["wr"], p["wk"], p["wv"], p["w1"], p["w2"], p["a1"], p["a2"], p["g1"], p["g2"]]
    if seq:
        nl = L // tile
        grid = (n_seq, nl)
        row = lambda b, l: (b * nl + l, 0)
        const = lambda b, l: (0, 0)
        prev_arg = prev.reshape(n_seq, 1, D)
        prev_spec = pl.BlockSpec((1, 1, D), lambda b, l: (b, 0, 0))
        so_shape = jax.ShapeDtypeStruct((n_seq, 1, D), F32)
        so_spec = pl.BlockSpec((1, 1, D), lambda b, l: (b, 0, 0))
        scratch = [pltpu.VMEM((1, D), F32)]
        sem = ("parallel", "arbitrary")
    else:
        grid = (M // tile,)
        row = lambda i: (i, 0)
        const = lambda i: (0, 0)
        prev_arg = prev
        prev_spec = pl.BlockSpec((tile, D), row)
        so_shape = jax.ShapeDtypeStruct((M, D), F32)
        so_spec = pl.BlockSpec((tile, D), row)
        scratch = []
        sem = ("parallel",)
    tok = pl.BlockSpec((tile, D), row)
    full = lambda a: _resident(a.shape, const)
    in_specs = ([tok, full(vec(p["ln"])), prev_spec, full(p["mu"]), full(vec(p["w0"])), full(vec(p["a0"]))]
                + [full(w) for w in weights])
    streamed = 8 * _nbytes((tile, D), F32)
    resident = sum(_nbytes(w.shape, BF16) for w in weights)
    outs = pl.pallas_call(
        functools.partial(_rwkv_pre_body, seq=seq),
        grid=grid, in_specs=in_specs,
        out_specs=[tok] * 6 + [so_spec],
        out_shape=[jax.ShapeDtypeStruct((M, D), F32)] * 6 + [so_shape],
        scratch_shapes=scratch,
        compiler_params=pltpu.CompilerParams(
            dimension_semantics=sem,
            vmem_limit_bytes=_vmem_limit(streamed, resident, 10 * _nbytes((tile, D), F32))),
        name="rwkv_pre",
    )(x, vec(p["ln"]), prev_arg, p["mu"], vec(p["w0"]), vec(p["a0"]), *weights)
    r, k, v, lw, a, g, so = outs
    return r, k, v, lw, a, g, so.reshape(n_seq, D) if seq else so


def _wkv_chunk_body(r_ref, k_ref, v_ref, lw_ref, a_ref, kkw_ref, kaw_ref, rk_ref, lg_ref, lb_ref,
                    y_ref, s_ref, sbd_scr, *, n_chunks):
    C = WKV_CHUNK
    N = RW_HEAD
    W = 2 * N
    n_s = r_ref.shape[0]
    T = n_s * C
    c = pl.program_id(1)

    @pl.when(c == 0)
    def _():
        sbd_scr[...] = jnp.zeros_like(sbd_scr)

    rows_of = lambda ref: ref[...].reshape(T, ref.shape[-1])
    r = rows_of(r_ref)
    k = rows_of(k_ref)
    v = rows_of(v_ref)
    lw = rows_of(lw_ref)
    a = rows_of(a_ref)
    ii = lax.broadcasted_iota(jnp.int32, (T, T), 0)
    jj = lax.broadcasted_iota(jnp.int32, (T, T), 1)
    shift = C.bit_length() - 1
    tri = (ii >= jj) & (jnp.right_shift(ii, shift) == jnp.right_shift(jj, shift))
    cl = jnp.dot(tri.astype(F32), lw, precision=HIGHEST, preferred_element_type=F32)
    gls = [cl[s * C + C - 1:s * C + C, :] for s in range(n_s)]
    gl = jnp.concatenate([jnp.broadcast_to(g, (C, g.shape[-1])) for g in gls], axis=0)
    e_cl = jnp.exp(cl)
    e_ncl = jnp.exp(-cl)
    e_ce = jnp.exp(cl - lw)
    e_g = jnp.exp(gl - cl)
    dgs = [jnp.exp(g) for g in gls]
    kkraw = k * kkw_ref[...]
    kmod = k * (1.0 + (a - 1.0) * kaw_ref[...])
    rkk = r * kmod * rk_ref[...]
    rt = r * e_cl
    kt = kmod * e_ncl
    kg = kmod * e_g
    lnx_g = lg_ref[...]
    lnx_b = lb_ref[...]
    row = lax.broadcasted_iota(jnp.int32, (C, W), 0)
    lane = lax.broadcasted_iota(jnp.int32, (C, W), 1)
    left = lane < N
    tok = jnp.where(left, lane, lane - N)
    strict = row > tok
    incl = row >= tok
    same_head = ((lax.broadcasted_iota(jnp.int32, (W, W), 0) < N)
                 == (lax.broadcasted_iota(jnp.int32, (W, W), 1) < N))

    def seg_sum(x):
        s_l = jnp.sum(jnp.where(left, x, 0.0), axis=-1, keepdims=True)
        s_r = jnp.sum(jnp.where(left, 0.0, x), axis=-1, keepdims=True)
        return jnp.where(left, s_l, s_r)

    def bd(x):
        xb = x.astype(BF16)
        zero = jnp.zeros_like(xb)
        return jnp.concatenate([jnp.where(left, xb, zero), jnp.where(left, zero, xb)], axis=0)

    units = [(s, p) for s in range(n_s) for p in range(RW_HEADS // 2)]
    at = {u: (slice(u[0] * C, (u[0] + 1) * C), slice(u[1] * W, (u[1] + 1) * W)) for u in units}
    S = {u: sbd_scr[u[0], u[1]] for u in units}
    V2 = {u: v[at[u]] for u in units}
    b2, lhs2, rhs_nt = {}, {}, {}
    for u in units:
        kk2 = kkraw[at[u]]
        kkn = kk2 / jnp.maximum(jnp.sqrt(seg_sum(kk2 * kk2)), 1e-12)
        b2[u] = kkn * a[at[u]]
        lhs2[u] = jnp.concatenate([-kkn * e_ce[at[u]], rt[at[u]]], axis=0).astype(BF16)
        rhs_nt[u] = jnp.concatenate([bd(b2[u] * e_ncl[at[u]]), bd(kt[at[u]])], axis=0)
    P = {u: lax.dot_general(lhs2[u], rhs_nt[u], NT_DIMS, preferred_element_type=F32) for u in units}
    LS = {u: lax.dot_general(lhs2[u], S[u].astype(BF16), NT_DIMS, preferred_element_type=F32) for u in units}
    bd_v = {u: bd(V2[u]) for u in units}
    X = {u: LS[u][:C] + jnp.dot(jnp.where(strict, P[u][:C, W:], 0.0).astype(BF16), bd_v[u],
                                preferred_element_type=F32) for u in units}
    Pk = {u: jnp.where(strict, P[u][:C, :W], 0.0) for u in units}
    span = 1
    while span < C:
        span *= 2
        if span < C:
            Z = {u: jnp.dot(Pk[u].astype(BF16), jnp.concatenate([bd(X[u]), bd(Pk[u])], axis=1),
                            preferred_element_type=F32) for u in units}
            X = {u: X[u] + Z[u][:, :W] for u in units}
            Pk = {u: Z[u][:, W:] for u in units}
        else:
            X = {u: X[u] + jnp.dot(Pk[u].astype(BF16), bd(X[u]), preferred_element_type=F32) for u in units}
    incl2 = jnp.concatenate([incl, incl], axis=1)
    Y = {u: LS[u][C:] + jnp.dot(jnp.where(incl2, P[u][C:], 0.0).astype(BF16),
                                jnp.concatenate([bd(X[u]), bd_v[u]], axis=0), preferred_element_type=F32)
         for u in units}
    s_new = {u: S[u] * dgs[u[0]][:, at[u][1]]
             + jnp.where(same_head,
                         _mm_tn(jnp.concatenate([X[u], V2[u]], axis=0),
                                jnp.concatenate([b2[u] * e_g[at[u]], kg[at[u]]], axis=0)), 0.0)
             for u in units}
    for u in units:
        sl = at[u][1]
        yc = Y[u] - seg_sum(Y[u]) * (1.0 / N)
        yn = yc * lax.rsqrt(seg_sum(yc * yc) * (1.0 / N) + RW_GN_EPS)
        y_ref[u[0], :, sl] = yn * lnx_g[:, sl] + lnx_b[:, sl] + seg_sum(rkk[at[u]]) * V2[u]
    for u in units:
        sbd_scr[u[0], u[1]] = s_new[u]

    @pl.when(c == n_chunks - 1)
    def _():
        for s, p in units:
            s_ref[s, 2 * p] = s_new[(s, p)][:N, :N]
            s_ref[s, 2 * p + 1] = s_new[(s, p)][N:, N:]


def _wkv_chunked(r, k, v, lw, a, p, *, n_seq, n_s):
    M, D = r.shape
    C = WKV_CHUNK
    L = M // n_seq
    nc = L // C
    seqs = lambda t: t.reshape(n_seq, L, D)
    tok = pl.BlockSpec((n_s, C, D), lambda b, c: (b, c, 0))
    par = _resident((1, D), lambda b, c: (0, 0))
    vec = lambda t: t.reshape(1, D)
    st_spec = pl.BlockSpec((n_s, RW_HEADS, RW_HEAD, RW_HEAD), lambda b, c: (b, 0, 0, 0))
    pair_state = (n_s, RW_HEADS // 2, 2 * RW_HEAD, 2 * RW_HEAD)
    blocks = 6 * _nbytes((n_s, C, D), F32) + _nbytes((n_s, RW_HEADS, RW_HEAD, 128), F32)
    y, state = pl.pallas_call(
        functools.partial(_wkv_chunk_body, n_chunks=nc), grid=(n_seq // n_s, nc),
        in_specs=[tok] * 5 + [par] * 5,
        out_specs=[tok, st_spec],
        out_shape=[jax.ShapeDtypeStruct((n_seq, L, D), F32),
                   jax.ShapeDtypeStruct((n_seq, RW_HEADS, RW_HEAD, RW_HEAD), F32)],
        scratch_shapes=[pltpu.VMEM(pair_state, F32)],
        compiler_params=pltpu.CompilerParams(
            dimension_semantics=("parallel", "arbitrary"),
            vmem_limit_bytes=_vmem_limit(blocks, _nbytes(pair_state, F32), 24 * _nbytes((n_s, C, D), F32))),
        name="wkv_chunk",
    )(seqs(r), seqs(k), seqs(v), seqs(lw), seqs(a),
      vec(p["kk"]), vec(p["ka"]), vec(p["rk"]), vec(p["lnx_g"]), vec(p["lnx_b"]))
    return y.reshape(M, D), state


def _wkv_step_body(r_ref, k_ref, v_ref, lw_ref, a_ref, kkw_ref, kaw_ref, rk_ref, lg_ref, lb_ref, s_ref,
                   y_ref, so_ref, y_scr):
    r = r_ref[0]
    k = k_ref[0]
    v = v_ref[0]
    a = a_ref[0]
    d = jnp.exp(lw_ref[0])
    kkh = k * kkw_ref[0]
    nrm = jnp.sqrt(jnp.sum(kkh * kkh, axis=0, keepdims=True))
    kkn = kkh / jnp.maximum(nrm, 1e-12)
    kmod = k * (1.0 + (a - 1.0) * kaw_ref[0])
    bh = kkn * a

    def value_channel(i, carry):
        S = s_ref[0, i]
        sa = -jnp.sum(S * kkn, axis=0, keepdims=True)
        s_new = S * d + sa * bh + v_ref[0, pl.ds(i, 1), :] * kmod
        so_ref[0, i] = s_new
        y_scr[pl.ds(i, 1), :] = jnp.sum(s_new * r, axis=0, keepdims=True)
        return carry

    lax.fori_loop(0, RW_HEAD, value_channel, 0, unroll=8)
    y = y_scr[...]
    mu = jnp.mean(y, axis=0, keepdims=True)
    yc = y - mu
    yn = yc * lax.rsqrt(jnp.mean(yc * yc, axis=0, keepdims=True) + RW_GN_EPS)
    y_ref[0] = yn * lg_ref[0] + lb_ref[0] + jnp.sum(r * kmod * rk_ref[0], axis=0, keepdims=True) * v


def _wkv_step(r, k, v, lw, a, p, state):
    B, D = r.shape
    H, N = RW_HEADS, RW_HEAD
    lanes = lambda t: t.T.reshape(H, N, B)
    par = lambda t: jnp.broadcast_to(t.reshape(H, N, 1), (H, N, B))
    vec = pl.BlockSpec((1, N, B), lambda h: (h, 0, 0))
    st = pl.BlockSpec((1, N, N, B), lambda h: (h, 0, 0, 0))
    blocks = 2 * _nbytes((N, N, B), F32) + 11 * _nbytes((N, B), F32)
    y, s_new = pl.pallas_call(
        _wkv_step_body, grid=(H,),
        in_specs=[vec] * 10 + [st],
        out_specs=[vec, st],
        out_shape=[jax.ShapeDtypeStruct((H, N, B), F32), jax.ShapeDtypeStruct((H, N, N, B), F32)],
        scratch_shapes=[pltpu.VMEM((N, B), F32)],
        compiler_params=pltpu.CompilerParams(dimension_semantics=("parallel",),
                                             vmem_limit_bytes=_vmem_limit(blocks, _nbytes((N, B), F32), 4 << 20)),
        name="wkv_step",
    )(lanes(r), lanes(k), lanes(v), lanes(lw), lanes(a),
      par(p["kk"]), par(p["ka"]), par(p["rk"]), par(p["lnx_g"]), par(p["lnx_b"]),
      jnp.transpose(state, (1, 2, 3, 0)))
    return y.reshape(D, B).T, jnp.transpose(s_new, (3, 0, 1, 2))


def _rwkv_post_body(x_ref, y_ref, g_ref, wo_ref, o_ref):
    o_ref[...] = x_ref[...] + _mm(y_ref[...] * g_ref[...], wo_ref[...])


def _rwkv_post(x, y, g, wo, *, tm):
    M, D = x.shape
    tok = pl.BlockSpec((tm, D), lambda i: (i, 0))
    streamed = 4 * _nbytes((tm, D), F32)
    return pl.pallas_call(
        _rwkv_post_body, grid=(M // tm,),
        in_specs=[tok, tok, tok, _resident(wo.shape, lambda i: (0, 0))],
        out_specs=tok, out_shape=jax.ShapeDtypeStruct((M, D), F32),
        compiler_params=pltpu.CompilerParams(
            dimension_semantics=("parallel",),
            vmem_limit_bytes=_vmem_limit(streamed, _nbytes(wo.shape, BF16), 3 * _nbytes((tm, D), F32))),
        name="rwkv_post",
    )(x, y, g, wo)


def _ret_pre_body(x_ref, lng_ref, cos_ref, sin_ref, wq_ref, wk_ref, wv_ref, wg_ref,
                  q_ref, k_ref, v_ref, gate_ref, h_scr):
    @pl.when(pl.program_id(1) == 0)
    def _():
        h_scr[...] = _rms(x_ref[...], lng_ref[...]).astype(BF16)

    h = h_scr[...]
    cos = cos_ref[...]
    sin = sin_ref[...]
    half = RET_DK // 2

    def rotate(t, out_ref):
        t1, t2 = t[:, :half], t[:, half:]
        out_ref[:, :half] = t1 * cos - t2 * sin
        out_ref[:, half:] = t1 * sin + t2 * cos

    rotate(jnp.dot(h, wq_ref[...], preferred_element_type=F32), q_ref)
    rotate(jnp.dot(h, wk_ref[...], preferred_element_type=F32) * (RET_DK ** -0.5), k_ref)
    v_ref[...] = jnp.dot(h, wv_ref[...], preferred_element_type=F32)
    gz = jnp.dot(h, wg_ref[...], preferred_element_type=F32)
    gate_ref[...] = gz * jax.nn.sigmoid(gz)


def _ret_pre(x, ln_g, cos, sin, p, *, tm):
    M, D = x.shape
    H, DK, DV = RET_HEADS, RET_DK, RET_DV
    n_tab = cos.shape[0] // tm
    tab = pl.BlockSpec((tm, DK // 2), lambda i, h: (i % n_tab, 0))
    col = lambda w: pl.BlockSpec((D, w), lambda i, h: (0, h))
    out = lambda w: pl.BlockSpec((tm, w), lambda i, h: (i, h))
    blocks = (_nbytes((tm, D), F32) + 2 * _nbytes((tm, DK // 2), F32) + 2 * _nbytes((D, DK), BF16)
              + 2 * _nbytes((D, DV), BF16) + 2 * _nbytes((tm, DK), F32) + 2 * _nbytes((tm, DV), F32))
    return pl.pallas_call(
        _ret_pre_body, grid=(M // tm, H),
        in_specs=[pl.BlockSpec((tm, D), lambda i, h: (i, 0)), pl.BlockSpec((1, D), lambda i, h: (0, 0)),
                  tab, tab, col(DK), col(DK), col(DV), col(DV)],
        out_specs=[out(DK), out(DK), out(DV), out(DV)],
        out_shape=[jax.ShapeDtypeStruct((M, H * DK), F32), jax.ShapeDtypeStruct((M, H * DK), F32),
                   jax.ShapeDtypeStruct((M, H * DV), F32), jax.ShapeDtypeStruct((M, H * DV), F32)],
        scratch_shapes=[pltpu.VMEM((tm, D), BF16)],
        compiler_params=pltpu.CompilerParams(
            dimension_semantics=("parallel", "arbitrary"),
            vmem_limit_bytes=_vmem_limit(blocks, _nbytes((tm, D), BF16), 4 * _nbytes((tm, DV), F32))),
        name="ret_pre",
    )(x, ln_g.reshape(1, D), cos, sin, p["wq"], p["wk"], p["wv"], p["wg"])


def _ret_log_gamma():
    return jnp.log1p(-jnp.exp2(-5.0 - jnp.arange(RET_HEADS, dtype=F32)))


def _ret_mix_body(lg_ref, x_ref, lng_ref, cos_ref, sin_ref, wq_ref, wk_ref, wv_ref, wg_ref, wo_ref,
                  out_ref, s_ref, *, n_sub):
    C, DK, DV, H = RET_CHUNK, RET_DK, RET_DV, RET_HEADS
    half = DK // 2

    @pl.when(pl.program_id(1) == 0)
    def _():
        s_ref[...] = jnp.zeros_like(s_ref)

    x = x_ref[...]
    h = _rms(x, lng_ref[...]).astype(BF16)
    cos = cos_ref[...]
    sin = sin_ref[...]
    ii = lax.broadcasted_iota(jnp.int32, (C, C), 0)
    jj = lax.broadcasted_iota(jnp.int32, (C, C), 1)
    diff = jnp.maximum((ii - jj).astype(F32), 0.0)
    idx = lax.broadcasted_iota(jnp.int32, (C, 1), 0).astype(F32)
    rows = lambda cc: slice(cc * C, (cc + 1) * C)

    def rotate(t):
        t1, t2 = t[:, :half], t[:, half:]
        return jnp.concatenate([t1 * cos - t2 * sin, t1 * sin + t2 * cos], axis=-1)

    def project(hd):
        qs, vs = slice(hd * DK, (hd + 1) * DK), slice(hd * DV, (hd + 1) * DV)
        q = rotate(jnp.dot(h, wq_ref[:, qs], preferred_element_type=F32))
        k = rotate(jnp.dot(h, wk_ref[:, qs], preferred_element_type=F32) * (DK ** -0.5))
        v = jnp.dot(h, wv_ref[:, vs], preferred_element_type=F32).astype(BF16)
        gz = jnp.dot(h, wg_ref[:, vs], preferred_element_type=F32)
        return q, k, v, gz * jax.nn.sigmoid(gz)

    def retain(hd, q, k, v):
        lg = lg_ref[hd]
        dmat = jnp.where(ii >= jj, jnp.exp(lg * diff), 0.0)
        q_dec = jnp.exp(lg * (idx + 1.0))
        k_dec = jnp.exp(lg * (C - 1.0 - idx))
        s_dec = jnp.exp(jnp.full((1, DV), lg * C, F32))
        subs = range(n_sub)
        inner = [_mm_nt(q[rows(cc)], k[rows(cc)]) * dmat for cc in subs]
        kv = [lax.dot_general((k[rows(cc)] * k_dec).astype(BF16), v[rows(cc)], TN_DIMS,
                              preferred_element_type=F32) for cc in subs]
        S = s_ref[0, hd]
        outs = []
        for cc in subs:
            outs.append(jnp.dot(inner[cc].astype(BF16), v[rows(cc)], preferred_element_type=F32)
                        + _mm(q[rows(cc)] * q_dec, S))
            S = S * s_dec + kv[cc]
        s_ref[0, hd] = S
        return jnp.concatenate(outs, axis=0)

    acc = x
    proj = project(0)
    for hd in range(H):
        nxt = project(hd + 1) if hd + 1 < H else None
        q, k, v, gate = proj
        o = retain(hd, q, k, v)
        acc = acc + _mm(gate * _head_norm(o, RET_GN_EPS), wo_ref[hd * DV:(hd + 1) * DV, :])
        proj = nxt
    out_ref[...] = acc


def _ret_mix(x, ln_g, cos, sin, p, *, n_seq, n_sub):
    M, D = x.shape
    H, DK, DV, C = RET_HEADS, RET_DK, RET_DV, RET_CHUNK
    T = C * n_sub
    nl = M // n_seq // T
    const = lambda b, l: (0, 0)
    tok = pl.BlockSpec((T, D), lambda b, l: (b * nl + l, 0))
    tab = pl.BlockSpec((T, DK // 2), lambda b, l: (l, 0))
    st = pl.BlockSpec((1, H, DK, DV), lambda b, l: (b, 0, 0, 0))
    weights = [p["wq"], p["wk"], p["wv"], p["wg"], p["wo"]]
    streamed = 2 * _nbytes((T, D), F32) + 2 * _nbytes((T, DK // 2), F32) + _nbytes((H, DK, DV), F32)
    resident = sum(_nbytes(w.shape, BF16) for w in weights)
    temps = 8 * _nbytes((T, DV), F32) + 4 * _nbytes((DK, DV), F32)
    return pl.pallas_call(
        functools.partial(_ret_mix_body, n_sub=n_sub), grid=(n_seq, nl),
        in_specs=[pl.BlockSpec(memory_space=pltpu.SMEM), tok, _resident((1, D), const), tab, tab]
                 + [_resident(w.shape, const) for w in weights],
        out_specs=[tok, st],
        out_shape=[jax.ShapeDtypeStruct((M, D), F32), jax.ShapeDtypeStruct((n_seq, H, DK, DV), F32)],
        compiler_params=pltpu.CompilerParams(dimension_semantics=("parallel", "arbitrary"),
                                             vmem_limit_bytes=_vmem_limit(streamed, resident, temps)),
        name="ret_mix",
    )(_ret_log_gamma(), x, ln_g.reshape(1, D), cos, sin, *weights)


def _ret_step_body(lg_ref, q_ref, k_ref, v_ref, s_ref, o_ref, so_ref):
    lg = lg_ref[pl.program_id(1)]
    q = q_ref[0]
    k = k_ref[0]
    v = v_ref[0]
    bt = q.shape[0]
    gam_k = jnp.exp(jnp.full((1, RET_DK), lg, F32))
    gam_v = jnp.exp(jnp.full((1, RET_DV), lg, F32))
    qg = q * gam_k
    vb = v.astype(BF16)
    rowid = lax.broadcasted_iota(jnp.int32, (SUBLANES, RET_DK), 0)
    inner = jnp.sum(q * k, axis=-1, keepdims=True)
    for grp in range(bt // SUBLANES):
        rs = slice(grp * SUBLANES, (grp + 1) * SUBLANES)
        o8 = inner[rs] * v[rs]
        for j in range(SUBLANES):
            b = grp * SUBLANES + j
            m = rowid == j
            S = s_ref[b, 0]
            o8 = o8 + _mm(jnp.where(m, qg[rs], 0.0), S)
            so_ref[b, 0] = S * gam_v + lax.dot_general(jnp.where(m, k[rs], 0.0).astype(BF16), vb[rs],
                                                       TN_DIMS, preferred_element_type=F32)
        o_ref[0, rs, :] = o8


def _ret_step(q, k, v, state, *, bt):
    B = q.shape[0]
    H, DK, DV = RET_HEADS, RET_DK, RET_DV
    heads = lambda t, w: t.reshape(B, H, w).transpose(1, 0, 2)
    tok = lambda w: pl.BlockSpec((1, bt, w), lambda i, h: (h, i, 0))
    st = pl.BlockSpec((bt, 1, DK, DV), lambda i, h: (i, h, 0, 0))
    blocks = 2 * _nbytes((bt, DK, DV), F32) + 4 * _nbytes((bt, DV), F32)
    o, s_new = pl.pallas_call(
        _ret_step_body, grid=(B // bt, H),
        in_specs=[pl.BlockSpec(memory_space=pltpu.SMEM), tok(DK), tok(DK), tok(DV), st],
        out_specs=[tok(DV), st],
        out_shape=[jax.ShapeDtypeStruct((H, B, DV), F32), jax.ShapeDtypeStruct((B, H, DK, DV), F32)],
        compiler_params=pltpu.CompilerParams(dimension_semantics=("parallel", "parallel"),
                                             vmem_limit_bytes=_vmem_limit(blocks, 0, 4 * _nbytes((DK, DV), F32))),
        name="ret_step",
    )(_ret_log_gamma(), heads(q, DK), heads(k, DK), heads(v, DV), state)
    return o.transpose(1, 0, 2).reshape(B, H * DV), s_new


def _ret_post_body(x_ref, o_ref, gate_ref, wo_ref, out_ref):
    acc = x_ref[...]
    for hd in range(RET_HEADS):
        sl = slice(hd * RET_DV, (hd + 1) * RET_DV)
        acc = acc + _mm(gate_ref[:, sl] * _head_norm(o_ref[:, sl], RET_GN_EPS), wo_ref[sl, :])
    out_ref[...] = acc


def _ret_post(x, o, gate, wo, *, tm):
    M, D = x.shape
    W = o.shape[1]
    tok = lambda w: pl.BlockSpec((tm, w), lambda i: (i, 0))
    streamed = 2 * _nbytes((tm, D), F32) + 2 * _nbytes((tm, W), F32)
    return pl.pallas_call(
        _ret_post_body, grid=(M // tm,),
        in_specs=[tok(D), tok(W), tok(W), _resident(wo.shape, lambda i: (0, 0))],
        out_specs=tok(D), out_shape=jax.ShapeDtypeStruct((M, D), F32),
        compiler_params=pltpu.CompilerParams(
            dimension_semantics=("parallel",),
            vmem_limit_bytes=_vmem_limit(streamed, _nbytes(wo.shape, BF16), 3 * _nbytes((tm, W), F32))),
        name="ret_post",
    )(x, o, gate, wo)


def _rope_tables(pos):
    half = RET_DK // 2
    inv = ROPE_BASE ** (-jnp.arange(half, dtype=F32) / half)
    ang = pos.astype(F32)[:, None] * inv[None, :]
    return jnp.cos(ang), jnp.sin(ang)


def _trunk(x, n_seq, pos, shift0, wkv0, ret0, w, tiles):
    M, D = x.shape
    L = M // n_seq
    cos, sin = _rope_tables(pos)
    if L == 1:
        cos, sin = jnp.tile(cos, (M, 1)), jnp.tile(sin, (M, 1))
    new_shift, new_wkv, new_ret = [], [], []
    for i in range(DEPTH):
        x = _ffn(x, w["ln_ffn1"][i], w["ff1_wg"][i], w["ff1_wu"][i], w["ff1_wd"][i],
                 tm=tiles["ffn_m"], tf=tiles["ffn_f"])
        j = i // N_MIXERS
        if i % N_MIXERS == 0:
            p = {n: w["rw_" + n][j] for n in ("mu", "w0", "w1", "w2", "a0", "a1", "a2", "g1", "g2", "kk", "ka",
                                              "rk", "wr", "wk", "wv", "wo", "lnx_g", "lnx_b")}
            p["ln"] = w["ln_mix"][i]
            prev = jnp.zeros((n_seq, D), F32) if shift0 is None else shift0[j]
            r, k, v, lw, a, g, sh = _rwkv_pre(x, prev, p, n_seq=n_seq, tile=tiles["rwkv_pre"])
            if L > 1:
                y, st = _wkv_chunked(r, k, v, lw, a, p, n_seq=n_seq, n_s=tiles["wkv_seqs"])
            else:
                y, st = _wkv_step(r, k, v, lw, a, p, wkv0[j])
            x = _rwkv_post(x, y, g, p["wo"], tm=tiles["post"])
            new_shift.append(sh)
            new_wkv.append(st)
        else:
            p = {n: w["ret_" + n][j] for n in ("wq", "wk", "wv", "wg", "wo")}
            if L > 1:
                x, st = _ret_mix(x, w["ln_mix"][i], cos, sin, p, n_seq=n_seq, n_sub=tiles["ret_sub"])
            else:
                q, k, v, gate = _ret_pre(x, w["ln_mix"][i], cos, sin, p, tm=tiles["ret_pre"])
                o, st = _ret_step(q, k, v, ret0[j], bt=tiles["ret_step"])
                x = _ret_post(x, o, gate, p["wo"], tm=tiles["post"])
            new_ret.append(st)
        x = _ffn(x, w["ln_ffn2"][i], w["ff2_wg"][i], w["ff2_wu"][i], w["ff2_wd"][i],
                 w["ln_final"] if i == DEPTH - 1 else None, tm=tiles["ffn_m"], tf=tiles["ffn_f"])
    return x, jnp.stack(new_shift), jnp.stack(new_wkv), jnp.stack(new_ret)


PROMPT_TILES = dict(ffn_m=1024, ffn_f=256, rwkv_pre=256, post=256, ret_sub=4, wkv_seqs=2)
SAMPLE_TILES = dict(ffn_m=128, ffn_f=256, rwkv_pre=128, post=128, ret_pre=128, ret_step=8)

MATMUL_WEIGHTS = ("ff1_wg", "ff1_wu", "ff1_wd", "ff2_wg", "ff2_wu", "ff2_wd", "rw_w1", "rw_w2", "rw_a1", "rw_a2",
                  "rw_g1", "rw_g2", "rw_wr", "rw_wk", "rw_wv", "rw_wo", "ret_wq", "ret_wk", "ret_wv", "ret_wg",
                  "ret_wo")


def kernel(x_prompt, x_sample, state_rwkv_shift, state_rwkv_wkv, state_ret, ln_ffn1, ff1_wg, ff1_wu, ff1_wd, ln_mix, ln_ffn2, ff2_wg, ff2_wu, ff2_wd, ln_final, rw_mu, rw_w0, rw_w1, rw_w2, rw_a0, rw_a1, rw_a2, rw_g1, rw_g2, rw_kk, rw_ka, rw_rk, rw_wr, rw_wk, rw_wv, rw_wo, rw_lnx_g, rw_lnx_b, ret_wq, ret_wk, ret_wv, ret_wg, ret_wo):
    w = dict(ln_ffn1=ln_ffn1, ff1_wg=ff1_wg, ff1_wu=ff1_wu, ff1_wd=ff1_wd, ln_mix=ln_mix,
             ln_ffn2=ln_ffn2, ff2_wg=ff2_wg, ff2_wu=ff2_wu, ff2_wd=ff2_wd, ln_final=ln_final,
             rw_mu=rw_mu, rw_w0=rw_w0, rw_w1=rw_w1, rw_w2=rw_w2, rw_a0=rw_a0, rw_a1=rw_a1,
             rw_a2=rw_a2, rw_g1=rw_g1, rw_g2=rw_g2, rw_kk=rw_kk, rw_ka=rw_ka, rw_rk=rw_rk,
             rw_wr=rw_wr, rw_wk=rw_wk, rw_wv=rw_wv, rw_wo=rw_wo, rw_lnx_g=rw_lnx_g,
             rw_lnx_b=rw_lnx_b, ret_wq=ret_wq, ret_wk=ret_wk, ret_wv=ret_wv, ret_wg=ret_wg,
             ret_wo=ret_wo)
    for name in MATMUL_WEIGHTS:
        w[name] = [w[name][i].astype(BF16) for i in range(w[name].shape[0])]
    B, L, D = x_prompt.shape
    Bs, Ls, _ = x_sample.shape
    y_p, p_shift, p_wkv, p_ret = _trunk(x_prompt.reshape(B * L, D), B, jnp.arange(L, dtype=F32),
                                        None, None, None, w, PROMPT_TILES)
    pos_s = PAST_LEN + jnp.arange(Ls, dtype=F32)
    y_s, s_shift, s_wkv, s_ret = _trunk(x_sample.reshape(Bs * Ls, D), Bs, pos_s,
                                        state_rwkv_shift, state_rwkv_wkv, state_ret, w, SAMPLE_TILES)
    return (y_p.reshape(B, L, D), y_s.reshape(Bs, Ls, D), p_shift, p_wkv, p_ret, s_shift, s_wkv, s_ret)
```

```python
import functools
import math

import numpy as np
import jax
import jax.numpy as jnp
from jax import lax
from jax.experimental import pallas as pl
from jax.experimental.pallas import tpu as pltpu

F32 = jnp.float32
BF16 = jnp.bfloat16
HIGHEST = lax.Precision.HIGHEST

DEPTH = 2
N_MIXERS = 2
RW_HEADS = 16
RW_HEAD = 64
RW_GN_EPS = 64e-5
RET_HEADS = 4
RET_DK = 256
RET_DV = 512
RET_CHUNK = 128
RET_GN_EPS = 1e-6
ROPE_BASE = 10000.0
FFN_RES = 0.5
RMS_EPS = 1e-6
PAST_LEN = 16384

WKV_CHUNK = 64
V7X_VMEM_CAP_BYTES = 60000 * 1024
SUBLANES = 8

NT_DIMS = (((1,), (1,)), ((), ()))
TN_DIMS = (((0,), (0,)), ((), ()))


def _vmem_limit(block_bytes, scratch_bytes=0, temp_bytes=0):
    return int(min(V7X_VMEM_CAP_BYTES, 2 * block_bytes + scratch_bytes + temp_bytes + (4 << 20)))


def _nbytes(shape, dtype):
    return int(np.prod(shape)) * jnp.dtype(dtype).itemsize


def _resident(shape, index_map):
    return pl.BlockSpec(shape, index_map, pipeline_mode=pl.Buffered(1))


def _rms(x, g):
    return x * lax.rsqrt(jnp.mean(x * x, axis=-1, keepdims=True) + RMS_EPS) * g


def _head_norm(y, eps):
    mu = jnp.mean(y, axis=-1, keepdims=True)
    yc = y - mu
    return yc * lax.rsqrt(jnp.mean(yc * yc, axis=-1, keepdims=True) + eps)


def _mm(a, b):
    return jnp.dot(a.astype(BF16), b.astype(BF16), preferred_element_type=F32)


def _mm_nt(a, b):
    return lax.dot_general(a.astype(BF16), b.astype(BF16), NT_DIMS, preferred_element_type=F32)


def _mm_tn(a, b):
    return lax.dot_general(a.astype(BF16), b.astype(BF16), TN_DIMS, preferred_element_type=F32)


def _ffn_body(*refs, ff_tile, final):
    if final:
        x_ref, g_ref, wg_ref, wu_ref, wd_ref, gf_ref, o_ref = refs
    else:
        x_ref, g_ref, wg_ref, wu_ref, wd_ref, o_ref = refs
    x = x_ref[...]
    h = _rms(x, g_ref[...]).astype(BF16)
    n_tiles = wg_ref.shape[1] // ff_tile

    def activation(j):
        sl = slice(j * ff_tile, (j + 1) * ff_tile)
        gate = jnp.dot(h, wg_ref[:, sl], preferred_element_type=F32)
        up = jnp.dot(h, wu_ref[:, sl], preferred_element_type=F32)
        return (gate * jax.nn.sigmoid(gate) * up).astype(BF16)

    act = activation(0)
    acc = None
    for j in range(n_tiles):
        nxt = activation(j + 1) if j + 1 < n_tiles else None
        part = jnp.dot(act, wd_ref[j * ff_tile:(j + 1) * ff_tile, :], preferred_element_type=F32)
        acc = part if acc is None else acc + part
        act = nxt
    y = x + FFN_RES * acc
    if final:
        y = _rms(y, gf_ref[...])
    o_ref[...] = y


def _ffn(x, ln_g, wg, wu, wd, ln_final=None, *, tm, tf):
    M, D = x.shape
    FF = wg.shape[1]
    final = ln_final is not None
    row = lambda i: (i, 0)
    const = lambda i: (0, 0)
    in_specs = [pl.BlockSpec((tm, D), row), _resident((1, D), const), _resident((D, FF), const),
                _resident((D, FF), const), _resident((FF, D), const)]
    args = [x, ln_g.reshape(1, D), wg, wu, wd]
    if final:
        in_specs.append(_resident((1, D), const))
        args.append(ln_final.reshape(1, D))
    streamed = 2 * _nbytes((tm, D), F32)
    resident = 3 * _nbytes((D, FF), BF16)
    temps = 6 * _nbytes((tm, tf), F32) + 3 * _nbytes((tm, D), F32)
    return pl.pallas_call(
        functools.partial(_ffn_body, ff_tile=tf, final=final),
        grid=(M // tm,), in_specs=in_specs, out_specs=pl.BlockSpec((tm, D), row),
        out_shape=jax.ShapeDtypeStruct((M, D), F32),
        compiler_params=pltpu.CompilerParams(dimension_semantics=("parallel",),
                                             vmem_limit_bytes=_vmem_limit(streamed, resident, temps)),
        name="ffn_final" if final else "ffn",
    )(*args)


def _rwkv_project(h, xprev, mu_ref, w0_ref, a0_ref, wr_ref, wk_ref, wv_ref, w1_ref, w2_ref, a1_ref, a2_ref,
                  g1_ref, g2_ref):
    xx = xprev - h
    xr, xw, xk, xv, xa, xg = (h + xx * mu_ref[i:i + 1, :] for i in range(6))
    r = _mm(xr, wr_ref[...])
    k = _mm(xk, wk_ref[...])
    v = _mm(xv, wv_ref[...])
    z = w0_ref[...] + _mm(jnp.tanh(_mm(xw, w1_ref[...])), w2_ref[...])
    lw = -math.exp(-0.5) * jax.nn.sigmoid(z)
    a = jax.nn.sigmoid(a0_ref[...] + _mm(_mm(xa, a1_ref[...]), a2_ref[...]))
    g = _mm(jax.nn.sigmoid(_mm(xg, g1_ref[...])), g2_ref[...])
    return r, k, v, lw, a, g


def _rwkv_pre_body(x_ref, lng_ref, prev_ref, *refs):
    weight_refs, out_refs = refs[:12], refs[12:]
    h = _rms(x_ref[...], lng_ref[...])
    for ref, val in zip(out_refs, _rwkv_project(h, prev_ref[...], *weight_refs) + (h,)):
        ref[...] = val


def _rwkv_pre(x, prev, p, *, tile):
    M, D = x.shape
    vec = lambda a: a.reshape(1, D)
    row = lambda i: (i, 0)
    tok = pl.BlockSpec((tile, D), row)
    full = lambda a: _resident(a.shape, lambda i: (0, 0))
    params = [p["mu"], vec(p["w0"]), vec(p["a0"])]
    weights = [p["wr"], p["wk"], p["wv"], p["w1"], p["w2"], p["a1"], p["a2"], p["g1"], p["g2"]]
    streamed = 9 * _nbytes((tile, D), F32)
    resident = sum(_nbytes(w.shape, BF16) for w in weights)
    return pl.pallas_call(
        _rwkv_pre_body, grid=(M // tile,),
        in_specs=[tok, full(vec(p["ln"])), tok] + [full(t) for t in params + weights],
        out_specs=[tok] * 7,
        out_shape=[jax.ShapeDtypeStruct((M, D), F32)] * 7,
        compiler_params=pltpu.CompilerParams(
            dimension_semantics=("parallel",),
            vmem_limit_bytes=_vmem_limit(streamed, resident, 10 * _nbytes((tile, D), F32))),
        name="rwkv_pre",
    )(x, vec(p["ln"]), prev, *params, *weights)


def _wkv_chunk(r, k, v, lw, a, kkw, kaw, rkw, lnx_g, lnx_b, S, n_s):
    C = WKV_CHUNK
    N = RW_HEAD
    W = 2 * N
    T = n_s * C
    ii = lax.broadcasted_iota(jnp.int32, (T, T), 0)
    jj = lax.broadcasted_iota(jnp.int32, (T, T), 1)
    shift = C.bit_length() - 1
    tri = (ii >= jj) & (jnp.right_shift(ii, shift) == jnp.right_shift(jj, shift))
    cl = jnp.dot(tri.astype(F32), lw, precision=HIGHEST, preferred_element_type=F32)
    gls = [cl[s * C + C - 1:s * C + C, :] for s in range(n_s)]
    gl = jnp.concatenate([jnp.broadcast_to(g, (C, g.shape[-1])) for g in gls], axis=0)
    e_cl = jnp.exp(cl)
    e_ncl = jnp.exp(-cl)
    e_ce = jnp.exp(cl - lw)
    e_g = jnp.exp(gl - cl)
    dgs = [jnp.exp(g) for g in gls]
    kkraw = k * kkw
    kmod = k * (1.0 + (a - 1.0) * kaw)
    rkk = r * kmod * rkw
    rt = r * e_cl
    kt = kmod * e_ncl
    kg = kmod * e_g
    row = lax.broadcasted_iota(jnp.int32, (C, W), 0)
    lane = lax.broadcasted_iota(jnp.int32, (C, W), 1)
    left = lane < N
    tok = jnp.where(left, lane, lane - N)
    strict = row > tok
    incl = row >= tok
    same_head = ((lax.broadcasted_iota(jnp.int32, (W, W), 0) < N)
                 == (lax.broadcasted_iota(jnp.int32, (W, W), 1) < N))

    def seg_sum(x):
        s_l = jnp.sum(jnp.where(left, x, 0.0), axis=-1, keepdims=True)
        s_r = jnp.sum(jnp.where(left, 0.0, x), axis=-1, keepdims=True)
        return jnp.where(left, s_l, s_r)

    def bd(x):
        xb = x.astype(BF16)
        zero = jnp.zeros_like(xb)
        return jnp.concatenate([jnp.where(left, xb, zero), jnp.where(left, zero, xb)], axis=0)

    units = list(S)
    at = {u: (slice(u[0] * C, (u[0] + 1) * C), slice(u[1] * W, (u[1] + 1) * W)) for u in units}
    V2 = {u: v[at[u]] for u in units}
    b2, lhs2, rhs_nt = {}, {}, {}
    for u in units:
        kk2 = kkraw[at[u]]
        kkn = kk2 / jnp.maximum(jnp.sqrt(seg_sum(kk2 * kk2)), 1e-12)
        b2[u] = kkn * a[at[u]]
        lhs2[u] = jnp.concatenate([-kkn * e_ce[at[u]], rt[at[u]]], axis=0).astype(BF16)
        rhs_nt[u] = jnp.concatenate([bd(b2[u] * e_ncl[at[u]]), bd(kt[at[u]])], axis=0)
    P = {u: lax.dot_general(lhs2[u], rhs_nt[u], NT_DIMS, preferred_element_type=F32) for u in units}
    LS = {u: lax.dot_general(lhs2[u], S[u].astype(BF16), NT_DIMS, preferred_element_type=F32) for u in units}
    bd_v = {u: bd(V2[u]) for u in units}
    X = {u: LS[u][:C] + jnp.dot(jnp.where(strict, P[u][:C, W:], 0.0).astype(BF16), bd_v[u],
                                preferred_element_type=F32) for u in units}
    Pk = {u: jnp.where(strict, P[u][:C, :W], 0.0) for u in units}
    span = 1
    while span < C:
        span *= 2
        if span < C:
            Z = {u: jnp.dot(Pk[u].astype(BF16), jnp.concatenate([bd(X[u]), bd(Pk[u])], axis=1),
                            preferred_element_type=F32) for u in units}
            X = {u: X[u] + Z[u][:, :W] for u in units}
            Pk = {u: Z[u][:, W:] for u in units}
        else:
            X = {u: X[u] + jnp.dot(Pk[u].astype(BF16), bd(X[u]), preferred_element_type=F32) for u in units}
    incl2 = jnp.concatenate([incl, incl], axis=1)
    Y = {u: LS[u][C:] + jnp.dot(jnp.where(incl2, P[u][C:], 0.0).astype(BF16),
                                jnp.concatenate([bd(X[u]), bd_v[u]], axis=0), preferred_element_type=F32)
         for u in units}
    s_new = {u: S[u] * dgs[u[0]][:, at[u][1]]
             + jnp.where(same_head,
                         _mm_tn(jnp.concatenate([X[u], V2[u]], axis=0),
                                jnp.concatenate([b2[u] * e_g[at[u]], kg[at[u]]], axis=0)), 0.0)
             for u in units}
    y = {}
    for u in units:
        sl = at[u][1]
        yc = Y[u] - seg_sum(Y[u]) * (1.0 / N)
        yn = yc * lax.rsqrt(seg_sum(yc * yc) * (1.0 / N) + RW_GN_EPS)
        y[u] = yn * lnx_g[:, sl] + lnx_b[:, sl] + seg_sum(rkk[at[u]]) * V2[u]
    return y, s_new


def _rwkv_mix_body(x_ref, lng_ref, prev_ref, mu_ref, w0_ref, a0_ref, kkw_ref, kaw_ref, rk_ref, lg_ref, lb_ref,
                   wr_ref, wk_ref, wv_ref, wo_ref, w1_ref, w2_ref, a1_ref, a2_ref, g1_ref, g2_ref,
                   out_ref, so_ref, s_ref,
                   carry, sbd_scr, r_scr, k_scr, v_scr, lw_scr, a_scr, y_scr, *, n_steps):
    C, N, W = WKV_CHUNK, RW_HEAD, 2 * RW_HEAD
    n_s, T, D = x_ref.shape
    step = pl.program_id(1)

    @pl.when(step == 0)
    def _():
        carry[...] = prev_ref[...]
        sbd_scr[...] = jnp.zeros_like(sbd_scr)

    x = x_ref[...].reshape(n_s * T, D)
    h = _rms(x, lng_ref[...])
    row = lax.broadcasted_iota(jnp.int32, h.shape, 0)
    xprev = pltpu.roll(h, 1, 0)
    for s in range(n_s):
        xprev = jnp.where(row == s * T, carry[s], xprev)
        last = h[s * T + T - 1:s * T + T, :]
        carry[s] = last
        so_ref[s] = last
    r_scr[...], k_scr[...], v_scr[...], lw_scr[...], a_scr[...], g = _rwkv_project(
        h, xprev, mu_ref, w0_ref, a0_ref, wr_ref, wk_ref, wv_ref, w1_ref, w2_ref, a1_ref, a2_ref, g1_ref, g2_ref)
    kkw, kaw, rkw, lnx_g, lnx_b = kkw_ref[...], kaw_ref[...], rk_ref[...], lg_ref[...], lb_ref[...]
    units = [(s, p) for s in range(n_s) for p in range(RW_HEADS // 2)]

    def chunk(c, loop_carry):
        starts = [pl.multiple_of(s * T + c * C, C) for s in range(n_s)]
        rows_of = lambda ref: jnp.concatenate([ref[pl.ds(st, C), :] for st in starts], axis=0)
        S = {u: sbd_scr[u[0], u[1]] for u in units}
        y, s_new = _wkv_chunk(rows_of(r_scr), rows_of(k_scr), rows_of(v_scr), rows_of(lw_scr), rows_of(a_scr),
                              kkw, kaw, rkw, lnx_g, lnx_b, S, n_s)
        for u in units:
            y_scr[pl.ds(starts[u[0]], C), u[1] * W:(u[1] + 1) * W] = y[u]
            sbd_scr[u[0], u[1]] = s_new[u]
        return loop_carry

    lax.fori_loop(0, T // C, chunk, 0)
    out_ref[...] = (x + _mm(y_scr[...] * g, wo_ref[...])).reshape(n_s, T, D)

    @pl.when(step == n_steps - 1)
    def _():
        for s, p in units:
            S = sbd_scr[s, p]
            s_ref[s, 2 * p] = S[:N, :N]
            s_ref[s, 2 * p + 1] = S[N:, N:]


def _rwkv_mix(x, prev, p, *, n_seq, n_s, T):
    M, D = x.shape
    L = M // n_seq
    nl = L // T
    H, N = RW_HEADS, RW_HEAD
    vec = lambda t: t.reshape(1, D)
    const = lambda b, l: (0, 0)
    tok = pl.BlockSpec((n_s, T, D), lambda b, l: (b, l, 0))
    row1 = pl.BlockSpec((n_s, 1, D), lambda b, l: (b, 0, 0))
    st = pl.BlockSpec((n_s, H, N, N), lambda b, l: (b, 0, 0, 0))
    params = [vec(p["ln"]), p["mu"], vec(p["w0"]), vec(p["a0"]), vec(p["kk"]), vec(p["ka"]), vec(p["rk"]),
              vec(p["lnx_g"]), vec(p["lnx_b"])]
    weights = [p["wr"], p["wk"], p["wv"], p["wo"], p["w1"], p["w2"], p["a1"], p["a2"], p["g1"], p["g2"]]
    full = lambda t: _resident(t.shape, const)
    in_specs = [tok, full(params[0]), row1] + [full(t) for t in params[1:]] + [full(t) for t in weights]
    rows = (n_s * T, D)
    pair_state = (n_s, H // 2, 2 * N, 2 * N)
    scratch = [pltpu.VMEM((n_s, 1, D), F32), pltpu.VMEM(pair_state, F32)] + [pltpu.VMEM(rows, F32)] * 6
    streamed = 2 * _nbytes((n_s, T, D), F32) + _nbytes((n_s, H, N, 128), F32)
    resident = sum(_nbytes(t.shape, BF16) for t in weights) + _nbytes(pair_state, F32) + 6 * _nbytes(rows, F32)
    out, so, state = pl.pallas_call(
        functools.partial(_rwkv_mix_body, n_steps=nl), grid=(n_seq // n_s, nl),
        in_specs=in_specs, out_specs=[tok, row1, st],
        out_shape=[jax.ShapeDtypeStruct((n_seq, L, D), F32), jax.ShapeDtypeStruct((n_seq, 1, D), F32),
                   jax.ShapeDtypeStruct((n_seq, H, N, N), F32)],
        scratch_shapes=scratch,
        compiler_params=pltpu.CompilerParams(dimension_semantics=("parallel", "arbitrary"),
                                             vmem_limit_bytes=_vmem_limit(streamed, resident, 12 * _nbytes(rows, F32))),
        name="rwkv_mix",
    )(x.reshape(n_seq, L, D), params[0], prev.reshape(n_seq, 1, D), *params[1:], *weights)
    return out.reshape(M, D), so.reshape(n_seq, D), state


def _wkv_step_body(r_ref, k_ref, v_ref, lw_ref, a_ref, kkw_ref, kaw_ref, rk_ref, lg_ref, lb_ref, s_ref,
                   y_ref, so_ref, y_scr):
    r = r_ref[0]
    k = k_ref[0]
    v = v_ref[0]
    a = a_ref[0]
    d = jnp.exp(lw_ref[0])
    kkh = k * kkw_ref[0]
    nrm = jnp.sqrt(jnp.sum(kkh * kkh, axis=0, keepdims=True))
    kkn = kkh / jnp.maximum(nrm, 1e-12)
    kmod = k * (1.0 + (a - 1.0) * kaw_ref[0])
    bh = kkn * a

    def value_channel(i, carry):
        S = s_ref[0, i]
        sa = -jnp.sum(S * kkn, axis=0, keepdims=True)
        s_new = S * d + sa * bh + v_ref[0, pl.ds(i, 1), :] * kmod
        so_ref[0, i] = s_new
        y_scr[pl.ds(i, 1), :] = jnp.sum(s_new * r, axis=0, keepdims=True)
        return carry

    lax.fori_loop(0, RW_HEAD, value_channel, 0, unroll=8)
    y = y_scr[...]
    mu = jnp.mean(y, axis=0, keepdims=True)
    yc = y - mu
    yn = yc * lax.rsqrt(jnp.mean(yc * yc, axis=0, keepdims=True) + RW_GN_EPS)
    y_ref[0] = yn * lg_ref[0] + lb_ref[0] + jnp.sum(r * kmod * rk_ref[0], axis=0, keepdims=True) * v


def _wkv_step(r, k, v, lw, a, p, state):
    B, D = r.shape
    H, N = RW_HEADS, RW_HEAD
    lanes = lambda t: t.T.reshape(H, N, B)
    par = lambda t: jnp.broadcast_to(t.reshape(H, N, 1), (H, N, B))
    vec = pl.BlockSpec((1, N, B), lambda h: (h, 0, 0))
    st = pl.BlockSpec((1, N, N, B), lambda h: (h, 0, 0, 0))
    blocks = 2 * _nbytes((N, N, B), F32) + 11 * _nbytes((N, B), F32)
    y, s_new = pl.pallas_call(
        _wkv_step_body, grid=(H,),
        in_specs=[vec] * 10 + [st],
        out_specs=[vec, st],
        out_shape=[jax.ShapeDtypeStruct((H, N, B), F32), jax.ShapeDtypeStruct((H, N, N, B), F32)],
        scratch_shapes=[pltpu.VMEM((N, B), F32)],
        compiler_params=pltpu.CompilerParams(dimension_semantics=("parallel",),
                                             vmem_limit_bytes=_vmem_limit(blocks, _nbytes((N, B), F32), 4 << 20)),
        name="wkv_step",
    )(lanes(r), lanes(k), lanes(v), lanes(lw), lanes(a),
      par(p["kk"]), par(p["ka"]), par(p["rk"]), par(p["lnx_g"]), par(p["lnx_b"]),
      jnp.transpose(state, (1, 2, 3, 0)))
    return y.reshape(D, B).T, jnp.transpose(s_new, (3, 0, 1, 2))


def _rwkv_post_body(x_ref, y_ref, g_ref, wo_ref, o_ref):
    o_ref[...] = x_ref[...] + _mm(y_ref[...] * g_ref[...], wo_ref[...])


def _rwkv_post(x, y, g, wo, *, tm):
    M, D = x.shape
    tok = pl.BlockSpec((tm, D), lambda i: (i, 0))
    streamed = 4 * _nbytes((tm, D), F32)
    return pl.pallas_call(
        _rwkv_post_body, grid=(M // tm,),
        in_specs=[tok, tok, tok, _resident(wo.shape, lambda i: (0, 0))],
        out_specs=tok, out_shape=jax.ShapeDtypeStruct((M, D), F32),
        compiler_params=pltpu.CompilerParams(
            dimension_semantics=("parallel",),
            vmem_limit_bytes=_vmem_limit(streamed, _nbytes(wo.shape, BF16), 3 * _nbytes((tm, D), F32))),
        name="rwkv_post",
    )(x, y, g, wo)


def _ret_pre_body(x_ref, lng_ref, cos_ref, sin_ref, wq_ref, wk_ref, wv_ref, wg_ref,
                  q_ref, k_ref, v_ref, gate_ref, h_scr):
    @pl.when(pl.program_id(1) == 0)
    def _():
        h_scr[...] = _rms(x_ref[...], lng_ref[...]).astype(BF16)

    h = h_scr[...]
    cos = cos_ref[...]
    sin = sin_ref[...]
    half = RET_DK // 2

    def rotate(t, out_ref):
        t1, t2 = t[:, :half], t[:, half:]
        out_ref[:, :half] = t1 * cos - t2 * sin
        out_ref[:, half:] = t1 * sin + t2 * cos

    rotate(jnp.dot(h, wq_ref[...], preferred_element_type=F32), q_ref)
    rotate(jnp.dot(h, wk_ref[...], preferred_element_type=F32) * (RET_DK ** -0.5), k_ref)
    v_ref[...] = jnp.dot(h, wv_ref[...], preferred_element_type=F32)
    gz = jnp.dot(h, wg_ref[...], preferred_element_type=F32)
    gate_ref[...] = gz * jax.nn.sigmoid(gz)


def _ret_pre(x, ln_g, cos, sin, p, *, tm):
    M, D = x.shape
    H, DK, DV = RET_HEADS, RET_DK, RET_DV
    n_tab = cos.shape[0] // tm
    tab = pl.BlockSpec((tm, DK // 2), lambda i, h: (i % n_tab, 0))
    col = lambda w: pl.BlockSpec((D, w), lambda i, h: (0, h))
    out = lambda w: pl.BlockSpec((tm, w), lambda i, h: (i, h))
    blocks = (_nbytes((tm, D), F32) + 2 * _nbytes((tm, DK // 2), F32) + 2 * _nbytes((D, DK), BF16)
              + 2 * _nbytes((D, DV), BF16) + 2 * _nbytes((tm, DK), F32) + 2 * _nbytes((tm, DV), F32))
    return pl.pallas_call(
        _ret_pre_body, grid=(M // tm, H),
        in_specs=[pl.BlockSpec((tm, D), lambda i, h: (i, 0)), pl.BlockSpec((1, D), lambda i, h: (0, 0)),
                  tab, tab, col(DK), col(DK), col(DV), col(DV)],
        out_specs=[out(DK), out(DK), out(DV), out(DV)],
        out_shape=[jax.ShapeDtypeStruct((M, H * DK), F32), jax.ShapeDtypeStruct((M, H * DK), F32),
                   jax.ShapeDtypeStruct((M, H * DV), F32), jax.ShapeDtypeStruct((M, H * DV), F32)],
        scratch_shapes=[pltpu.VMEM((tm, D), BF16)],
        compiler_params=pltpu.CompilerParams(
            dimension_semantics=("parallel", "arbitrary"),
            vmem_limit_bytes=_vmem_limit(blocks, _nbytes((tm, D), BF16), 4 * _nbytes((tm, DV), F32))),
        name="ret_pre",
    )(x, ln_g.reshape(1, D), cos, sin, p["wq"], p["wk"], p["wv"], p["wg"])


def _ret_log_gamma():
    return jnp.log1p(-jnp.exp2(-5.0 - jnp.arange(RET_HEADS, dtype=F32)))


def _ret_mix_body(lg_ref, x_ref, lng_ref, cos_ref, sin_ref, wq_ref, wk_ref, wv_ref, wg_ref, wo_ref,
                  out_ref, s_ref, *, n_sub):
    C, DK, DV, H = RET_CHUNK, RET_DK, RET_DV, RET_HEADS
    half = DK // 2

    @pl.when(pl.program_id(1) == 0)
    def _():
        s_ref[...] = jnp.zeros_like(s_ref)

    x = x_ref[...]
    h = _rms(x, lng_ref[...]).astype(BF16)
    cos = cos_ref[...]
    sin = sin_ref[...]
    ii = lax.broadcasted_iota(jnp.int32, (C, C), 0)
    jj = lax.broadcasted_iota(jnp.int32, (C, C), 1)
    diff = jnp.maximum((ii - jj).astype(F32), 0.0)
    idx = lax.broadcasted_iota(jnp.int32, (C, 1), 0).astype(F32)
    rows = lambda cc: slice(cc * C, (cc + 1) * C)

    def rotate(t):
        t1, t2 = t[:, :half], t[:, half:]
        return jnp.concatenate([t1 * cos - t2 * sin, t1 * sin + t2 * cos], axis=-1)

    def project(hd):
        qs, vs = slice(hd * DK, (hd + 1) * DK), slice(hd * DV, (hd + 1) * DV)
        q = rotate(jnp.dot(h, wq_ref[:, qs], preferred_element_type=F32))
        k = rotate(jnp.dot(h, wk_ref[:, qs], preferred_element_type=F32) * (DK ** -0.5))
        v = jnp.dot(h, wv_ref[:, vs], preferred_element_type=F32).astype(BF16)
        gz = jnp.dot(h, wg_ref[:, vs], preferred_element_type=F32)
        return q, k, v, gz * jax.nn.sigmoid(gz)

    def retain(hd, q, k, v):
        lg = lg_ref[hd]
        dmat = jnp.where(ii >= jj, jnp.exp(lg * diff), 0.0)
        q_dec = jnp.exp(lg * (idx + 1.0))
        k_dec = jnp.exp(lg * (C - 1.0 - idx))
        s_dec = jnp.exp(jnp.full((1, DV), lg * C, F32))
        subs = range(n_sub)
        inner = [_mm_nt(q[rows(cc)], k[rows(cc)]) * dmat for cc in subs]
        kv = [lax.dot_general((k[rows(cc)] * k_dec).astype(BF16), v[rows(cc)], TN_DIMS,
                              preferred_element_type=F32) for cc in subs]
        S = s_ref[0, hd]
        outs = []
        for cc in subs:
            outs.append(jnp.dot(inner[cc].astype(BF16), v[rows(cc)], preferred_element_type=F32)
                        + _mm(q[rows(cc)] * q_dec, S))
            S = S * s_dec + kv[cc]
        s_ref[0, hd] = S
        return jnp.concatenate(outs, axis=0)

    acc = x
    proj = project(0)
    for hd in range(H):
        nxt = project(hd + 1) if hd + 1 < H else None
        q, k, v, gate = proj
        o = retain(hd, q, k, v)
        acc = acc + _mm(gate * _head_norm(o, RET_GN_EPS), wo_ref[hd * DV:(hd + 1) * DV, :])
        proj = nxt
    out_ref[...] = acc


def _ret_mix(x, ln_g, cos, sin, p, *, n_seq, n_sub):
    M, D = x.shape
    H, DK, DV, C = RET_HEADS, RET_DK, RET_DV, RET_CHUNK
    T = C * n_sub
    nl = M // n_seq // T
    const = lambda b, l: (0, 0)
    tok = pl.BlockSpec((T, D), lambda b, l: (b * nl + l, 0))
    tab = pl.BlockSpec((T, DK // 2), lambda b, l: (l, 0))
    st = pl.BlockSpec((1, H, DK, DV), lambda b, l: (b, 0, 0, 0))
    weights = [p["wq"], p["wk"], p["wv"], p["wg"], p["wo"]]
    streamed = 2 * _nbytes((T, D), F32) + 2 * _nbytes((T, DK // 2), F32) + _nbytes((H, DK, DV), F32)
    resident = sum(_nbytes(w.shape, BF16) for w in weights)
    temps = 8 * _nbytes((T, DV), F32) + 4 * _nbytes((DK, DV), F32)
    return pl.pallas_call(
        functools.partial(_ret_mix_body, n_sub=n_sub), grid=(n_seq, nl),
        in_specs=[pl.BlockSpec(memory_space=pltpu.SMEM), tok, _resident((1, D), const), tab, tab]
                 + [_resident(w.shape, const) for w in weights],
        out_specs=[tok, st],
        out_shape=[jax.ShapeDtypeStruct((M, D), F32), jax.ShapeDtypeStruct((n_seq, H, DK, DV), F32)],
        compiler_params=pltpu.CompilerParams(dimension_semantics=("parallel", "arbitrary"),
                                             vmem_limit_bytes=_vmem_limit(streamed, resident, temps)),
        name="ret_mix",
    )(_ret_log_gamma(), x, ln_g.reshape(1, D), cos, sin, *weights)


def _ret_step_body(lg_ref, q_ref, k_ref, v_ref, s_ref, o_ref, so_ref):
    lg = lg_ref[pl.program_id(1)]
    q = q_ref[0]
    k = k_ref[0]
    v = v_ref[0]
    bt = q.shape[0]
    gam_k = jnp.exp(jnp.full((1, RET_DK), lg, F32))
    gam_v = jnp.exp(jnp.full((1, RET_DV), lg, F32))
    qg = q * gam_k
    vb = v.astype(BF16)
    rowid = lax.broadcasted_iota(jnp.int32, (SUBLANES, RET_DK), 0)
    inner = jnp.sum(q * k, axis=-1, keepdims=True)
    for grp in range(bt // SUBLANES):
        rs = slice(grp * SUBLANES, (grp + 1) * SUBLANES)
        o8 = inner[rs] * v[rs]
        for j in range(SUBLANES):
            b = grp * SUBLANES + j
            m = rowid == j
            S = s_ref[b, 0]
            o8 = o8 + _mm(jnp.where(m, qg[rs], 0.0), S)
            so_ref[b, 0] = S * gam_v + lax.dot_general(jnp.where(m, k[rs], 0.0).astype(BF16), vb[rs],
                                                       TN_DIMS, preferred_element_type=F32)
        o_ref[0, rs, :] = o8


def _ret_step(q, k, v, state, *, bt):
    B = q.shape[0]
    H, DK, DV = RET_HEADS, RET_DK, RET_DV
    heads = lambda t, w: t.reshape(B, H, w).transpose(1, 0, 2)
    tok = lambda w: pl.BlockSpec((1, bt, w), lambda i, h: (h, i, 0))
    st = pl.BlockSpec((bt, 1, DK, DV), lambda i, h: (i, h, 0, 0))
    blocks = 2 * _nbytes((bt, DK, DV), F32) + 4 * _nbytes((bt, DV), F32)
    o, s_new = pl.pallas_call(
        _ret_step_body, grid=(B // bt, H),
        in_specs=[pl.BlockSpec(memory_space=pltpu.SMEM), tok(DK), tok(DK), tok(DV), st],
        out_specs=[tok(DV), st],
        out_shape=[jax.ShapeDtypeStruct((H, B, DV), F32), jax.ShapeDtypeStruct((B, H, DK, DV), F32)],
        compiler_params=pltpu.CompilerParams(dimension_semantics=("parallel", "parallel"),
                                             vmem_limit_bytes=_vmem_limit(blocks, 0, 4 * _nbytes((DK, DV), F32))),
        name="ret_step",
    )(_ret_log_gamma(), heads(q, DK), heads(k, DK), heads(v, DV), state)
    return o.transpose(1, 0, 2).reshape(B, H * DV), s_new


def _ret_post_body(x_ref, o_ref, gate_ref, wo_ref, out_ref):
    acc = x_ref[...]
    for hd in range(RET_HEADS):
        sl = slice(hd * RET_DV, (hd + 1) * RET_DV)
        acc = acc + _mm(gate_ref[:, sl] * _head_norm(o_ref[:, sl], RET_GN_EPS), wo_ref[sl, :])
    out_ref[...] = acc


def _ret_post(x, o, gate, wo, *, tm):
    M, D = x.shape
    W = o.shape[1]
    tok = lambda w: pl.BlockSpec((tm, w), lambda i: (i, 0))
    streamed = 2 * _nbytes((tm, D), F32) + 2 * _nbytes((tm, W), F32)
    return pl.pallas_call(
        _ret_post_body, grid=(M // tm,),
        in_specs=[tok(D), tok(W), tok(W), _resident(wo.shape, lambda i: (0, 0))],
        out_specs=tok(D), out_shape=jax.ShapeDtypeStruct((M, D), F32),
        compiler_params=pltpu.CompilerParams(
            dimension_semantics=("parallel",),
            vmem_limit_bytes=_vmem_limit(streamed, _nbytes(wo.shape, BF16), 3 * _nbytes((tm, W), F32))),
        name="ret_post",
    )(x, o, gate, wo)


def _rope_tables(pos):
    half = RET_DK // 2
    inv = ROPE_BASE ** (-jnp.arange(half, dtype=F32) / half)
    ang = pos.astype(F32)[:, None] * inv[None, :]
    return jnp.cos(ang), jnp.sin(ang)


def _trunk(x, n_seq, pos, shift0, wkv0, ret0, w, tiles):
    M, D = x.shape
    L = M // n_seq
    cos, sin = _rope_tables(pos)
    if L == 1:
        cos, sin = jnp.tile(cos, (M, 1)), jnp.tile(sin, (M, 1))
    new_shift, new_wkv, new_ret = [], [], []
    for i in range(DEPTH):
        x = _ffn(x, w["ln_ffn1"][i], w["ff1_wg"][i], w["ff1_wu"][i], w["ff1_wd"][i],
                 tm=tiles["ffn_m"], tf=tiles["ffn_f"])
        j = i // N_MIXERS
        if i % N_MIXERS == 0:
            p = {n: w["rw_" + n][j] for n in ("mu", "w0", "w1", "w2", "a0", "a1", "a2", "g1", "g2", "kk", "ka",
                                              "rk", "wr", "wk", "wv", "wo", "lnx_g", "lnx_b")}
            p["ln"] = w["ln_mix"][i]
            prev = jnp.zeros((n_seq, D), F32) if shift0 is None else shift0[j]
            if L > 1:
                x, sh, st = _rwkv_mix(x, prev, p, n_seq=n_seq, n_s=tiles["wkv_seqs"], T=tiles["wkv_tokens"])
            else:
                r, k, v, lw, a, g, sh = _rwkv_pre(x, prev, p, tile=tiles["rwkv_pre"])
                y, st = _wkv_step(r, k, v, lw, a, p, wkv0[j])
                x = _rwkv_post(x, y, g, p["wo"], tm=tiles["post"])
            new_shift.append(sh)
            new_wkv.append(st)
        else:
            p = {n: w["ret_" + n][j] for n in ("wq", "wk", "wv", "wg", "wo")}
            if L > 1:
                x, st = _ret_mix(x, w["ln_mix"][i], cos, sin, p, n_seq=n_seq, n_sub=tiles["ret_sub"])
            else:
                q, k, v, gate = _ret_pre(x, w["ln_mix"][i], cos, sin, p, tm=tiles["ret_pre"])
                o, st = _ret_step(q, k, v, ret0[j], bt=tiles["ret_step"])
                x = _ret_post(x, o, gate, p["wo"], tm=tiles["post"])
            new_ret.append(st)
        x = _ffn(x, w["ln_ffn2"][i], w["ff2_wg"][i], w["ff2_wu"][i], w["ff2_wd"][i],
                 w["ln_final"] if i == DEPTH - 1 else None, tm=tiles["ffn_m"], tf=tiles["ffn_f"])
    return x, jnp.stack(new_shift), jnp.stack(new_wkv), jnp.stack(new_ret)


PROMPT_TILES = dict(ffn_m=1024, ffn_f=256, ret_sub=4, wkv_seqs=2, wkv_tokens=256)
SAMPLE_TILES = dict(ffn_m=128, ffn_f=256, rwkv_pre=128, post=128, ret_pre=128, ret_step=8)

MATMUL_WEIGHTS = ("ff1_wg", "ff1_wu", "ff1_wd", "ff2_wg", "ff2_wu", "ff2_wd", "rw_w1", "rw_w2", "rw_a1", "rw_a2",
                  "rw_g1", "rw_g2", "rw_wr", "rw_wk", "rw_wv", "rw_wo", "ret_wq", "ret_wk", "ret_wv", "ret_wg",
                  "ret_wo")


def kernel(x_prompt, x_sample, state_rwkv_shift, state_rwkv_wkv, state_ret, ln_ffn1, ff1_wg, ff1_wu, ff1_wd, ln_mix, ln_ffn2, ff2_wg, ff2_wu, ff2_wd, ln_final, rw_mu, rw_w0, rw_w1, rw_w2, rw_a0, rw_a1, rw_a2, rw_g1, rw_g2, rw_kk, rw_ka, rw_rk, rw_wr, rw_wk, rw_wv, rw_wo, rw_lnx_g, rw_lnx_b, ret_wq, ret_wk, ret_wv, ret_wg, ret_wo):
    w = dict(ln_ffn1=ln_ffn1, ff1_wg=ff1_wg, ff1_wu=ff1_wu, ff1_wd=ff1_wd, ln_mix=ln_mix,
             ln_ffn2=ln_ffn2, ff2_wg=ff2_wg, ff2_wu=ff2_wu, ff2_wd=ff2_wd, ln_final=ln_final,
             rw_mu=rw_mu, rw_w0=rw_w0, rw_w1=rw_w1, rw_w2=rw_w2, rw_a0=rw_a0, rw_a1=rw_a1,
             rw_a2=rw_a2, rw_g1=rw_g1, rw_g2=rw_g2, rw_kk=rw_kk, rw_ka=rw_ka, rw_rk=rw_rk,
             rw_wr=rw_wr, rw_wk=rw_wk, rw_wv=rw_wv, rw_wo=rw_wo, rw_lnx_g=rw_lnx_g,
             rw_lnx_b=rw_lnx_b, ret_wq=ret_wq, ret_wk=ret_wk, ret_wv=ret_wv, ret_wg=ret_wg,
             ret_wo=ret_wo)
    for name in MATMUL_WEIGHTS:
        w[name] = [w[name][i].astype(BF16) for i in range(w[name].shape[0])]
    B, L, D = x_prompt.shape
    Bs, Ls, _ = x_sample.shape
    y_p, p_shift, p_wkv, p_ret = _trunk(x_prompt.reshape(B * L, D), B, jnp.arange(L, dtype=F32),
                                        None, None, None, w, PROMPT_TILES)
    pos_s = PAST_LEN + jnp.arange(Ls, dtype=F32)
    y_s, s_shift, s_wkv, s_ret = _trunk(x_sample.reshape(Bs * Ls, D), Bs, pos_s,
                                        state_rwkv_shift, state_rwkv_wkv, state_ret, w, SAMPLE_TILES)
    return (y_p.reshape(B, L, D), y_s.reshape(Bs, Ls, D), p_shift, p_wkv, p_ret, s_shift, s_wkv, s_ret)
```

```python
import functools
import math

import numpy as np
import jax
import jax.numpy as jnp
from jax import lax
from jax.experimental import pallas as pl
from jax.experimental.pallas import tpu as pltpu

F32 = jnp.float32
BF16 = jnp.bfloat16

DEPTH = 2
N_MIXERS = 2
RW_HEADS = 16
RW_HEAD = 64
RW_GN_EPS = 64e-5
RET_HEADS = 4
RET_DK = 256
RET_DV = 512
RET_CHUNK = 128
RET_GN_EPS = 1e-6
ROPE_BASE = 10000.0
FFN_RES = 0.5
RMS_EPS = 1e-6
PAST_LEN = 16384

WKV_CHUNK = 64
V7X_VMEM_CAP_BYTES = 60000 * 1024
SUBLANES = 8

NT_DIMS = (((1,), (1,)), ((), ()))
TN_DIMS = (((0,), (0,)), ((), ()))


def _vmem_limit(block_bytes, scratch_bytes=0, temp_bytes=0):
    return int(min(V7X_VMEM_CAP_BYTES, 2 * block_bytes + scratch_bytes + temp_bytes + (4 << 20)))


def _nbytes(shape, dtype):
    return int(np.prod(shape)) * jnp.dtype(dtype).itemsize


def _resident(shape, index_map):
    return pl.BlockSpec(shape, index_map, pipeline_mode=pl.Buffered(1))


def _rms(x, g):
    return x * lax.rsqrt(jnp.mean(x * x, axis=-1, keepdims=True) + RMS_EPS) * g


def _head_norm(y, eps):
    mu = jnp.mean(y, axis=-1, keepdims=True)
    yc = y - mu
    return yc * lax.rsqrt(jnp.mean(yc * yc, axis=-1, keepdims=True) + eps)


def _mm(a, b):
    return jnp.dot(a.astype(BF16), b.astype(BF16), preferred_element_type=F32)


def _mm_nt(a, b):
    return lax.dot_general(a.astype(BF16), b.astype(BF16), NT_DIMS, preferred_element_type=F32)


def _mm_tn(a, b):
    return lax.dot_general(a.astype(BF16), b.astype(BF16), TN_DIMS, preferred_element_type=F32)


def _ffn_body(*refs, ff_tile, final):
    if final:
        x_ref, g_ref, wg_ref, wu_ref, wd_ref, gf_ref, o_ref = refs
    else:
        x_ref, g_ref, wg_ref, wu_ref, wd_ref, o_ref = refs
    x = x_ref[...]
    h = _rms(x, g_ref[...]).astype(BF16)
    n_tiles = wg_ref.shape[1] // ff_tile

    def activation(j):
        sl = slice(j * ff_tile, (j + 1) * ff_tile)
        gate = jnp.dot(h, wg_ref[:, sl], preferred_element_type=F32)
        up = jnp.dot(h, wu_ref[:, sl], preferred_element_type=F32)
        return (gate * jax.nn.sigmoid(gate) * up).astype(BF16)

    act = activation(0)
    acc = None
    for j in range(n_tiles):
        nxt = activation(j + 1) if j + 1 < n_tiles else None
        part = jnp.dot(act, wd_ref[j * ff_tile:(j + 1) * ff_tile, :], preferred_element_type=F32)
        acc = part if acc is None else acc + part
        act = nxt
    y = x + FFN_RES * acc
    if final:
        y = _rms(y, gf_ref[...])
    o_ref[...] = y


def _ffn(x, ln_g, wg, wu, wd, ln_final=None, *, layer, tm, tf):
    M, D = x.shape
    FF = wg.shape[2]
    final = ln_final is not None
    row = lambda i: (i, 0)
    const = lambda i: (0, 0)
    of_layer = lambda i: (layer, 0, 0)
    in_specs = [pl.BlockSpec((tm, D), row), _resident((1, D), const), _resident((None, D, FF), of_layer),
                _resident((None, D, FF), of_layer), _resident((None, FF, D), of_layer)]
    args = [x, ln_g.reshape(1, D), wg, wu, wd]
    if final:
        in_specs.append(_resident((1, D), const))
        args.append(ln_final.reshape(1, D))
    streamed = 2 * _nbytes((tm, D), F32)
    resident = 3 * _nbytes((D, FF), BF16)
    temps = 6 * _nbytes((tm, tf), F32) + 3 * _nbytes((tm, D), F32)
    return pl.pallas_call(
        functools.partial(_ffn_body, ff_tile=tf, final=final),
        grid=(M // tm,), in_specs=in_specs, out_specs=pl.BlockSpec((tm, D), row),
        out_shape=jax.ShapeDtypeStruct((M, D), F32),
        compiler_params=pltpu.CompilerParams(dimension_semantics=("parallel",),
                                             vmem_limit_bytes=_vmem_limit(streamed, resident, temps)),
        name="ffn_final" if final else "ffn",
    )(*args)


def _rwkv_project(h, xprev, mu_ref, w0_ref, a0_ref, wr_ref, wk_ref, wv_ref, w1_ref, w2_ref, a1_ref, a2_ref,
                  g1_ref, g2_ref):
    xx = xprev - h
    xr, xw, xk, xv, xa, xg = (h + xx * mu_ref[i:i + 1, :] for i in range(6))
    r = _mm(xr, wr_ref[...])
    k = _mm(xk, wk_ref[...])
    v = _mm(xv, wv_ref[...])
    z = w0_ref[...] + _mm(jnp.tanh(_mm(xw, w1_ref[...])), w2_ref[...])
    lw = -math.exp(-0.5) * jax.nn.sigmoid(z)
    a = jax.nn.sigmoid(a0_ref[...] + _mm(_mm(xa, a1_ref[...]), a2_ref[...]))
    g = _mm(jax.nn.sigmoid(_mm(xg, g1_ref[...])), g2_ref[...])
    return r, k, v, lw, a, g


def _rwkv_pre_body(x_ref, lng_ref, prev_ref, *refs):
    weight_refs, out_refs = refs[:12], refs[12:]
    h = _rms(x_ref[...], lng_ref[...])
    for ref, val in zip(out_refs, _rwkv_project(h, prev_ref[...], *weight_refs) + (h,)):
        ref[...] = val


def _rwkv_pre(x, prev, p, *, tile):
    M, D = x.shape
    vec = lambda a: a.reshape(1, D)
    row = lambda i: (i, 0)
    tok = pl.BlockSpec((tile, D), row)
    full = lambda a: _resident(a.shape, lambda i: (0, 0))
    params = [p["mu"], vec(p["w0"]), vec(p["a0"])]
    weights = [p["wr"], p["wk"], p["wv"], p["w1"], p["w2"], p["a1"], p["a2"], p["g1"], p["g2"]]
    streamed = 9 * _nbytes((tile, D), F32)
    resident = sum(_nbytes(w.shape, BF16) for w in weights)
    return pl.pallas_call(
        _rwkv_pre_body, grid=(M // tile,),
        in_specs=[tok, full(vec(p["ln"])), tok] + [full(t) for t in params + weights],
        out_specs=[tok] * 7,
        out_shape=[jax.ShapeDtypeStruct((M, D), F32)] * 7,
        compiler_params=pltpu.CompilerParams(
            dimension_semantics=("parallel",),
            vmem_limit_bytes=_vmem_limit(streamed, resident, 10 * _nbytes((tile, D), F32))),
        name="rwkv_pre",
    )(x, vec(p["ln"]), prev, *params, *weights)


def _wkv_chunk(r, k, v, lw, a, kkw, kaw, rkw, lnx_g, lnx_b, S, n_s):
    C = WKV_CHUNK
    N = RW_HEAD
    W = 2 * N
    T = n_s * C
    ii = lax.broadcasted_iota(jnp.int32, (T, T), 0)
    jj = lax.broadcasted_iota(jnp.int32, (T, T), 1)
    shift = C.bit_length() - 1
    tri = (ii >= jj) & (jnp.right_shift(ii, shift) == jnp.right_shift(jj, shift))
    lw_hi = lw.astype(BF16)
    rest = lw - lw_hi.astype(F32)
    lw_mid = rest.astype(BF16)
    lw_lo = (rest - lw_mid.astype(F32)).astype(BF16)
    tri = tri.astype(BF16)
    cl = (jnp.dot(tri, lw_hi, preferred_element_type=F32) + jnp.dot(tri, lw_mid, preferred_element_type=F32)
          + jnp.dot(tri, lw_lo, preferred_element_type=F32))
    gls = [cl[s * C + C - 1:s * C + C, :] for s in range(n_s)]
    gl = jnp.concatenate([jnp.broadcast_to(g, (C, g.shape[-1])) for g in gls], axis=0)
    e_cl = jnp.exp(cl)
    e_ncl = jnp.exp(-cl)
    e_ce = jnp.exp(cl - lw)
    e_g = jnp.exp(gl - cl)
    dgs = [jnp.exp(g) for g in gls]
    kkraw = k * kkw
    kmod = k * (1.0 + (a - 1.0) * kaw)
    rkk = r * kmod * rkw
    rt = r * e_cl
    kt = kmod * e_ncl
    kg = kmod * e_g
    row = lax.broadcasted_iota(jnp.int32, (C, W), 0)
    lane = lax.broadcasted_iota(jnp.int32, (C, W), 1)
    left = lane < N
    tok = jnp.where(left, lane, lane - N)
    strict = row > tok
    incl = row >= tok
    same_head = ((lax.broadcasted_iota(jnp.int32, (W, W), 0) < N)
                 == (lax.broadcasted_iota(jnp.int32, (W, W), 1) < N))

    def seg_sum(x):
        s_l = jnp.sum(jnp.where(left, x, 0.0), axis=-1, keepdims=True)
        s_r = jnp.sum(jnp.where(left, 0.0, x), axis=-1, keepdims=True)
        return jnp.where(left, s_l, s_r)

    def bd(x):
        xb = x.astype(BF16)
        zero = jnp.zeros_like(xb)
        return jnp.concatenate([jnp.where(left, xb, zero), jnp.where(left, zero, xb)], axis=0)

    units = list(S)
    at = {u: (slice(u[0] * C, (u[0] + 1) * C), slice(u[1] * W, (u[1] + 1) * W)) for u in units}
    V2 = {u: v[at[u]] for u in units}
    b2, lhs2, rhs_nt = {}, {}, {}
    for u in units:
        kk2 = kkraw[at[u]]
        kkn = kk2 / jnp.maximum(jnp.sqrt(seg_sum(kk2 * kk2)), 1e-12)
        b2[u] = kkn * a[at[u]]
        lhs2[u] = jnp.concatenate([-kkn * e_ce[at[u]], rt[at[u]]], axis=0).astype(BF16)
        rhs_nt[u] = jnp.concatenate([bd(b2[u] * e_ncl[at[u]]), bd(kt[at[u]])], axis=0)
    P = {u: lax.dot_general(lhs2[u], rhs_nt[u], NT_DIMS, preferred_element_type=F32) for u in units}
    LS = {u: lax.dot_general(lhs2[u], S[u].astype(BF16), NT_DIMS, preferred_element_type=F32) for u in units}
    bd_v = {u: bd(V2[u]) for u in units}
    X = {u: LS[u][:C] + jnp.dot(jnp.where(strict, P[u][:C, W:], 0.0).astype(BF16), bd_v[u],
                                preferred_element_type=F32) for u in units}
    Pk = {u: jnp.where(strict, P[u][:C, :W], 0.0) for u in units}
    span = 1
    while span < C:
        span *= 2
        if span < C:
            Z = {u: jnp.dot(Pk[u].astype(BF16), jnp.concatenate([bd(X[u]), bd(Pk[u])], axis=1),
                            preferred_element_type=F32) for u in units}
            X = {u: X[u] + Z[u][:, :W] for u in units}
            Pk = {u: Z[u][:, W:] for u in units}
        else:
            X = {u: X[u] + jnp.dot(Pk[u].astype(BF16), bd(X[u]), preferred_element_type=F32) for u in units}
    incl2 = jnp.concatenate([incl, incl], axis=1)
    Y = {u: LS[u][C:] + jnp.dot(jnp.where(incl2, P[u][C:], 0.0).astype(BF16),
                                jnp.concatenate([bd(X[u]), bd_v[u]], axis=0), preferred_element_type=F32)
         for u in units}
    s_new = {u: S[u] * dgs[u[0]][:, at[u][1]]
             + jnp.where(same_head,
                         _mm_tn(jnp.concatenate([X[u], V2[u]], axis=0),
                                jnp.concatenate([b2[u] * e_g[at[u]], kg[at[u]]], axis=0)), 0.0)
             for u in units}
    y = {}
    for u in units:
        sl = at[u][1]
        yc = Y[u] - seg_sum(Y[u]) * (1.0 / N)
        yn = yc * lax.rsqrt(seg_sum(yc * yc) * (1.0 / N) + RW_GN_EPS)
        y[u] = yn * lnx_g[:, sl] + lnx_b[:, sl] + seg_sum(rkk[at[u]]) * V2[u]
    return y, s_new


def _rwkv_mix_body(x_ref, lng_ref, prev_ref, mu_ref, w0_ref, a0_ref, kkw_ref, kaw_ref, rk_ref, lg_ref, lb_ref,
                   wr_ref, wk_ref, wv_ref, wo_ref, w1_ref, w2_ref, a1_ref, a2_ref, g1_ref, g2_ref,
                   out_ref, so_ref, s_ref,
                   carry, sbd_scr, r_scr, k_scr, v_scr, lw_scr, a_scr, y_scr, *, n_steps):
    C, N, W = WKV_CHUNK, RW_HEAD, 2 * RW_HEAD
    n_s, T, D = x_ref.shape
    step = pl.program_id(1)

    @pl.when(step == 0)
    def _():
        carry[...] = prev_ref[...]
        sbd_scr[...] = jnp.zeros_like(sbd_scr)

    x = x_ref[...].reshape(n_s * T, D)
    h = _rms(x, lng_ref[...])
    row = lax.broadcasted_iota(jnp.int32, h.shape, 0)
    xprev = pltpu.roll(h, 1, 0)
    for s in range(n_s):
        xprev = jnp.where(row == s * T, carry[s], xprev)
        last = h[s * T + T - 1:s * T + T, :]
        carry[s] = last
        so_ref[s] = last
    r_scr[...], k_scr[...], v_scr[...], lw_scr[...], a_scr[...], g = _rwkv_project(
        h, xprev, mu_ref, w0_ref, a0_ref, wr_ref, wk_ref, wv_ref, w1_ref, w2_ref, a1_ref, a2_ref, g1_ref, g2_ref)
    kkw, kaw, rkw, lnx_g, lnx_b = kkw_ref[...], kaw_ref[...], rk_ref[...], lg_ref[...], lb_ref[...]
    units = [(s, p) for s in range(n_s) for p in range(RW_HEADS // 2)]

    def chunk(c, loop_carry):
        starts = [pl.multiple_of(s * T + c * C, C) for s in range(n_s)]
        rows_of = lambda ref: jnp.concatenate([ref[pl.ds(st, C), :] for st in starts], axis=0)
        S = {u: sbd_scr[u[0], u[1]] for u in units}
        y, s_new = _wkv_chunk(rows_of(r_scr), rows_of(k_scr), rows_of(v_scr), rows_of(lw_scr), rows_of(a_scr),
                              kkw, kaw, rkw, lnx_g, lnx_b, S, n_s)
        for u in units:
            y_scr[pl.ds(starts[u[0]], C), u[1] * W:(u[1] + 1) * W] = y[u]
            sbd_scr[u[0], u[1]] = s_new[u]
        return loop_carry

    lax.fori_loop(0, T // C, chunk, 0)
    out_ref[...] = (x + _mm(y_scr[...] * g, wo_ref[...])).reshape(n_s, T, D)

    @pl.when(step == n_steps - 1)
    def _():
        for s, p in units:
            S = sbd_scr[s, p]
            s_ref[s, 2 * p] = S[:N, :N]
            s_ref[s, 2 * p + 1] = S[N:, N:]


def _rwkv_mix(x, prev, p, *, n_seq, n_s, T):
    M, D = x.shape
    L = M // n_seq
    nl = L // T
    H, N = RW_HEADS, RW_HEAD
    vec = lambda t: t.reshape(1, D)
    const = lambda b, l: (0, 0)
    tok = pl.BlockSpec((n_s, T, D), lambda b, l: (b, l, 0))
    row1 = pl.BlockSpec((n_s, 1, D), lambda b, l: (b, 0, 0))
    st = pl.BlockSpec((n_s, H, N, N), lambda b, l: (b, 0, 0, 0))
    params = [vec(p["ln"]), p["mu"], vec(p["w0"]), vec(p["a0"]), vec(p["kk"]), vec(p["ka"]), vec(p["rk"]),
              vec(p["lnx_g"]), vec(p["lnx_b"])]
    weights = [p["wr"], p["wk"], p["wv"], p["wo"], p["w1"], p["w2"], p["a1"], p["a2"], p["g1"], p["g2"]]
    full = lambda t: _resident(t.shape, const)
    in_specs = [tok, full(params[0]), row1] + [full(t) for t in params[1:]] + [full(t) for t in weights]
    rows = (n_s * T, D)
    pair_state = (n_s, H // 2, 2 * N, 2 * N)
    scratch = [pltpu.VMEM((n_s, 1, D), F32), pltpu.VMEM(pair_state, F32)] + [pltpu.VMEM(rows, F32)] * 6
    streamed = 2 * _nbytes((n_s, T, D), F32) + _nbytes((n_s, H, N, 128), F32)
    resident = sum(_nbytes(t.shape, BF16) for t in weights) + _nbytes(pair_state, F32) + 6 * _nbytes(rows, F32)
    out, so, state = pl.pallas_call(
        functools.partial(_rwkv_mix_body, n_steps=nl), grid=(n_seq // n_s, nl),
        in_specs=in_specs, out_specs=[tok, row1, st],
        out_shape=[jax.ShapeDtypeStruct((n_seq, L, D), F32), jax.ShapeDtypeStruct((n_seq, 1, D), F32),
                   jax.ShapeDtypeStruct((n_seq, H, N, N), F32)],
        scratch_shapes=scratch,
        compiler_params=pltpu.CompilerParams(dimension_semantics=("parallel", "arbitrary"),
                                             vmem_limit_bytes=_vmem_limit(streamed, resident, 12 * _nbytes(rows, F32))),
        name="rwkv_mix",
    )(x.reshape(n_seq, L, D), params[0], prev.reshape(n_seq, 1, D), *params[1:], *weights)
    return out.reshape(M, D), so.reshape(n_seq, D), state


def _wkv_step_body(r_ref, k_ref, v_ref, lw_ref, a_ref, kkw_ref, kaw_ref, rk_ref, lg_ref, lb_ref, s_ref,
                   y_ref, so_ref, y_scr):
    r = r_ref[0]
    k = k_ref[0]
    v = v_ref[0]
    a = a_ref[0]
    d = jnp.exp(lw_ref[0])
    kkh = k * kkw_ref[0]
    nrm = jnp.sqrt(jnp.sum(kkh * kkh, axis=0, keepdims=True))
    kkn = kkh / jnp.maximum(nrm, 1e-12)
    kmod = k * (1.0 + (a - 1.0) * kaw_ref[0])
    bh = kkn * a

    def value_channel(i, carry):
        S = s_ref[0, i]
        sa = -jnp.sum(S * kkn, axis=0, keepdims=True)
        s_new = S * d + sa * bh + v_ref[0, pl.ds(i, 1), :] * kmod
        so_ref[0, i] = s_new
        y_scr[pl.ds(i, 1), :] = jnp.sum(s_new * r, axis=0, keepdims=True)
        return carry

    lax.fori_loop(0, RW_HEAD, value_channel, 0, unroll=8)
    y = y_scr[...]
    mu = jnp.mean(y, axis=0, keepdims=True)
    yc = y - mu
    yn = yc * lax.rsqrt(jnp.mean(yc * yc, axis=0, keepdims=True) + RW_GN_EPS)
    y_ref[0] = yn * lg_ref[0] + lb_ref[0] + jnp.sum(r * kmod * rk_ref[0], axis=0, keepdims=True) * v


def _wkv_step(r, k, v, lw, a, p, state):
    B, D = r.shape
    H, N = RW_HEADS, RW_HEAD
    lanes = lambda t: t.T.reshape(H, N, B)
    par = lambda t: jnp.broadcast_to(t.reshape(H, N, 1), (H, N, B))
    vec = pl.BlockSpec((1, N, B), lambda h: (h, 0, 0))
    st = pl.BlockSpec((1, N, N, B), lambda h: (h, 0, 0, 0))
    blocks = 2 * _nbytes((N, N, B), F32) + 11 * _nbytes((N, B), F32)
    y, s_new = pl.pallas_call(
        _wkv_step_body, grid=(H,),
        in_specs=[vec] * 10 + [st],
        out_specs=[vec, st],
        out_shape=[jax.ShapeDtypeStruct((H, N, B), F32), jax.ShapeDtypeStruct((H, N, N, B), F32)],
        scratch_shapes=[pltpu.VMEM((N, B), F32)],
        compiler_params=pltpu.CompilerParams(dimension_semantics=("parallel",),
                                             vmem_limit_bytes=_vmem_limit(blocks, _nbytes((N, B), F32), 4 << 20)),
        name="wkv_step",
    )(lanes(r), lanes(k), lanes(v), lanes(lw), lanes(a),
      par(p["kk"]), par(p["ka"]), par(p["rk"]), par(p["lnx_g"]), par(p["lnx_b"]),
      jnp.transpose(state, (1, 2, 3, 0)))
    return y.reshape(D, B).T, jnp.transpose(s_new, (3, 0, 1, 2))


def _rwkv_post_body(x_ref, y_ref, g_ref, wo_ref, o_ref):
    o_ref[...] = x_ref[...] + _mm(y_ref[...] * g_ref[...], wo_ref[...])


def _rwkv_post(x, y, g, wo, *, tm):
    M, D = x.shape
    tok = pl.BlockSpec((tm, D), lambda i: (i, 0))
    streamed = 4 * _nbytes((tm, D), F32)
    return pl.pallas_call(
        _rwkv_post_body, grid=(M // tm,),
        in_specs=[tok, tok, tok, _resident(wo.shape, lambda i: (0, 0))],
        out_specs=tok, out_shape=jax.ShapeDtypeStruct((M, D), F32),
        compiler_params=pltpu.CompilerParams(
            dimension_semantics=("parallel",),
            vmem_limit_bytes=_vmem_limit(streamed, _nbytes(wo.shape, BF16), 3 * _nbytes((tm, D), F32))),
        name="rwkv_post",
    )(x, y, g, wo)


def _ret_pre_body(x_ref, lng_ref, cos_ref, sin_ref, wq_ref, wk_ref, wv_ref, wg_ref,
                  q_ref, k_ref, v_ref, gate_ref, h_scr):
    @pl.when(pl.program_id(1) == 0)
    def _():
        h_scr[...] = _rms(x_ref[...], lng_ref[...]).astype(BF16)

    h = h_scr[...]
    cos = cos_ref[...]
    sin = sin_ref[...]
    half = RET_DK // 2

    def rotate(t, out_ref):
        t1, t2 = t[:, :half], t[:, half:]
        out_ref[:, :half] = t1 * cos - t2 * sin
        out_ref[:, half:] = t1 * sin + t2 * cos

    rotate(jnp.dot(h, wq_ref[...], preferred_element_type=F32), q_ref)
    rotate(jnp.dot(h, wk_ref[...], preferred_element_type=F32) * (RET_DK ** -0.5), k_ref)
    v_ref[...] = jnp.dot(h, wv_ref[...], preferred_element_type=F32)
    gz = jnp.dot(h, wg_ref[...], preferred_element_type=F32)
    gate_ref[...] = gz * jax.nn.sigmoid(gz)


def _ret_pre(x, ln_g, cos, sin, p, *, tm):
    M, D = x.shape
    H, DK, DV = RET_HEADS, RET_DK, RET_DV
    n_tab = cos.shape[0] // tm
    tab = pl.BlockSpec((tm, DK // 2), lambda i, h: (i % n_tab, 0))
    col = lambda w: pl.BlockSpec((D, w), lambda i, h: (0, h))
    out = lambda w: pl.BlockSpec((tm, w), lambda i, h: (i, h))
    blocks = (_nbytes((tm, D), F32) + 2 * _nbytes((tm, DK // 2), F32) + 2 * _nbytes((D, DK), BF16)
              + 2 * _nbytes((D, DV), BF16) + 2 * _nbytes((tm, DK), F32) + 2 * _nbytes((tm, DV), F32))
    return pl.pallas_call(
        _ret_pre_body, grid=(M // tm, H),
        in_specs=[pl.BlockSpec((tm, D), lambda i, h: (i, 0)), pl.BlockSpec((1, D), lambda i, h: (0, 0)),
                  tab, tab, col(DK), col(DK), col(DV), col(DV)],
        out_specs=[out(DK), out(DK), out(DV), out(DV)],
        out_shape=[jax.ShapeDtypeStruct((M, H * DK), F32), jax.ShapeDtypeStruct((M, H * DK), F32),
                   jax.ShapeDtypeStruct((M, H * DV), F32), jax.ShapeDtypeStruct((M, H * DV), F32)],
        scratch_shapes=[pltpu.VMEM((tm, D), BF16)],
        compiler_params=pltpu.CompilerParams(
            dimension_semantics=("parallel", "arbitrary"),
            vmem_limit_bytes=_vmem_limit(blocks, _nbytes((tm, D), BF16), 4 * _nbytes((tm, DV), F32))),
        name="ret_pre",
    )(x, ln_g.reshape(1, D), cos, sin, p["wq"], p["wk"], p["wv"], p["wg"])


def _ret_log_gamma():
    return jnp.log1p(-jnp.exp2(-5.0 - jnp.arange(RET_HEADS, dtype=F32)))


def _ret_mix_body(lg_ref, x_ref, lng_ref, cos_ref, sin_ref, wq_ref, wk_ref, wv_ref, wg_ref, wo_ref,
                  out_ref, s_ref, *, n_sub):
    C, DK, DV, H = RET_CHUNK, RET_DK, RET_DV, RET_HEADS
    half = DK // 2

    @pl.when(pl.program_id(1) == 0)
    def _():
        s_ref[...] = jnp.zeros_like(s_ref)

    x = x_ref[...]
    h = _rms(x, lng_ref[...]).astype(BF16)
    cos = cos_ref[...]
    sin = sin_ref[...]
    ii = lax.broadcasted_iota(jnp.int32, (C, C), 0)
    jj = lax.broadcasted_iota(jnp.int32, (C, C), 1)
    diff = jnp.maximum((ii - jj).astype(F32), 0.0)
    idx = lax.broadcasted_iota(jnp.int32, (C, 1), 0).astype(F32)
    rows = lambda cc: slice(cc * C, (cc + 1) * C)

    def rotate(t):
        t1, t2 = t[:, :half], t[:, half:]
        return jnp.concatenate([t1 * cos - t2 * sin, t1 * sin + t2 * cos], axis=-1)

    def project(hd):
        qs, vs = slice(hd * DK, (hd + 1) * DK), slice(hd * DV, (hd + 1) * DV)
        q = rotate(jnp.dot(h, wq_ref[:, qs], preferred_element_type=F32))
        k = rotate(jnp.dot(h, wk_ref[:, qs], preferred_element_type=F32) * (DK ** -0.5))
        v = jnp.dot(h, wv_ref[:, vs], preferred_element_type=F32).astype(BF16)
        gz = jnp.dot(h, wg_ref[:, vs], preferred_element_type=F32)
        return q, k, v, gz * jax.nn.sigmoid(gz)

    def retain(hd, q, k, v):
        lg = lg_ref[hd]
        dmat = jnp.where(ii >= jj, jnp.exp(lg * diff), 0.0)
        q_dec = jnp.exp(lg * (idx + 1.0))
        k_dec = jnp.exp(lg * (C - 1.0 - idx))
        s_dec = jnp.exp(jnp.full((1, DV), lg * C, F32))
        subs = range(n_sub)
        inner = [_mm_nt(q[rows(cc)], k[rows(cc)]) * dmat for cc in subs]
        kv = [lax.dot_general((k[rows(cc)] * k_dec).astype(BF16), v[rows(cc)], TN_DIMS,
                              preferred_element_type=F32) for cc in subs]
        S = s_ref[0, hd]
        outs = []
        for cc in subs:
            outs.append(jnp.dot(inner[cc].astype(BF16), v[rows(cc)], preferred_element_type=F32)
                        + _mm(q[rows(cc)] * q_dec, S))
            S = S * s_dec + kv[cc]
        s_ref[0, hd] = S
        return jnp.concatenate(outs, axis=0)

    acc = x
    proj = project(0)
    for hd in range(H):
        nxt = project(hd + 1) if hd + 1 < H else None
        q, k, v, gate = proj
        o = retain(hd, q, k, v)
        acc = acc + _mm(gate * _head_norm(o, RET_GN_EPS), wo_ref[hd * DV:(hd + 1) * DV, :])
        proj = nxt
    out_ref[...] = acc


def _ret_mix(x, ln_g, cos, sin, p, *, n_seq, n_sub):
    M, D = x.shape
    H, DK, DV, C = RET_HEADS, RET_DK, RET_DV, RET_CHUNK
    T = C * n_sub
    nl = M // n_seq // T
    const = lambda b, l: (0, 0)
    tok = pl.BlockSpec((T, D), lambda b, l: (b * nl + l, 0))
    tab = pl.BlockSpec((T, DK // 2), lambda b, l: (l, 0))
    st = pl.BlockSpec((1, H, DK, DV), lambda b, l: (b, 0, 0, 0))
    weights = [p["wq"], p["wk"], p["wv"], p["wg"], p["wo"]]
    streamed = 2 * _nbytes((T, D), F32) + 2 * _nbytes((T, DK // 2), F32) + _nbytes((H, DK, DV), F32)
    resident = sum(_nbytes(w.shape, BF16) for w in weights)
    temps = 8 * _nbytes((T, DV), F32) + 4 * _nbytes((DK, DV), F32)
    return pl.pallas_call(
        functools.partial(_ret_mix_body, n_sub=n_sub), grid=(n_seq, nl),
        in_specs=[pl.BlockSpec(memory_space=pltpu.SMEM), tok, _resident((1, D), const), tab, tab]
                 + [_resident(w.shape, const) for w in weights],
        out_specs=[tok, st],
        out_shape=[jax.ShapeDtypeStruct((M, D), F32), jax.ShapeDtypeStruct((n_seq, H, DK, DV), F32)],
        compiler_params=pltpu.CompilerParams(dimension_semantics=("parallel", "arbitrary"),
                                             vmem_limit_bytes=_vmem_limit(streamed, resident, temps)),
        name="ret_mix",
    )(_ret_log_gamma(), x, ln_g.reshape(1, D), cos, sin, *weights)


def _ret_step_body(lg_ref, q_ref, k_ref, v_ref, s_ref, o_ref, so_ref):
    lg = lg_ref[pl.program_id(1)]
    q = q_ref[0]
    k = k_ref[0]
    v = v_ref[0]
    bt = q.shape[0]
    gam_k = jnp.exp(jnp.full((1, RET_DK), lg, F32))
    gam_v = jnp.exp(jnp.full((1, RET_DV), lg, F32))
    qg = q * gam_k
    vb = v.astype(BF16)
    rowid = lax.broadcasted_iota(jnp.int32, (SUBLANES, RET_DK), 0)
    inner = jnp.sum(q * k, axis=-1, keepdims=True)
    for grp in range(bt // SUBLANES):
        rs = slice(grp * SUBLANES, (grp + 1) * SUBLANES)
        o8 = inner[rs] * v[rs]
        for j in range(SUBLANES):
            b = grp * SUBLANES + j
            m = rowid == j
            S = s_ref[b, 0]
            o8 = o8 + _mm(jnp.where(m, qg[rs], 0.0), S)
            so_ref[b, 0] = S * gam_v + lax.dot_general(jnp.where(m, k[rs], 0.0).astype(BF16), vb[rs],
                                                       TN_DIMS, preferred_element_type=F32)
        o_ref[0, rs, :] = o8


def _ret_step(q, k, v, state, *, bt):
    B = q.shape[0]
    H, DK, DV = RET_HEADS, RET_DK, RET_DV
    heads = lambda t, w: t.reshape(B, H, w).transpose(1, 0, 2)
    tok = lambda w: pl.BlockSpec((1, bt, w), lambda i, h: (h, i, 0))
    st = pl.BlockSpec((bt, 1, DK, DV), lambda i, h: (i, h, 0, 0))
    blocks = 2 * _nbytes((bt, DK, DV), F32) + 4 * _nbytes((bt, DV), F32)
    o, s_new = pl.pallas_call(
        _ret_step_body, grid=(B // bt, H),
        in_specs=[pl.BlockSpec(memory_space=pltpu.SMEM), tok(DK), tok(DK), tok(DV), st],
        out_specs=[tok(DV), st],
        out_shape=[jax.ShapeDtypeStruct((H, B, DV), F32), jax.ShapeDtypeStruct((B, H, DK, DV), F32)],
        compiler_params=pltpu.CompilerParams(dimension_semantics=("parallel", "parallel"),
                                             vmem_limit_bytes=_vmem_limit(blocks, 0, 4 * _nbytes((DK, DV), F32))),
        name="ret_step",
    )(_ret_log_gamma(), heads(q, DK), heads(k, DK), heads(v, DV), state)
    return o.transpose(1, 0, 2).reshape(B, H * DV), s_new


def _ret_post_body(x_ref, o_ref, gate_ref, wo_ref, out_ref):
    acc = x_ref[...]
    for hd in range(RET_HEADS):
        sl = slice(hd * RET_DV, (hd + 1) * RET_DV)
        acc = acc + _mm(gate_ref[:, sl] * _head_norm(o_ref[:, sl], RET_GN_EPS), wo_ref[sl, :])
    out_ref[...] = acc


def _ret_post(x, o, gate, wo, *, tm):
    M, D = x.shape
    W = o.shape[1]
    tok = lambda w: pl.BlockSpec((tm, w), lambda i: (i, 0))
    streamed = 2 * _nbytes((tm, D), F32) + 2 * _nbytes((tm, W), F32)
    return pl.pallas_call(
        _ret_post_body, grid=(M // tm,),
        in_specs=[tok(D), tok(W), tok(W), _resident(wo.shape, lambda i: (0, 0))],
        out_specs=tok(D), out_shape=jax.ShapeDtypeStruct((M, D), F32),
        compiler_params=pltpu.CompilerParams(
            dimension_semantics=("parallel",),
            vmem_limit_bytes=_vmem_limit(streamed, _nbytes(wo.shape, BF16), 3 * _nbytes((tm, W), F32))),
        name="ret_post",
    )(x, o, gate, wo)


def _rope_tables(pos):
    half = RET_DK // 2
    inv = ROPE_BASE ** (-jnp.arange(half, dtype=F32) / half)
    ang = pos.astype(F32)[:, None] * inv[None, :]
    return jnp.cos(ang), jnp.sin(ang)


def _trunk(x, n_seq, pos, shift0, wkv0, ret0, w, tiles):
    M, D = x.shape
    L = M // n_seq
    cos, sin = _rope_tables(pos)
    if L == 1:
        cos, sin = jnp.tile(cos, (M, 1)), jnp.tile(sin, (M, 1))
    new_shift, new_wkv, new_ret = [], [], []
    for i in range(DEPTH):
        x = _ffn(x, w["ln_ffn1"][i], w["ff1_wg"], w["ff1_wu"], w["ff1_wd"],
                 layer=i, tm=tiles["ffn_m"], tf=tiles["ffn_f"])
        j = i // N_MIXERS
        if i % N_MIXERS == 0:
            p = {n: w["rw_" + n][j] for n in ("mu", "w0", "w1", "w2", "a0", "a1", "a2", "g1", "g2", "kk", "ka",
                                              "rk", "wr", "wk", "wv", "wo", "lnx_g", "lnx_b")}
            p["ln"] = w["ln_mix"][i]
            prev = jnp.zeros((n_seq, D), F32) if shift0 is None else shift0[j]
            if L > 1:
                x, sh, st = _rwkv_mix(x, prev, p, n_seq=n_seq, n_s=tiles["wkv_seqs"], T=tiles["wkv_tokens"])
            else:
                r, k, v, lw, a, g, sh = _rwkv_pre(x, prev, p, tile=tiles["rwkv_pre"])
                y, st = _wkv_step(r, k, v, lw, a, p, wkv0[j])
                x = _rwkv_post(x, y, g, p["wo"], tm=tiles["post"])
            new_shift.append(sh)
            new_wkv.append(st)
        else:
            p = {n: w["ret_" + n][j] for n in ("wq", "wk", "wv", "wg", "wo")}
            if L > 1:
                x, st = _ret_mix(x, w["ln_mix"][i], cos, sin, p, n_seq=n_seq, n_sub=tiles["ret_sub"])
            else:
                q, k, v, gate = _ret_pre(x, w["ln_mix"][i], cos, sin, p, tm=tiles["ret_pre"])
                o, st = _ret_step(q, k, v, ret0[j], bt=tiles["ret_step"])
                x = _ret_post(x, o, gate, p["wo"], tm=tiles["post"])
            new_ret.append(st)
        x = _ffn(x, w["ln_ffn2"][i], w["ff2_wg"], w["ff2_wu"], w["ff2_wd"],
                 w["ln_final"] if i == DEPTH - 1 else None, layer=i, tm=tiles["ffn_m"], tf=tiles["ffn_f"])
    return x, jnp.stack(new_shift), jnp.stack(new_wkv), jnp.stack(new_ret)


PROMPT_TILES = dict(ffn_m=1024, ffn_f=256, ret_sub=4, wkv_seqs=2, wkv_tokens=256)
SAMPLE_TILES = dict(ffn_m=128, ffn_f=256, rwkv_pre=128, post=128, ret_pre=128, ret_step=8)

FFN_WEIGHTS = ("ff1_wg", "ff1_wu", "ff1_wd", "ff2_wg", "ff2_wu", "ff2_wd")
MIXER_WEIGHTS = ("rw_w1", "rw_w2", "rw_a1", "rw_a2", "rw_g1", "rw_g2", "rw_wr", "rw_wk", "rw_wv", "rw_wo",
                 "ret_wq", "ret_wk", "ret_wv", "ret_wg", "ret_wo")


def kernel(x_prompt, x_sample, state_rwkv_shift, state_rwkv_wkv, state_ret, ln_ffn1, ff1_wg, ff1_wu, ff1_wd, ln_mix, ln_ffn2, ff2_wg, ff2_wu, ff2_wd, ln_final, rw_mu, rw_w0, rw_w1, rw_w2, rw_a0, rw_a1, rw_a2, rw_g1, rw_g2, rw_kk, rw_ka, rw_rk, rw_wr, rw_wk, rw_wv, rw_wo, rw_lnx_g, rw_lnx_b, ret_wq, ret_wk, ret_wv, ret_wg, ret_wo):
    w = dict(ln_ffn1=ln_ffn1, ff1_wg=ff1_wg, ff1_wu=ff1_wu, ff1_wd=ff1_wd, ln_mix=ln_mix,
             ln_ffn2=ln_ffn2, ff2_wg=ff2_wg, ff2_wu=ff2_wu, ff2_wd=ff2_wd, ln_final=ln_final,
             rw_mu=rw_mu, rw_w0=rw_w0, rw_w1=rw_w1, rw_w2=rw_w2, rw_a0=rw_a0, rw_a1=rw_a1,
             rw_a2=rw_a2, rw_g1=rw_g1, rw_g2=rw_g2, rw_kk=rw_kk, rw_ka=rw_ka, rw_rk=rw_rk,
             rw_wr=rw_wr, rw_wk=rw_wk, rw_wv=rw_wv, rw_wo=rw_wo, rw_lnx_g=rw_lnx_g,
             rw_lnx_b=rw_lnx_b, ret_wq=ret_wq, ret_wk=ret_wk, ret_wv=ret_wv, ret_wg=ret_wg,
             ret_wo=ret_wo)
    for name in FFN_WEIGHTS:
        w[name] = w[name].astype(BF16)
    for name in MIXER_WEIGHTS:
        w[name] = [w[name][i].astype(BF16) for i in range(w[name].shape[0])]
    B, L, D = x_prompt.shape
    Bs, Ls, _ = x_sample.shape
    y_p, p_shift, p_wkv, p_ret = _trunk(x_prompt.reshape(B * L, D), B, jnp.arange(L, dtype=F32),
                                        None, None, None, w, PROMPT_TILES)
    pos_s = PAST_LEN + jnp.arange(Ls, dtype=F32)
    y_s, s_shift, s_wkv, s_ret = _trunk(x_sample.reshape(Bs * Ls, D), Bs, pos_s,
                                        state_rwkv_shift, state_rwkv_wkv, state_ret, w, SAMPLE_TILES)
    return (y_p.reshape(B, L, D), y_s.reshape(Bs, Ls, D), p_shift, p_wkv, p_ret, s_shift, s_wkv, s_ret)
```

```python
import functools
import math

import numpy as np
import jax
import jax.numpy as jnp
from jax import lax
from jax.experimental import pallas as pl
from jax.experimental.pallas import tpu as pltpu

F32 = jnp.float32
BF16 = jnp.bfloat16

DEPTH = 2
N_MIXERS = 2
RW_HEADS = 16
RW_HEAD = 64
RW_GN_EPS = 64e-5
RET_HEADS = 4
RET_DK = 256
RET_DV = 512
RET_CHUNK = 128
RET_GN_EPS = 1e-6
ROPE_BASE = 10000.0
FFN_RES = 0.5
RMS_EPS = 1e-6
PAST_LEN = 16384

WKV_CHUNK = 64
V7X_VMEM_CAP_BYTES = 60000 * 1024
SUBLANES = 8

NT_DIMS = (((1,), (1,)), ((), ()))
TN_DIMS = (((0,), (0,)), ((), ()))


def _vmem_limit(block_bytes, scratch_bytes=0, temp_bytes=0):
    return int(min(V7X_VMEM_CAP_BYTES, 2 * block_bytes + scratch_bytes + temp_bytes + (4 << 20)))


def _nbytes(shape, dtype):
    return int(np.prod(shape)) * jnp.dtype(dtype).itemsize


def _resident(shape, index_map):
    return pl.BlockSpec(shape, index_map, pipeline_mode=pl.Buffered(1))


def _rms(x, g):
    return x * lax.rsqrt(jnp.mean(x * x, axis=-1, keepdims=True) + RMS_EPS) * g


def _head_norm(y, eps):
    mu = jnp.mean(y, axis=-1, keepdims=True)
    yc = y - mu
    return yc * lax.rsqrt(jnp.mean(yc * yc, axis=-1, keepdims=True) + eps)


def _mm(a, b):
    return jnp.dot(a.astype(BF16), b.astype(BF16), preferred_element_type=F32)


def _mm_nt(a, b):
    return lax.dot_general(a.astype(BF16), b.astype(BF16), NT_DIMS, preferred_element_type=F32)


def _mm_tn(a, b):
    return lax.dot_general(a.astype(BF16), b.astype(BF16), TN_DIMS, preferred_element_type=F32)


def _ffn_body(*refs, ff_tile, final, side):
    refs = list(refs)
    x_ref, g_ref, wg_ref, wu_ref, wd_ref = refs[:5]
    del refs[:5]
    gf_ref = refs.pop(0) if final else None
    if side:
        side_in = refs[:5]
        del refs[:6 if side == "chained" else 5]
    o_ref = refs.pop(0)
    x = x_ref[...]
    h = _rms(x, g_ref[...]).astype(BF16)
    n_tiles = wg_ref.shape[1] // ff_tile

    def activation(j):
        sl = slice(j * ff_tile, (j + 1) * ff_tile)
        gate = jnp.dot(h, wg_ref[:, sl], preferred_element_type=F32)
        up = jnp.dot(h, wu_ref[:, sl], preferred_element_type=F32)
        return (gate * jax.nn.sigmoid(gate) * up).astype(BF16)

    act = activation(0)
    acc = None
    for j in range(n_tiles):
        nxt = activation(j + 1) if j + 1 < n_tiles else None
        part = jnp.dot(act, wd_ref[j * ff_tile:(j + 1) * ff_tile, :], preferred_element_type=F32)
        acc = part if acc is None else acc + part
        act = nxt
    y = x + FFN_RES * acc
    if final:
        y = _rms(y, gf_ref[...])
    o_ref[...] = y
    if side:
        _ret_side(*side_in, *refs)


def _ffn(x, ln_g, wg, wu, wd, ln_final=None, *, layer, tm, tf, side=None):
    M, D = x.shape
    FF = wg.shape[2]
    final = ln_final is not None
    steps = M // tm
    row = lambda i: (i, 0)
    const = lambda i: (0, 0)
    of_layer = lambda i: (layer, 0, 0)
    in_specs = [pl.BlockSpec((tm, D), row), _resident((1, D), const), _resident((None, D, FF), of_layer),
                _resident((None, D, FF), of_layer), _resident((None, FF, D), of_layer)]
    args = [x, ln_g.reshape(1, D), wg, wu, wd]
    if final:
        in_specs.append(_resident((1, D), const))
        args.append(ln_final.reshape(1, D))
    out_specs = [pl.BlockSpec((tm, D), row)]
    out_shape = [jax.ShapeDtypeStruct((M, D), F32)]
    streamed = 2 * _nbytes((tm, D), F32)
    aliases = {}
    mode = None
    if side is not None:
        B, H, DK, DV = side["state"].shape
        bt = side["seqs"]
        first = side["first"] // bt
        taken = lambda i: (first + i, 0, 0)
        st = pl.BlockSpec((bt, H, DK, DV), lambda i: (first + i, 0, 0, 0))
        in_specs += [pl.BlockSpec(memory_space=pltpu.SMEM), pl.BlockSpec((bt, DK, H), taken),
                     pl.BlockSpec((bt, DK, H), taken), pl.BlockSpec((bt, H, DV), taken), st]
        args += [_ret_log_gamma(), side["q"], side["k"], side["v"], side["state"]]
        mode = "first"
        if side["buf"] is not None:
            mode = "chained"
            aliases = {len(args): 2}
            in_specs.append(pl.BlockSpec(memory_space=pl.ANY))
            args.append(side["buf"])
        out_specs += [pl.BlockSpec((bt, H, DV), lambda i: (i, 0, 0)), st]
        out_shape += [jax.ShapeDtypeStruct((steps * bt, H, DV), F32), jax.ShapeDtypeStruct((B, H, DK, DV), F32)]
        streamed += 2 * _nbytes((bt, H, DK, DV), F32) + 2 * _nbytes((bt, DK, 128), F32)
    resident = 3 * _nbytes((D, FF), BF16)
    temps = 6 * _nbytes((tm, tf), F32) + 3 * _nbytes((tm, D), F32)
    out = pl.pallas_call(
        functools.partial(_ffn_body, ff_tile=tf, final=final, side=mode),
        grid=(steps,), in_specs=in_specs, out_specs=out_specs, out_shape=out_shape,
        input_output_aliases=aliases,
        compiler_params=pltpu.CompilerParams(dimension_semantics=("parallel",),
                                             vmem_limit_bytes=_vmem_limit(streamed, resident, temps)),
        name="ffn_final" if final else "ffn",
    )(*args)
    return out if side is not None else out[0]


def _rwkv_project(h, xprev, mu_ref, w0_ref, a0_ref, wr_ref, wk_ref, wv_ref, w1_ref, w2_ref, a1_ref, a2_ref,
                  g1_ref, g2_ref):
    xx = xprev - h
    xr, xw, xk, xv, xa, xg = (h + xx * mu_ref[i:i + 1, :] for i in range(6))
    r = _mm(xr, wr_ref[...])
    k = _mm(xk, wk_ref[...])
    v = _mm(xv, wv_ref[...])
    z = w0_ref[...] + _mm(jnp.tanh(_mm(xw, w1_ref[...])), w2_ref[...])
    lw = -math.exp(-0.5) * jax.nn.sigmoid(z)
    a = jax.nn.sigmoid(a0_ref[...] + _mm(_mm(xa, a1_ref[...]), a2_ref[...]))
    g = _mm(jax.nn.sigmoid(_mm(xg, g1_ref[...])), g2_ref[...])
    return r, k, v, lw, a, g


def _rwkv_pre_body(x_ref, lng_ref, prev_ref, *refs):
    weight_refs, out_refs = refs[:12], refs[12:]
    h = _rms(x_ref[...], lng_ref[...])
    for ref, val in zip(out_refs, _rwkv_project(h, prev_ref[...], *weight_refs) + (h,)):
        ref[...] = val


def _rwkv_pre(x, prev, p, *, tile):
    M, D = x.shape
    vec = lambda a: a.reshape(1, D)
    row = lambda i: (i, 0)
    tok = pl.BlockSpec((tile, D), row)
    full = lambda a: _resident(a.shape, lambda i: (0, 0))
    params = [p["mu"], vec(p["w0"]), vec(p["a0"])]
    weights = [p["wr"], p["wk"], p["wv"], p["w1"], p["w2"], p["a1"], p["a2"], p["g1"], p["g2"]]
    streamed = 9 * _nbytes((tile, D), F32)
    resident = sum(_nbytes(w.shape, BF16) for w in weights)
    return pl.pallas_call(
        _rwkv_pre_body, grid=(M // tile,),
        in_specs=[tok, full(vec(p["ln"])), tok] + [full(t) for t in params + weights],
        out_specs=[tok] * 7,
        out_shape=[jax.ShapeDtypeStruct((M, D), F32)] * 7,
        compiler_params=pltpu.CompilerParams(
            dimension_semantics=("parallel",),
            vmem_limit_bytes=_vmem_limit(streamed, resident, 10 * _nbytes((tile, D), F32))),
        name="rwkv_pre",
    )(x, vec(p["ln"]), prev, *params, *weights)


def _wkv_chunk(r, k, v, lw, a, kkw, kaw, rkw, lnx_g, lnx_b, S, n_s):
    C = WKV_CHUNK
    N = RW_HEAD
    W = 2 * N
    T = n_s * C
    ii = lax.broadcasted_iota(jnp.int32, (T, T), 0)
    jj = lax.broadcasted_iota(jnp.int32, (T, T), 1)
    shift = C.bit_length() - 1
    tri = (ii >= jj) & (jnp.right_shift(ii, shift) == jnp.right_shift(jj, shift))
    lw_hi = lw.astype(BF16)
    rest = lw - lw_hi.astype(F32)
    lw_mid = rest.astype(BF16)
    lw_lo = (rest - lw_mid.astype(F32)).astype(BF16)
    tri = tri.astype(BF16)
    cl = (jnp.dot(tri, lw_hi, preferred_element_type=F32) + jnp.dot(tri, lw_mid, preferred_element_type=F32)
          + jnp.dot(tri, lw_lo, preferred_element_type=F32))
    gls = [cl[s * C + C - 1:s * C + C, :] for s in range(n_s)]
    gl = jnp.concatenate([jnp.broadcast_to(g, (C, g.shape[-1])) for g in gls], axis=0)
    e_cl = jnp.exp(cl)
    e_ncl = jnp.exp(-cl)
    e_ce = jnp.exp(cl - lw)
    e_g = jnp.exp(gl - cl)
    dgs = [jnp.exp(g) for g in gls]
    kkraw = k * kkw
    kmod = k * (1.0 + (a - 1.0) * kaw)
    rkk = r * kmod * rkw
    rt = r * e_cl
    kt = kmod * e_ncl
    kg = kmod * e_g
    row = lax.broadcasted_iota(jnp.int32, (C, W), 0)
    lane = lax.broadcasted_iota(jnp.int32, (C, W), 1)
    left = lane < N
    tok = jnp.where(left, lane, lane - N)
    strict = row > tok
    incl = row >= tok
    same_head = ((lax.broadcasted_iota(jnp.int32, (W, W), 0) < N)
                 == (lax.broadcasted_iota(jnp.int32, (W, W), 1) < N))

    def seg_sum(x):
        s_l = jnp.sum(jnp.where(left, x, 0.0), axis=-1, keepdims=True)
        s_r = jnp.sum(jnp.where(left, 0.0, x), axis=-1, keepdims=True)
        return jnp.where(left, s_l, s_r)

    def bd(x):
        xb = x.astype(BF16)
        zero = jnp.zeros_like(xb)
        return jnp.concatenate([jnp.where(left, xb, zero), jnp.where(left, zero, xb)], axis=0)

    units = list(S)
    at = {u: (slice(u[0] * C, (u[0] + 1) * C), slice(u[1] * W, (u[1] + 1) * W)) for u in units}
    V2 = {u: v[at[u]] for u in units}
    b2, lhs2, rhs_nt = {}, {}, {}
    for u in units:
        kk2 = kkraw[at[u]]
        kkn = kk2 / jnp.maximum(jnp.sqrt(seg_sum(kk2 * kk2)), 1e-12)
        b2[u] = kkn * a[at[u]]
        lhs2[u] = jnp.concatenate([-kkn * e_ce[at[u]], rt[at[u]]], axis=0).astype(BF16)
        rhs_nt[u] = jnp.concatenate([bd(b2[u] * e_ncl[at[u]]), bd(kt[at[u]])], axis=0)
    P = {u: lax.dot_general(lhs2[u], rhs_nt[u], NT_DIMS, preferred_element_type=F32) for u in units}
    LS = {u: lax.dot_general(lhs2[u], S[u].astype(BF16), NT_DIMS, preferred_element_type=F32) for u in units}
    bd_v = {u: bd(V2[u]) for u in units}
    X = {u: LS[u][:C] + jnp.dot(jnp.where(strict, P[u][:C, W:], 0.0).astype(BF16), bd_v[u],
                                preferred_element_type=F32) for u in units}
    Pk = {u: jnp.where(strict, P[u][:C, :W], 0.0) for u in units}
    span = 1
    while span < C:
        span *= 2
        if span < C:
            Z = {u: jnp.dot(Pk[u].astype(BF16), jnp.concatenate([bd(X[u]), bd(Pk[u])], axis=1),
                            preferred_element_type=F32) for u in units}
            X = {u: X[u] + Z[u][:, :W] for u in units}
            Pk = {u: Z[u][:, W:] for u in units}
        else:
            X = {u: X[u] + jnp.dot(Pk[u].astype(BF16), bd(X[u]), preferred_element_type=F32) for u in units}
    incl2 = jnp.concatenate([incl, incl], axis=1)
    Y = {u: LS[u][C:] + jnp.dot(jnp.where(incl2, P[u][C:], 0.0).astype(BF16),
                                jnp.concatenate([bd(X[u]), bd_v[u]], axis=0), preferred_element_type=F32)
         for u in units}
    s_new = {u: S[u] * dgs[u[0]][:, at[u][1]]
             + jnp.where(same_head,
                         _mm_tn(jnp.concatenate([X[u], V2[u]], axis=0),
                                jnp.concatenate([b2[u] * e_g[at[u]], kg[at[u]]], axis=0)), 0.0)
             for u in units}
    y = {}
    for u in units:
        sl = at[u][1]
        yc = Y[u] - seg_sum(Y[u]) * (1.0 / N)
        yn = yc * lax.rsqrt(seg_sum(yc * yc) * (1.0 / N) + RW_GN_EPS)
        y[u] = yn * lnx_g[:, sl] + lnx_b[:, sl] + seg_sum(rkk[at[u]]) * V2[u]
    return y, s_new


def _rwkv_mix_body(x_ref, lng_ref, prev_ref, mu_ref, w0_ref, a0_ref, kkw_ref, kaw_ref, rk_ref, lg_ref, lb_ref,
                   wr_ref, wk_ref, wv_ref, wo_ref, w1_ref, w2_ref, a1_ref, a2_ref, g1_ref, g2_ref,
                   out_ref, so_ref, s_ref,
                   carry, sbd_scr, r_scr, k_scr, v_scr, lw_scr, a_scr, y_scr, *, n_steps):
    C, N, W = WKV_CHUNK, RW_HEAD, 2 * RW_HEAD
    n_s, T, D = x_ref.shape
    step = pl.program_id(1)

    @pl.when(step == 0)
    def _():
        carry[...] = prev_ref[...]
        sbd_scr[...] = jnp.zeros_like(sbd_scr)

    x = x_ref[...].reshape(n_s * T, D)
    h = _rms(x, lng_ref[...])
    row = lax.broadcasted_iota(jnp.int32, h.shape, 0)
    xprev = pltpu.roll(h, 1, 0)
    for s in range(n_s):
        xprev = jnp.where(row == s * T, carry[s], xprev)
        last = h[s * T + T - 1:s * T + T, :]
        carry[s] = last
        so_ref[s] = last
    r_scr[...], k_scr[...], v_scr[...], lw_scr[...], a_scr[...], g = _rwkv_project(
        h, xprev, mu_ref, w0_ref, a0_ref, wr_ref, wk_ref, wv_ref, w1_ref, w2_ref, a1_ref, a2_ref, g1_ref, g2_ref)
    kkw, kaw, rkw, lnx_g, lnx_b = kkw_ref[...], kaw_ref[...], rk_ref[...], lg_ref[...], lb_ref[...]
    units = [(s, p) for s in range(n_s) for p in range(RW_HEADS // 2)]

    def chunk(c, loop_carry):
        starts = [pl.multiple_of(s * T + c * C, C) for s in range(n_s)]
        rows_of = lambda ref: jnp.concatenate([ref[pl.ds(st, C), :] for st in starts], axis=0)
        S = {u: sbd_scr[u[0], u[1]] for u in units}
        y, s_new = _wkv_chunk(rows_of(r_scr), rows_of(k_scr), rows_of(v_scr), rows_of(lw_scr), rows_of(a_scr),
                              kkw, kaw, rkw, lnx_g, lnx_b, S, n_s)
        for u in units:
            y_scr[pl.ds(starts[u[0]], C), u[1] * W:(u[1] + 1) * W] = y[u]
            sbd_scr[u[0], u[1]] = s_new[u]
        return loop_carry

    lax.fori_loop(0, T // C, chunk, 0)
    out_ref[...] = (x + _mm(y_scr[...] * g, wo_ref[...])).reshape(n_s, T, D)

    @pl.when(step == n_steps - 1)
    def _():
        for s, p in units:
            S = sbd_scr[s, p]
            s_ref[s, 2 * p] = S[:N, :N]
            s_ref[s, 2 * p + 1] = S[N:, N:]


def _rwkv_mix(x, prev, p, *, n_seq, n_s, T):
    M, D = x.shape
    L = M // n_seq
    nl = L // T
    H, N = RW_HEADS, RW_HEAD
    vec = lambda t: t.reshape(1, D)
    const = lambda b, l: (0, 0)
    tok = pl.BlockSpec((n_s, T, D), lambda b, l: (b, l, 0))
    row1 = pl.BlockSpec((n_s, 1, D), lambda b, l: (b, 0, 0))
    st = pl.BlockSpec((n_s, H, N, N), lambda b, l: (b, 0, 0, 0))
    params = [vec(p["ln"]), p["mu"], vec(p["w0"]), vec(p["a0"]), vec(p["kk"]), vec(p["ka"]), vec(p["rk"]),
              vec(p["lnx_g"]), vec(p["lnx_b"])]
    weights = [p["wr"], p["wk"], p["wv"], p["wo"], p["w1"], p["w2"], p["a1"], p["a2"], p["g1"], p["g2"]]
    full = lambda t: _resident(t.shape, const)
    in_specs = [tok, full(params[0]), row1] + [full(t) for t in params[1:]] + [full(t) for t in weights]
    rows = (n_s * T, D)
    pair_state = (n_s, H // 2, 2 * N, 2 * N)
    scratch = [pltpu.VMEM((n_s, 1, D), F32), pltpu.VMEM(pair_state, F32)] + [pltpu.VMEM(rows, F32)] * 6
    streamed = 2 * _nbytes((n_s, T, D), F32) + _nbytes((n_s, H, N, 128), F32)
    resident = sum(_nbytes(t.shape, BF16) for t in weights) + _nbytes(pair_state, F32) + 6 * _nbytes(rows, F32)
    out, so, state = pl.pallas_call(
        functools.partial(_rwkv_mix_body, n_steps=nl), grid=(n_seq // n_s, nl),
        in_specs=in_specs, out_specs=[tok, row1, st],
        out_shape=[jax.ShapeDtypeStruct((n_seq, L, D), F32), jax.ShapeDtypeStruct((n_seq, 1, D), F32),
                   jax.ShapeDtypeStruct((n_seq, H, N, N), F32)],
        scratch_shapes=scratch,
        compiler_params=pltpu.CompilerParams(dimension_semantics=("parallel", "arbitrary"),
                                             vmem_limit_bytes=_vmem_limit(streamed, resident, 12 * _nbytes(rows, F32))),
        name="rwkv_mix",
    )(x.reshape(n_seq, L, D), params[0], prev.reshape(n_seq, 1, D), *params[1:], *weights)
    return out.reshape(M, D), so.reshape(n_seq, D), state


def _wkv_step_body(r_ref, k_ref, v_ref, lw_ref, a_ref, kkw_ref, kaw_ref, rk_ref, lg_ref, lb_ref, s_ref,
                   y_ref, so_ref, y_scr):
    r = r_ref[0]
    k = k_ref[0]
    v = v_ref[0]
    a = a_ref[0]
    d = jnp.exp(lw_ref[0])
    kkh = k * kkw_ref[0]
    nrm = jnp.sqrt(jnp.sum(kkh * kkh, axis=0, keepdims=True))
    kkn = kkh / jnp.maximum(nrm, 1e-12)
    kmod = k * (1.0 + (a - 1.0) * kaw_ref[0])
    bh = kkn * a

    def value_channel(i, carry):
        S = s_ref[0, i]
        sa = -jnp.sum(S * kkn, axis=0, keepdims=True)
        s_new = S * d + sa * bh + v_ref[0, pl.ds(i, 1), :] * kmod
        so_ref[0, i] = s_new
        y_scr[pl.ds(i, 1), :] = jnp.sum(s_new * r, axis=0, keepdims=True)
        return carry

    lax.fori_loop(0, RW_HEAD, value_channel, 0, unroll=8)
    y = y_scr[...]
    mu = jnp.mean(y, axis=0, keepdims=True)
    yc = y - mu
    yn = yc * lax.rsqrt(jnp.mean(yc * yc, axis=0, keepdims=True) + RW_GN_EPS)
    y_ref[0] = yn * lg_ref[0] + lb_ref[0] + jnp.sum(r * kmod * rk_ref[0], axis=0, keepdims=True) * v


def _wkv_step(r, k, v, lw, a, p, state):
    B, D = r.shape
    H, N = RW_HEADS, RW_HEAD
    lanes = lambda t: t.T.reshape(H, N, B)
    par = lambda t: jnp.broadcast_to(t.reshape(H, N, 1), (H, N, B))
    vec = pl.BlockSpec((1, N, B), lambda h: (h, 0, 0))
    st = pl.BlockSpec((1, N, N, B), lambda h: (h, 0, 0, 0))
    blocks = 2 * _nbytes((N, N, B), F32) + 11 * _nbytes((N, B), F32)
    y, s_new = pl.pallas_call(
        _wkv_step_body, grid=(H,),
        in_specs=[vec] * 10 + [st],
        out_specs=[vec, st],
        out_shape=[jax.ShapeDtypeStruct((H, N, B), F32), jax.ShapeDtypeStruct((H, N, N, B), F32)],
        scratch_shapes=[pltpu.VMEM((N, B), F32)],
        compiler_params=pltpu.CompilerParams(dimension_semantics=("parallel",),
                                             vmem_limit_bytes=_vmem_limit(blocks, _nbytes((N, B), F32), 4 << 20)),
        name="wkv_step",
    )(lanes(r), lanes(k), lanes(v), lanes(lw), lanes(a),
      par(p["kk"]), par(p["ka"]), par(p["rk"]), par(p["lnx_g"]), par(p["lnx_b"]),
      jnp.transpose(state, (1, 2, 3, 0)))
    return y.reshape(D, B).T, jnp.transpose(s_new, (3, 0, 1, 2))


def _rwkv_post_body(x_ref, y_ref, g_ref, wo_ref, o_ref):
    o_ref[...] = x_ref[...] + _mm(y_ref[...] * g_ref[...], wo_ref[...])


def _rwkv_post(x, y, g, wo, *, tm):
    M, D = x.shape
    tok = pl.BlockSpec((tm, D), lambda i: (i, 0))
    streamed = 4 * _nbytes((tm, D), F32)
    return pl.pallas_call(
        _rwkv_post_body, grid=(M // tm,),
        in_specs=[tok, tok, tok, _resident(wo.shape, lambda i: (0, 0))],
        out_specs=tok, out_shape=jax.ShapeDtypeStruct((M, D), F32),
        compiler_params=pltpu.CompilerParams(
            dimension_semantics=("parallel",),
            vmem_limit_bytes=_vmem_limit(streamed, _nbytes(wo.shape, BF16), 3 * _nbytes((tm, D), F32))),
        name="rwkv_post",
    )(x, y, g, wo)


def _ret_pre_body(x_ref, lng_ref, cos_ref, sin_ref, wq_ref, wk_ref, wv_ref, wg_ref,
                  q_ref, k_ref, v_ref, gate_ref, h_scr):
    @pl.when(pl.program_id(1) == 0)
    def _():
        h_scr[...] = _rms(x_ref[...], lng_ref[...]).astype(BF16)

    h = h_scr[...]
    cos = cos_ref[...]
    sin = sin_ref[...]
    half = RET_DK // 2

    def rotate(t, out_ref):
        t1, t2 = t[:, :half], t[:, half:]
        out_ref[:, :half] = t1 * cos - t2 * sin
        out_ref[:, half:] = t1 * sin + t2 * cos

    rotate(jnp.dot(h, wq_ref[...], preferred_element_type=F32), q_ref)
    rotate(jnp.dot(h, wk_ref[...], preferred_element_type=F32) * (RET_DK ** -0.5), k_ref)
    v_ref[...] = jnp.dot(h, wv_ref[...], preferred_element_type=F32)
    gz = jnp.dot(h, wg_ref[...], preferred_element_type=F32)
    gate_ref[...] = gz * jax.nn.sigmoid(gz)


def _ret_pre(x, ln_g, cos, sin, p, *, tm):
    M, D = x.shape
    H, DK, DV = RET_HEADS, RET_DK, RET_DV
    n_tab = cos.shape[0] // tm
    tab = pl.BlockSpec((tm, DK // 2), lambda i, h: (i % n_tab, 0))
    col = lambda w: pl.BlockSpec((D, w), lambda i, h: (0, h))
    out = lambda w: pl.BlockSpec((tm, w), lambda i, h: (i, h))
    blocks = (_nbytes((tm, D), F32) + 2 * _nbytes((tm, DK // 2), F32) + 2 * _nbytes((D, DK), BF16)
              + 2 * _nbytes((D, DV), BF16) + 2 * _nbytes((tm, DK), F32) + 2 * _nbytes((tm, DV), F32))
    return pl.pallas_call(
        _ret_pre_body, grid=(M // tm, H),
        in_specs=[pl.BlockSpec((tm, D), lambda i, h: (i, 0)), pl.BlockSpec((1, D), lambda i, h: (0, 0)),
                  tab, tab, col(DK), col(DK), col(DV), col(DV)],
        out_specs=[out(DK), out(DK), out(DV), out(DV)],
        out_shape=[jax.ShapeDtypeStruct((M, H * DK), F32), jax.ShapeDtypeStruct((M, H * DK), F32),
                   jax.ShapeDtypeStruct((M, H * DV), F32), jax.ShapeDtypeStruct((M, H * DV), F32)],
        scratch_shapes=[pltpu.VMEM((tm, D), BF16)],
        compiler_params=pltpu.CompilerParams(
            dimension_semantics=("parallel", "arbitrary"),
            vmem_limit_bytes=_vmem_limit(blocks, _nbytes((tm, D), BF16), 4 * _nbytes((tm, DV), F32))),
        name="ret_pre",
    )(x, ln_g.reshape(1, D), cos, sin, p["wq"], p["wk"], p["wv"], p["wg"])


def _ret_log_gamma():
    return jnp.log1p(-jnp.exp2(-5.0 - jnp.arange(RET_HEADS, dtype=F32)))


def _ret_mix_body(lg_ref, x_ref, lng_ref, cos_ref, sin_ref, wq_ref, wk_ref, wv_ref, wg_ref, wo_ref,
                  out_ref, s_ref, *, n_sub):
    C, DK, DV, H = RET_CHUNK, RET_DK, RET_DV, RET_HEADS
    half = DK // 2

    @pl.when(pl.program_id(1) == 0)
    def _():
        s_ref[...] = jnp.zeros_like(s_ref)

    x = x_ref[...]
    h = _rms(x, lng_ref[...]).astype(BF16)
    cos = cos_ref[...]
    sin = sin_ref[...]
    ii = lax.broadcasted_iota(jnp.int32, (C, C), 0)
    jj = lax.broadcasted_iota(jnp.int32, (C, C), 1)
    diff = jnp.maximum((ii - jj).astype(F32), 0.0)
    idx = lax.broadcasted_iota(jnp.int32, (C, 1), 0).astype(F32)
    rows = lambda cc: slice(cc * C, (cc + 1) * C)

    def rotate(t):
        t1, t2 = t[:, :half], t[:, half:]
        return jnp.concatenate([t1 * cos - t2 * sin, t1 * sin + t2 * cos], axis=-1)

    def project(hd):
        qs, vs = slice(hd * DK, (hd + 1) * DK), slice(hd * DV, (hd + 1) * DV)
        q = rotate(jnp.dot(h, wq_ref[:, qs], preferred_element_type=F32))
        k = rotate(jnp.dot(h, wk_ref[:, qs], preferred_element_type=F32) * (DK ** -0.5))
        v = jnp.dot(h, wv_ref[:, vs], preferred_element_type=F32).astype(BF16)
        gz = jnp.dot(h, wg_ref[:, vs], preferred_element_type=F32)
        return q, k, v, gz * jax.nn.sigmoid(gz)

    def retain(hd, q, k, v):
        lg = lg_ref[hd]
        dmat = jnp.where(ii >= jj, jnp.exp(lg * diff), 0.0)
        q_dec = jnp.exp(lg * (idx + 1.0))
        k_dec = jnp.exp(lg * (C - 1.0 - idx))
        s_dec = jnp.exp(jnp.full((1, DV), lg * C, F32))
        subs = range(n_sub)
        inner = [_mm_nt(q[rows(cc)], k[rows(cc)]) * dmat for cc in subs]
        kv = [lax.dot_general((k[rows(cc)] * k_dec).astype(BF16), v[rows(cc)], TN_DIMS,
                              preferred_element_type=F32) for cc in subs]
        S = s_ref[0, hd]
        outs = []
        for cc in subs:
            outs.append(jnp.dot(inner[cc].astype(BF16), v[rows(cc)], preferred_element_type=F32)
                        + _mm(q[rows(cc)] * q_dec, S))
            S = S * s_dec + kv[cc]
        s_ref[0, hd] = S
        return jnp.concatenate(outs, axis=0)

    acc = x
    proj = project(0)
    for hd in range(H):
        nxt = project(hd + 1) if hd + 1 < H else None
        q, k, v, gate = proj
        o = retain(hd, q, k, v)
        acc = acc + _mm(gate * _head_norm(o, RET_GN_EPS), wo_ref[hd * DV:(hd + 1) * DV, :])
        proj = nxt
    out_ref[...] = acc


def _ret_mix(x, ln_g, cos, sin, p, *, n_seq, n_sub):
    M, D = x.shape
    H, DK, DV, C = RET_HEADS, RET_DK, RET_DV, RET_CHUNK
    T = C * n_sub
    nl = M // n_seq // T
    const = lambda b, l: (0, 0)
    tok = pl.BlockSpec((T, D), lambda b, l: (b * nl + l, 0))
    tab = pl.BlockSpec((T, DK // 2), lambda b, l: (l, 0))
    st = pl.BlockSpec((1, H, DK, DV), lambda b, l: (b, 0, 0, 0))
    weights = [p["wq"], p["wk"], p["wv"], p["wg"], p["wo"]]
    streamed = 2 * _nbytes((T, D), F32) + 2 * _nbytes((T, DK // 2), F32) + _nbytes((H, DK, DV), F32)
    resident = sum(_nbytes(w.shape, BF16) for w in weights)
    temps = 8 * _nbytes((T, DV), F32) + 4 * _nbytes((DK, DV), F32)
    return pl.pallas_call(
        functools.partial(_ret_mix_body, n_sub=n_sub), grid=(n_seq, nl),
        in_specs=[pl.BlockSpec(memory_space=pltpu.SMEM), tok, _resident((1, D), const), tab, tab]
                 + [_resident(w.shape, const) for w in weights],
        out_specs=[tok, st],
        out_shape=[jax.ShapeDtypeStruct((M, D), F32), jax.ShapeDtypeStruct((n_seq, H, DK, DV), F32)],
        compiler_params=pltpu.CompilerParams(dimension_semantics=("parallel", "arbitrary"),
                                             vmem_limit_bytes=_vmem_limit(streamed, resident, temps)),
        name="ret_mix",
    )(_ret_log_gamma(), x, ln_g.reshape(1, D), cos, sin, *weights)


def _ret_side(lg_ref, qc_ref, kc_ref, v_ref, s_ref, o_ref, so_ref):
    for hd in range(RET_HEADS):
        gam = jnp.exp(jnp.full((1, RET_DV), lg_ref[hd], F32))
        for b in range(s_ref.shape[0]):
            q = qc_ref[b, :, hd:hd + 1]
            k = kc_ref[b, :, hd:hd + 1]
            v = v_ref[b, hd:hd + 1, :]
            S = s_ref[b, hd]
            so_ref[b, hd] = S * gam + k * v
            o_ref[b, hd:hd + 1, :] = (gam * jnp.sum(q * S, axis=0, keepdims=True)
                                      + jnp.sum(q * k, axis=0, keepdims=True) * v)


def _ret_post_body(x_ref, o_ref, gate_ref, wo_ref, out_ref):
    acc = x_ref[...]
    for hd in range(RET_HEADS):
        sl = slice(hd * RET_DV, (hd + 1) * RET_DV)
        acc = acc + _mm(gate_ref[:, sl] * _head_norm(o_ref[:, sl], RET_GN_EPS), wo_ref[sl, :])
    out_ref[...] = acc


def _ret_post(x, o, gate, wo, *, tm):
    M, D = x.shape
    W = o.shape[1]
    tok = lambda w: pl.BlockSpec((tm, w), lambda i: (i, 0))
    streamed = 2 * _nbytes((tm, D), F32) + 2 * _nbytes((tm, W), F32)
    return pl.pallas_call(
        _ret_post_body, grid=(M // tm,),
        in_specs=[tok(D), tok(W), tok(W), _resident(wo.shape, lambda i: (0, 0))],
        out_specs=tok(D), out_shape=jax.ShapeDtypeStruct((M, D), F32),
        compiler_params=pltpu.CompilerParams(
            dimension_semantics=("parallel",),
            vmem_limit_bytes=_vmem_limit(streamed, _nbytes(wo.shape, BF16), 3 * _nbytes((tm, W), F32))),
        name="ret_post",
    )(x, o, gate, wo)


def _rope_tables(pos):
    half = RET_DK // 2
    inv = ROPE_BASE ** (-jnp.arange(half, dtype=F32) / half)
    ang = pos.astype(F32)[:, None] * inv[None, :]
    return jnp.cos(ang), jnp.sin(ang)


def _rwkv_params(w, i):
    j = i // N_MIXERS
    p = {n: w["rw_" + n][j] for n in ("mu", "w0", "w1", "w2", "a0", "a1", "a2", "g1", "g2", "kk", "ka", "rk",
                                      "wr", "wk", "wv", "wo", "lnx_g", "lnx_b")}
    p["ln"] = w["ln_mix"][i]
    return p


def _ret_params(w, i):
    return {n: w["ret_" + n][i // N_MIXERS] for n in ("wq", "wk", "wv", "wg", "wo")}


def _ffn_of(w, which, i, tiles):
    ln_final = w["ln_final"] if (which == 2 and i == DEPTH - 1) else None
    return functools.partial(_ffn, ln_g=w[f"ln_ffn{which}"][i], wg=w[f"ff{which}_wg"], wu=w[f"ff{which}_wu"],
                             wd=w[f"ff{which}_wd"], ln_final=ln_final, layer=i, tm=tiles["ffn_m"],
                             tf=tiles["ffn_f"])


def _prefill(x, n_seq, w, tiles, side):
    M, D = x.shape
    cos, sin = _rope_tables(jnp.arange(M // n_seq, dtype=F32))
    new_shift, new_wkv, new_ret, side_o = [], [], [], []
    side = dict(side, first=0, buf=None)
    n_side = side["state"].shape[0]

    def ffn(x, which, i):
        call = _ffn_of(w, which, i, tiles)
        if side["first"] == n_side:
            return call(x)
        x, o, side["buf"] = call(x, side=side)
        side_o.append(o)
        side["first"] += o.shape[0]
        return x

    for i in range(DEPTH):
        x = ffn(x, 1, i)
        if i % N_MIXERS == 0:
            x, sh, st = _rwkv_mix(x, jnp.zeros((n_seq, D), F32), _rwkv_params(w, i), n_seq=n_seq,
                                  n_s=tiles["wkv_seqs"], T=tiles["wkv_tokens"])
            new_shift.append(sh)
            new_wkv.append(st)
        else:
            x, st = _ret_mix(x, w["ln_mix"][i], cos, sin, _ret_params(w, i), n_seq=n_seq, n_sub=tiles["ret_sub"])
            new_ret.append(st)
        x = ffn(x, 2, i)
    assert side["first"] == n_side, "the prompt's FFN calls must carry the whole decode retention state"
    return (x, jnp.stack(new_shift), jnp.stack(new_wkv), jnp.stack(new_ret),
            jnp.concatenate(side_o, axis=0), side["buf"])


def _decode(x, pos, shift0, wkv0, w, tiles):
    B, D = x.shape
    cos, sin = (jnp.tile(t, (B, 1)) for t in _rope_tables(pos))
    new_shift, new_wkv = [], []
    for i in range(DEPTH):
        x = _ffn_of(w, 1, i, tiles)(x)
        if i % N_MIXERS == 0:
            p = _rwkv_params(w, i)
            r, k, v, lw, a, g, sh = _rwkv_pre(x, shift0[i // N_MIXERS], p, tile=tiles["rwkv_pre"])
            y, st = _wkv_step(r, k, v, lw, a, p, wkv0[i // N_MIXERS])
            x = _rwkv_post(x, y, g, p["wo"], tm=tiles["post"])
            new_shift.append(sh)
            new_wkv.append(st)
        else:
            p = _ret_params(w, i)
            q, k, v, gate = _ret_pre(x, w["ln_mix"][i], cos, sin, p, tm=tiles["ret_pre"])
            o = yield q, k, v
            x = _ret_post(x, o, gate, p["wo"], tm=tiles["post"])
        x = _ffn_of(w, 2, i, tiles)(x)
    return x, jnp.stack(new_shift), jnp.stack(new_wkv)


PROMPT_TILES = dict(ffn_m=512, ffn_f=256, ret_sub=4, wkv_seqs=2, wkv_tokens=256, ret_side_seqs=2)
SAMPLE_TILES = dict(ffn_m=128, ffn_f=256, rwkv_pre=128, post=128, ret_pre=128)

FFN_WEIGHTS = ("ff1_wg", "ff1_wu", "ff1_wd", "ff2_wg", "ff2_wu", "ff2_wd")
MIXER_WEIGHTS = ("rw_w1", "rw_w2", "rw_a1", "rw_a2", "rw_g1", "rw_g2", "rw_wr", "rw_wk", "rw_wv", "rw_wo",
                 "ret_wq", "ret_wk", "ret_wv", "ret_wg", "ret_wo")


def kernel(x_prompt, x_sample, state_rwkv_shift, state_rwkv_wkv, state_ret, ln_ffn1, ff1_wg, ff1_wu, ff1_wd, ln_mix, ln_ffn2, ff2_wg, ff2_wu, ff2_wd, ln_final, rw_mu, rw_w0, rw_w1, rw_w2, rw_a0, rw_a1, rw_a2, rw_g1, rw_g2, rw_kk, rw_ka, rw_rk, rw_wr, rw_wk, rw_wv, rw_wo, rw_lnx_g, rw_lnx_b, ret_wq, ret_wk, ret_wv, ret_wg, ret_wo):
    w = dict(ln_ffn1=ln_ffn1, ff1_wg=ff1_wg, ff1_wu=ff1_wu, ff1_wd=ff1_wd, ln_mix=ln_mix,
             ln_ffn2=ln_ffn2, ff2_wg=ff2_wg, ff2_wu=ff2_wu, ff2_wd=ff2_wd, ln_final=ln_final,
             rw_mu=rw_mu, rw_w0=rw_w0, rw_w1=rw_w1, rw_w2=rw_w2, rw_a0=rw_a0, rw_a1=rw_a1,
             rw_a2=rw_a2, rw_g1=rw_g1, rw_g2=rw_g2, rw_kk=rw_kk, rw_ka=rw_ka, rw_rk=rw_rk,
             rw_wr=rw_wr, rw_wk=rw_wk, rw_wv=rw_wv, rw_wo=rw_wo, rw_lnx_g=rw_lnx_g,
             rw_lnx_b=rw_lnx_b, ret_wq=ret_wq, ret_wk=ret_wk, ret_wv=ret_wv, ret_wg=ret_wg,
             ret_wo=ret_wo)
    for name in FFN_WEIGHTS:
        w[name] = w[name].astype(BF16)
    for name in MIXER_WEIGHTS:
        w[name] = [w[name][i].astype(BF16) for i in range(w[name].shape[0])]
    B, L, D = x_prompt.shape
    Bs, Ls, _ = x_sample.shape
    assert Ls == 1 and state_ret.shape[0] == 1, "one decode token and one retention layer are supported"
    H, DK, DV = RET_HEADS, RET_DK, RET_DV
    decode = _decode(x_sample.reshape(Bs, D), PAST_LEN + jnp.arange(Ls, dtype=F32), state_rwkv_shift,
                     state_rwkv_wkv, w, SAMPLE_TILES)
    q, k, v = next(decode)
    columns = lambda t: t.reshape(Bs, H, DK).transpose(0, 2, 1)
    side = dict(q=columns(q), k=columns(k), v=v.reshape(Bs, H, DV), state=state_ret[0],
                seqs=PROMPT_TILES["ret_side_seqs"])
    y_p, p_shift, p_wkv, p_ret, o, s_ret = _prefill(x_prompt.reshape(B * L, D), B, w, PROMPT_TILES, side)
    try:
        decode.send(o.reshape(Bs, H * DV))
        raise AssertionError("the decode group has a single retention layer")
    except StopIteration as done:
        y_s, s_shift, s_wkv = done.value
    return (y_p.reshape(B, L, D), y_s.reshape(Bs, Ls, D), p_shift, p_wkv, p_ret, s_shift, s_wkv, s_ret[None])
```

```python
import functools
import math

import numpy as np
import jax
import jax.numpy as jnp
from jax import lax
from jax.experimental import pallas as pl
from jax.experimental.pallas import tpu as pltpu

F32 = jnp.float32
BF16 = jnp.bfloat16

DEPTH = 2
N_MIXERS = 2
RW_HEADS = 16
RW_HEAD = 64
RW_GN_EPS = 64e-5
RET_HEADS = 4
RET_DK = 256
RET_DV = 512
RET_CHUNK = 128
RET_GN_EPS = 1e-6
ROPE_BASE = 10000.0
FFN_RES = 0.5
RMS_EPS = 1e-6
PAST_LEN = 16384

WKV_CHUNK = 64
V7X_VMEM_CAP_BYTES = 60000 * 1024
SUBLANES = 8

NT_DIMS = (((1,), (1,)), ((), ()))
TN_DIMS = (((0,), (0,)), ((), ()))


def _vmem_limit(block_bytes, scratch_bytes=0, temp_bytes=0):
    return int(min(V7X_VMEM_CAP_BYTES, 2 * block_bytes + scratch_bytes + temp_bytes + (4 << 20)))


def _nbytes(shape, dtype):
    return int(np.prod(shape)) * jnp.dtype(dtype).itemsize


def _resident(shape, index_map):
    return pl.BlockSpec(shape, index_map, pipeline_mode=pl.Buffered(1))


def _rms(x, g):
    return x * lax.rsqrt(jnp.mean(x * x, axis=-1, keepdims=True) + RMS_EPS) * g


def _head_norm(y, eps):
    mu = jnp.mean(y, axis=-1, keepdims=True)
    yc = y - mu
    return yc * lax.rsqrt(jnp.mean(yc * yc, axis=-1, keepdims=True) + eps)


def _mm(a, b):
    return jnp.dot(a.astype(BF16), b.astype(BF16), preferred_element_type=F32)


def _mm_nt(a, b):
    return lax.dot_general(a.astype(BF16), b.astype(BF16), NT_DIMS, preferred_element_type=F32)


def _mm_tn(a, b):
    return lax.dot_general(a.astype(BF16), b.astype(BF16), TN_DIMS, preferred_element_type=F32)


def _ffn_body(*refs, ff_tile, final, side):
    refs = list(refs)
    x_ref, g_ref, wg_ref, wu_ref, wd_ref = refs[:5]
    del refs[:5]
    gf_ref = refs.pop(0) if final else None
    if side:
        side_in = refs[:5]
        del refs[:6 if side == "chained" else 5]
    o_ref = refs.pop(0)
    x = x_ref[...]
    h = _rms(x, g_ref[...]).astype(BF16)
    n_tiles = wg_ref.shape[1] // ff_tile

    def activation(j):
        sl = slice(j * ff_tile, (j + 1) * ff_tile)
        gate = jnp.dot(h, wg_ref[:, sl], preferred_element_type=F32)
        up = jnp.dot(h, wu_ref[:, sl], preferred_element_type=F32)
        return (gate * jax.nn.sigmoid(gate) * up).astype(BF16)

    side_units = _ret_side(*side_in, *refs) if side else []
    per_tile = -(-len(side_units) // n_tiles)
    act = activation(0)
    acc = None
    for j in range(n_tiles):
        nxt = activation(j + 1) if j + 1 < n_tiles else None
        part = jnp.dot(act, wd_ref[j * ff_tile:(j + 1) * ff_tile, :], preferred_element_type=F32)
        acc = part if acc is None else acc + part
        act = nxt
        for unit in side_units[j * per_tile:(j + 1) * per_tile]:
            unit()
    y = x + FFN_RES * acc
    if final:
        y = _rms(y, gf_ref[...])
    o_ref[...] = y


def _ffn(x, ln_g, wg, wu, wd, ln_final=None, *, layer, tm, tf, side=None):
    M, D = x.shape
    FF = wg.shape[2]
    final = ln_final is not None
    steps = M // tm
    row = lambda i: (i, 0)
    const = lambda i: (0, 0)
    of_layer = lambda i: (layer, 0, 0)
    in_specs = [pl.BlockSpec((tm, D), row), _resident((1, D), const), _resident((None, D, FF), of_layer),
                _resident((None, D, FF), of_layer), _resident((None, FF, D), of_layer)]
    args = [x, ln_g.reshape(1, D), wg, wu, wd]
    if final:
        in_specs.append(_resident((1, D), const))
        args.append(ln_final.reshape(1, D))
    out_specs = [pl.BlockSpec((tm, D), row)]
    out_shape = [jax.ShapeDtypeStruct((M, D), F32)]
    streamed = 2 * _nbytes((tm, D), F32)
    aliases = {}
    mode = None
    if side is not None:
        B, H, DK, DV = side["state"].shape
        bt = side["seqs"]
        first = side["first"] // bt
        taken = lambda i: (first + i, 0, 0)
        st = pl.BlockSpec((bt, H, DK, DV), lambda i: (first + i, 0, 0, 0))
        in_specs += [pl.BlockSpec(memory_space=pltpu.SMEM), pl.BlockSpec((bt, H, DK), taken),
                     pl.BlockSpec((bt, H, DK), taken), pl.BlockSpec((bt, H, DV), taken), st]
        args += [_ret_log_gamma(), side["q"], side["k"], side["v"], side["state"]]
        mode = "first"
        if side["buf"] is not None:
            mode = "chained"
            aliases = {len(args): 2}
            in_specs.append(pl.BlockSpec(memory_space=pl.ANY))
            args.append(side["buf"])
        out_specs += [pl.BlockSpec((bt, H, DV), lambda i: (i, 0, 0)), st]
        out_shape += [jax.ShapeDtypeStruct((steps * bt, H, DV), F32), jax.ShapeDtypeStruct((B, H, DK, DV), F32)]
        streamed += 2 * _nbytes((bt, H, DK, DV), F32) + 4 * _nbytes((bt, SUBLANES, DV), F32)
    resident = 3 * _nbytes((D, FF), BF16)
    temps = 6 * _nbytes((tm, tf), F32) + 3 * _nbytes((tm, D), F32)
    out = pl.pallas_call(
        functools.partial(_ffn_body, ff_tile=tf, final=final, side=mode),
        grid=(steps,), in_specs=in_specs, out_specs=out_specs, out_shape=out_shape,
        input_output_aliases=aliases,
        compiler_params=pltpu.CompilerParams(dimension_semantics=("parallel",),
                                             vmem_limit_bytes=_vmem_limit(streamed, resident, temps)),
        name="ffn_final" if final else "ffn",
    )(*args)
    return out if side is not None else out[0]


def _rwkv_project(h, xprev, mu_ref, w0_ref, a0_ref, wr_ref, wk_ref, wv_ref, w1_ref, w2_ref, a1_ref, a2_ref,
                  g1_ref, g2_ref):
    xx = xprev - h
    xr, xw, xk, xv, xa, xg = (h + xx * mu_ref[i:i + 1, :] for i in range(6))
    r = _mm(xr, wr_ref[...])
    k = _mm(xk, wk_ref[...])
    v = _mm(xv, wv_ref[...])
    z = w0_ref[...] + _mm(jnp.tanh(_mm(xw, w1_ref[...])), w2_ref[...])
    lw = -math.exp(-0.5) * jax.nn.sigmoid(z)
    a = jax.nn.sigmoid(a0_ref[...] + _mm(_mm(xa, a1_ref[...]), a2_ref[...]))
    g = _mm(jax.nn.sigmoid(_mm(xg, g1_ref[...])), g2_ref[...])
    return r, k, v, lw, a, g


def _rwkv_pre_body(x_ref, lng_ref, prev_ref, *refs):
    weight_refs, out_refs = refs[:12], refs[12:]
    h = _rms(x_ref[...], lng_ref[...])
    for ref, val in zip(out_refs, _rwkv_project(h, prev_ref[...], *weight_refs) + (h,)):
        ref[...] = val


def _rwkv_pre(x, prev, p, *, tile):
    M, D = x.shape
    vec = lambda a: a.reshape(1, D)
    row = lambda i: (i, 0)
    tok = pl.BlockSpec((tile, D), row)
    full = lambda a: _resident(a.shape, lambda i: (0, 0))
    params = [p["mu"], vec(p["w0"]), vec(p["a0"])]
    weights = [p["wr"], p["wk"], p["wv"], p["w1"], p["w2"], p["a1"], p["a2"], p["g1"], p["g2"]]
    streamed = 9 * _nbytes((tile, D), F32)
    resident = sum(_nbytes(w.shape, BF16) for w in weights)
    return pl.pallas_call(
        _rwkv_pre_body, grid=(M // tile,),
        in_specs=[tok, full(vec(p["ln"])), tok] + [full(t) for t in params + weights],
        out_specs=[tok] * 7,
        out_shape=[jax.ShapeDtypeStruct((M, D), F32)] * 7,
        compiler_params=pltpu.CompilerParams(
            dimension_semantics=("parallel",),
            vmem_limit_bytes=_vmem_limit(streamed, resident, 10 * _nbytes((tile, D), F32))),
        name="rwkv_pre",
    )(x, vec(p["ln"]), prev, *params, *weights)


def _wkv_chunk(r, k, v, lw, a, kkw, kaw, rkw, lnx_g, lnx_b, S, n_s):
    C = WKV_CHUNK
    N = RW_HEAD
    W = 2 * N
    T = n_s * C
    ii = lax.broadcasted_iota(jnp.int32, (T, T), 0)
    jj = lax.broadcasted_iota(jnp.int32, (T, T), 1)
    shift = C.bit_length() - 1
    tri = (ii >= jj) & (jnp.right_shift(ii, shift) == jnp.right_shift(jj, shift))
    lw_hi = lw.astype(BF16)
    rest = lw - lw_hi.astype(F32)
    lw_mid = rest.astype(BF16)
    lw_lo = (rest - lw_mid.astype(F32)).astype(BF16)
    tri = tri.astype(BF16)
    cl = (jnp.dot(tri, lw_hi, preferred_element_type=F32) + jnp.dot(tri, lw_mid, preferred_element_type=F32)
          + jnp.dot(tri, lw_lo, preferred_element_type=F32))
    gls = [cl[s * C + C - 1:s * C + C, :] for s in range(n_s)]
    gl = jnp.concatenate([jnp.broadcast_to(g, (C, g.shape[-1])) for g in gls], axis=0)
    e_cl = jnp.exp(cl)
    e_ncl = jnp.exp(-cl)
    e_ce = jnp.exp(cl - lw)
    e_g = jnp.exp(gl - cl)
    dgs = [jnp.exp(g) for g in gls]
    kkraw = k * kkw
    kmod = k * (1.0 + (a - 1.0) * kaw)
    rkk = r * kmod * rkw
    rt = r * e_cl
    kt = kmod * e_ncl
    kg = kmod * e_g
    row = lax.broadcasted_iota(jnp.int32, (C, W), 0)
    lane = lax.broadcasted_iota(jnp.int32, (C, W), 1)
    left = lane < N
    tok = jnp.where(left, lane, lane - N)
    strict = row > tok
    incl = row >= tok
    same_head = ((lax.broadcasted_iota(jnp.int32, (W, W), 0) < N)
                 == (lax.broadcasted_iota(jnp.int32, (W, W), 1) < N))

    def seg_sum(x):
        s_l = jnp.sum(jnp.where(left, x, 0.0), axis=-1, keepdims=True)
        s_r = jnp.sum(jnp.where(left, 0.0, x), axis=-1, keepdims=True)
        return jnp.where(left, s_l, s_r)

    def bd(x):
        xb = x.astype(BF16)
        zero = jnp.zeros_like(xb)
        return jnp.concatenate([jnp.where(left, xb, zero), jnp.where(left, zero, xb)], axis=0)

    units = list(S)
    at = {u: (slice(u[0] * C, (u[0] + 1) * C), slice(u[1] * W, (u[1] + 1) * W)) for u in units}
    V2 = {u: v[at[u]] for u in units}
    b2, lhs2, rhs_nt = {}, {}, {}
    for u in units:
        kk2 = kkraw[at[u]]
        kkn = kk2 / jnp.maximum(jnp.sqrt(seg_sum(kk2 * kk2)), 1e-12)
        b2[u] = kkn * a[at[u]]
        lhs2[u] = jnp.concatenate([-kkn * e_ce[at[u]], rt[at[u]]], axis=0).astype(BF16)
        rhs_nt[u] = jnp.concatenate([bd(b2[u] * e_ncl[at[u]]), bd(kt[at[u]])], axis=0)
    P = {u: lax.dot_general(lhs2[u], rhs_nt[u], NT_DIMS, preferred_element_type=F32) for u in units}
    LS = {u: lax.dot_general(lhs2[u], S[u].astype(BF16), NT_DIMS, preferred_element_type=F32) for u in units}
    bd_v = {u: bd(V2[u]) for u in units}
    X = {u: LS[u][:C] + jnp.dot(jnp.where(strict, P[u][:C, W:], 0.0).astype(BF16), bd_v[u],
                                preferred_element_type=F32) for u in units}
    Pk = {u: jnp.where(strict, P[u][:C, :W], 0.0) for u in units}
    span = 1
    while span < C:
        span *= 2
        if span < C:
            Z = {u: jnp.dot(Pk[u].astype(BF16), jnp.concatenate([bd(X[u]), bd(Pk[u])], axis=1),
                            preferred_element_type=F32) for u in units}
            X = {u: X[u] + Z[u][:, :W] for u in units}
            Pk = {u: Z[u][:, W:] for u in units}
        else:
            X = {u: X[u] + jnp.dot(Pk[u].astype(BF16), bd(X[u]), preferred_element_type=F32) for u in units}
    incl2 = jnp.concatenate([incl, incl], axis=1)
    Y = {u: LS[u][C:] + jnp.dot(jnp.where(incl2, P[u][C:], 0.0).astype(BF16),
                                jnp.concatenate([bd(X[u]), bd_v[u]], axis=0), preferred_element_type=F32)
         for u in units}
    s_new = {u: S[u] * dgs[u[0]][:, at[u][1]]
             + jnp.where(same_head,
                         _mm_tn(jnp.concatenate([X[u], V2[u]], axis=0),
                                jnp.concatenate([b2[u] * e_g[at[u]], kg[at[u]]], axis=0)), 0.0)
             for u in units}
    y = {}
    for u in units:
        sl = at[u][1]
        yc = Y[u] - seg_sum(Y[u]) * (1.0 / N)
        yn = yc * lax.rsqrt(seg_sum(yc * yc) * (1.0 / N) + RW_GN_EPS)
        y[u] = yn * lnx_g[:, sl] + lnx_b[:, sl] + seg_sum(rkk[at[u]]) * V2[u]
    return y, s_new


def _rwkv_mix_body(x_ref, lng_ref, prev_ref, mu_ref, w0_ref, a0_ref, kkw_ref, kaw_ref, rk_ref, lg_ref, lb_ref,
                   wr_ref, wk_ref, wv_ref, wo_ref, w1_ref, w2_ref, a1_ref, a2_ref, g1_ref, g2_ref,
                   out_ref, so_ref, s_ref,
                   carry, sbd_scr, r_scr, k_scr, v_scr, lw_scr, a_scr, y_scr, *, n_steps):
    C, N, W = WKV_CHUNK, RW_HEAD, 2 * RW_HEAD
    n_s, T, D = x_ref.shape
    step = pl.program_id(1)

    @pl.when(step == 0)
    def _():
        carry[...] = prev_ref[...]
        sbd_scr[...] = jnp.zeros_like(sbd_scr)

    x = x_ref[...].reshape(n_s * T, D)
    h = _rms(x, lng_ref[...])
    row = lax.broadcasted_iota(jnp.int32, h.shape, 0)
    xprev = pltpu.roll(h, 1, 0)
    for s in range(n_s):
        xprev = jnp.where(row == s * T, carry[s], xprev)
        last = h[s * T + T - 1:s * T + T, :]
        carry[s] = last
        so_ref[s] = last
    r_scr[...], k_scr[...], v_scr[...], lw_scr[...], a_scr[...], g = _rwkv_project(
        h, xprev, mu_ref, w0_ref, a0_ref, wr_ref, wk_ref, wv_ref, w1_ref, w2_ref, a1_ref, a2_ref, g1_ref, g2_ref)
    kkw, kaw, rkw, lnx_g, lnx_b = kkw_ref[...], kaw_ref[...], rk_ref[...], lg_ref[...], lb_ref[...]
    units = [(s, p) for s in range(n_s) for p in range(RW_HEADS // 2)]

    def chunk(c, loop_carry):
        starts = [pl.multiple_of(s * T + c * C, C) for s in range(n_s)]
        rows_of = lambda ref: jnp.concatenate([ref[pl.ds(st, C), :] for st in starts], axis=0)
        S = {u: sbd_scr[u[0], u[1]] for u in units}
        y, s_new = _wkv_chunk(rows_of(r_scr), rows_of(k_scr), rows_of(v_scr), rows_of(lw_scr), rows_of(a_scr),
                              kkw, kaw, rkw, lnx_g, lnx_b, S, n_s)
        for u in units:
            y_scr[pl.ds(starts[u[0]], C), u[1] * W:(u[1] + 1) * W] = y[u]
            sbd_scr[u[0], u[1]] = s_new[u]
        return loop_carry

    lax.fori_loop(0, T // C, chunk, 0)
    out_ref[...] = (x + _mm(y_scr[...] * g, wo_ref[...])).reshape(n_s, T, D)

    @pl.when(step == n_steps - 1)
    def _():
        for s, p in units:
            S = sbd_scr[s, p]
            s_ref[s, 2 * p] = S[:N, :N]
            s_ref[s, 2 * p + 1] = S[N:, N:]


def _rwkv_mix(x, prev, p, *, n_seq, n_s, T):
    M, D = x.shape
    L = M // n_seq
    nl = L // T
    H, N = RW_HEADS, RW_HEAD
    vec = lambda t: t.reshape(1, D)
    const = lambda b, l: (0, 0)
    tok = pl.BlockSpec((n_s, T, D), lambda b, l: (b, l, 0))
    row1 = pl.BlockSpec((n_s, 1, D), lambda b, l: (b, 0, 0))
    st = pl.BlockSpec((n_s, H, N, N), lambda b, l: (b, 0, 0, 0))
    params = [vec(p["ln"]), p["mu"], vec(p["w0"]), vec(p["a0"]), vec(p["kk"]), vec(p["ka"]), vec(p["rk"]),
              vec(p["lnx_g"]), vec(p["lnx_b"])]
    weights = [p["wr"], p["wk"], p["wv"], p["wo"], p["w1"], p["w2"], p["a1"], p["a2"], p["g1"], p["g2"]]
    full = lambda t: _resident(t.shape, const)
    in_specs = [tok, full(params[0]), row1] + [full(t) for t in params[1:]] + [full(t) for t in weights]
    rows = (n_s * T, D)
    pair_state = (n_s, H // 2, 2 * N, 2 * N)
    scratch = [pltpu.VMEM((n_s, 1, D), F32), pltpu.VMEM(pair_state, F32)] + [pltpu.VMEM(rows, F32)] * 6
    streamed = 2 * _nbytes((n_s, T, D), F32) + _nbytes((n_s, H, N, 128), F32)
    resident = sum(_nbytes(t.shape, BF16) for t in weights) + _nbytes(pair_state, F32) + 6 * _nbytes(rows, F32)
    out, so, state = pl.pallas_call(
        functools.partial(_rwkv_mix_body, n_steps=nl), grid=(n_seq // n_s, nl),
        in_specs=in_specs, out_specs=[tok, row1, st],
        out_shape=[jax.ShapeDtypeStruct((n_seq, L, D), F32), jax.ShapeDtypeStruct((n_seq, 1, D), F32),
                   jax.ShapeDtypeStruct((n_seq, H, N, N), F32)],
        scratch_shapes=scratch,
        compiler_params=pltpu.CompilerParams(dimension_semantics=("parallel", "arbitrary"),
                                             vmem_limit_bytes=_vmem_limit(streamed, resident, 12 * _nbytes(rows, F32))),
        name="rwkv_mix",
    )(x.reshape(n_seq, L, D), params[0], prev.reshape(n_seq, 1, D), *params[1:], *weights)
    return out.reshape(M, D), so.reshape(n_seq, D), state


def _wkv_step_body(r_ref, k_ref, v_ref, lw_ref, a_ref, kkw_ref, kaw_ref, rk_ref, lg_ref, lb_ref, s_ref,
                   y_ref, so_ref, y_scr):
    r = r_ref[0]
    k = k_ref[0]
    v = v_ref[0]
    a = a_ref[0]
    d = jnp.exp(lw_ref[0])
    kkh = k * kkw_ref[0]
    nrm = jnp.sqrt(jnp.sum(kkh * kkh, axis=0, keepdims=True))
    kkn = kkh / jnp.maximum(nrm, 1e-12)
    kmod = k * (1.0 + (a - 1.0) * kaw_ref[0])
    bh = kkn * a

    def value_channel(i, carry):
        S = s_ref[0, i]
        sa = -jnp.sum(S * kkn, axis=0, keepdims=True)
        s_new = S * d + sa * bh + v_ref[0, pl.ds(i, 1), :] * kmod
        so_ref[0, i] = s_new
        y_scr[pl.ds(i, 1), :] = jnp.sum(s_new * r, axis=0, keepdims=True)
        return carry

    lax.fori_loop(0, RW_HEAD, value_channel, 0, unroll=8)
    y = y_scr[...]
    mu = jnp.mean(y, axis=0, keepdims=True)
    yc = y - mu
    yn = yc * lax.rsqrt(jnp.mean(yc * yc, axis=0, keepdims=True) + RW_GN_EPS)
    y_ref[0] = yn * lg_ref[0] + lb_ref[0] + jnp.sum(r * kmod * rk_ref[0], axis=0, keepdims=True) * v


def _wkv_step(r, k, v, lw, a, p, state):
    B, D = r.shape
    H, N = RW_HEADS, RW_HEAD
    lanes = lambda t: t.T.reshape(H, N, B)
    par = lambda t: jnp.broadcast_to(t.reshape(H, N, 1), (H, N, B))
    vec = pl.BlockSpec((1, N, B), lambda h: (h, 0, 0))
    st = pl.BlockSpec((1, N, N, B), lambda h: (h, 0, 0, 0))
    blocks = 2 * _nbytes((N, N, B), F32) + 11 * _nbytes((N, B), F32)
    y, s_new = pl.pallas_call(
        _wkv_step_body, grid=(H,),
        in_specs=[vec] * 10 + [st],
        out_specs=[vec, st],
        out_shape=[jax.ShapeDtypeStruct((H, N, B), F32), jax.ShapeDtypeStruct((H, N, N, B), F32)],
        scratch_shapes=[pltpu.VMEM((N, B), F32)],
        compiler_params=pltpu.CompilerParams(dimension_semantics=("parallel",),
                                             vmem_limit_bytes=_vmem_limit(blocks, _nbytes((N, B), F32), 4 << 20)),
        name="wkv_step",
    )(lanes(r), lanes(k), lanes(v), lanes(lw), lanes(a),
      par(p["kk"]), par(p["ka"]), par(p["rk"]), par(p["lnx_g"]), par(p["lnx_b"]),
      jnp.transpose(state, (1, 2, 3, 0)))
    return y.reshape(D, B).T, jnp.transpose(s_new, (3, 0, 1, 2))


def _rwkv_post_body(x_ref, y_ref, g_ref, wo_ref, o_ref):
    o_ref[...] = x_ref[...] + _mm(y_ref[...] * g_ref[...], wo_ref[...])


def _rwkv_post(x, y, g, wo, *, tm):
    M, D = x.shape
    tok = pl.BlockSpec((tm, D), lambda i: (i, 0))
    streamed = 4 * _nbytes((tm, D), F32)
    return pl.pallas_call(
        _rwkv_post_body, grid=(M // tm,),
        in_specs=[tok, tok, tok, _resident(wo.shape, lambda i: (0, 0))],
        out_specs=tok, out_shape=jax.ShapeDtypeStruct((M, D), F32),
        compiler_params=pltpu.CompilerParams(
            dimension_semantics=("parallel",),
            vmem_limit_bytes=_vmem_limit(streamed, _nbytes(wo.shape, BF16), 3 * _nbytes((tm, D), F32))),
        name="rwkv_post",
    )(x, y, g, wo)


def _ret_pre_body(x_ref, lng_ref, cos_ref, sin_ref, wq_ref, wk_ref, wv_ref, wg_ref,
                  q_ref, k_ref, v_ref, gate_ref, h_scr):
    @pl.when(pl.program_id(1) == 0)
    def _():
        h_scr[...] = _rms(x_ref[...], lng_ref[...]).astype(BF16)

    h = h_scr[...]
    cos = cos_ref[...]
    sin = sin_ref[...]
    half = RET_DK // 2

    def rotate(t, out_ref):
        t1, t2 = t[:, :half], t[:, half:]
        out_ref[:, :half] = t1 * cos - t2 * sin
        out_ref[:, half:] = t1 * sin + t2 * cos

    rotate(jnp.dot(h, wq_ref[...], preferred_element_type=F32), q_ref)
    rotate(jnp.dot(h, wk_ref[...], preferred_element_type=F32) * (RET_DK ** -0.5), k_ref)
    v_ref[...] = jnp.dot(h, wv_ref[...], preferred_element_type=F32)
    gz = jnp.dot(h, wg_ref[...], preferred_element_type=F32)
    gate_ref[...] = gz * jax.nn.sigmoid(gz)


def _ret_pre(x, ln_g, cos, sin, p, *, tm):
    M, D = x.shape
    H, DK, DV = RET_HEADS, RET_DK, RET_DV
    n_tab = cos.shape[0] // tm
    tab = pl.BlockSpec((tm, DK // 2), lambda i, h: (i % n_tab, 0))
    col = lambda w: pl.BlockSpec((D, w), lambda i, h: (0, h))
    out = lambda w: pl.BlockSpec((tm, w), lambda i, h: (i, h))
    blocks = (_nbytes((tm, D), F32) + 2 * _nbytes((tm, DK // 2), F32) + 2 * _nbytes((D, DK), BF16)
              + 2 * _nbytes((D, DV), BF16) + 2 * _nbytes((tm, DK), F32) + 2 * _nbytes((tm, DV), F32))
    return pl.pallas_call(
        _ret_pre_body, grid=(M // tm, H),
        in_specs=[pl.BlockSpec((tm, D), lambda i, h: (i, 0)), pl.BlockSpec((1, D), lambda i, h: (0, 0)),
                  tab, tab, col(DK), col(DK), col(DV), col(DV)],
        out_specs=[out(DK), out(DK), out(DV), out(DV)],
        out_shape=[jax.ShapeDtypeStruct((M, H * DK), F32), jax.ShapeDtypeStruct((M, H * DK), F32),
                   jax.ShapeDtypeStruct((M, H * DV), F32), jax.ShapeDtypeStruct((M, H * DV), F32)],
        scratch_shapes=[pltpu.VMEM((tm, D), BF16)],
        compiler_params=pltpu.CompilerParams(
            dimension_semantics=("parallel", "arbitrary"),
            vmem_limit_bytes=_vmem_limit(blocks, _nbytes((tm, D), BF16), 4 * _nbytes((tm, DV), F32))),
        name="ret_pre",
    )(x, ln_g.reshape(1, D), cos, sin, p["wq"], p["wk"], p["wv"], p["wg"])


def _ret_log_gamma():
    return jnp.log1p(-jnp.exp2(-5.0 - jnp.arange(RET_HEADS, dtype=F32)))


def _ret_mix_body(lg_ref, x_ref, lng_ref, cos_ref, sin_ref, wq_ref, wk_ref, wv_ref, wg_ref, wo_ref,
                  out_ref, s_ref, *, n_sub):
    C, DK, DV, H = RET_CHUNK, RET_DK, RET_DV, RET_HEADS
    half = DK // 2

    @pl.when(pl.program_id(1) == 0)
    def _():
        s_ref[...] = jnp.zeros_like(s_ref)

    x = x_ref[...]
    h = _rms(x, lng_ref[...]).astype(BF16)
    cos = cos_ref[...]
    sin = sin_ref[...]
    ii = lax.broadcasted_iota(jnp.int32, (C, C), 0)
    jj = lax.broadcasted_iota(jnp.int32, (C, C), 1)
    diff = jnp.maximum((ii - jj).astype(F32), 0.0)
    idx = lax.broadcasted_iota(jnp.int32, (C, 1), 0).astype(F32)
    rows = lambda cc: slice(cc * C, (cc + 1) * C)

    def rotate(t):
        t1, t2 = t[:, :half], t[:, half:]
        return jnp.concatenate([t1 * cos - t2 * sin, t1 * sin + t2 * cos], axis=-1)

    def project(hd):
        qs, vs = slice(hd * DK, (hd + 1) * DK), slice(hd * DV, (hd + 1) * DV)
        q = rotate(jnp.dot(h, wq_ref[:, qs], preferred_element_type=F32))
        k = rotate(jnp.dot(h, wk_ref[:, qs], preferred_element_type=F32) * (DK ** -0.5))
        v = jnp.dot(h, wv_ref[:, vs], preferred_element_type=F32).astype(BF16)
        gz = jnp.dot(h, wg_ref[:, vs], preferred_element_type=F32)
        return q, k, v, gz * jax.nn.sigmoid(gz)

    def retain(hd, q, k, v):
        lg = lg_ref[hd]
        dmat = jnp.where(ii >= jj, jnp.exp(lg * diff), 0.0)
        q_dec = jnp.exp(lg * (idx + 1.0))
        k_dec = jnp.exp(lg * (C - 1.0 - idx))
        s_dec = jnp.exp(jnp.full((1, DV), lg * C, F32))
        subs = range(n_sub)
        inner = [_mm_nt(q[rows(cc)], k[rows(cc)]) * dmat for cc in subs]
        kv = [lax.dot_general((k[rows(cc)] * k_dec).astype(BF16), v[rows(cc)], TN_DIMS,
                              preferred_element_type=F32) for cc in subs]
        S = s_ref[0, hd]
        outs = []
        for cc in subs:
            outs.append(jnp.dot(inner[cc].astype(BF16), v[rows(cc)], preferred_element_type=F32)
                        + _mm(q[rows(cc)] * q_dec, S))
            S = S * s_dec + kv[cc]
        s_ref[0, hd] = S
        return jnp.concatenate(outs, axis=0)

    acc = x
    proj = project(0)
    for hd in range(H):
        nxt = project(hd + 1) if hd + 1 < H else None
        q, k, v, gate = proj
        o = retain(hd, q, k, v)
        acc = acc + _mm(gate * _head_norm(o, RET_GN_EPS), wo_ref[hd * DV:(hd + 1) * DV, :])
        proj = nxt
    out_ref[...] = acc


def _ret_mix(x, ln_g, cos, sin, p, *, n_seq, n_sub):
    M, D = x.shape
    H, DK, DV, C = RET_HEADS, RET_DK, RET_DV, RET_CHUNK
    T = C * n_sub
    nl = M // n_seq // T
    const = lambda b, l: (0, 0)
    tok = pl.BlockSpec((T, D), lambda b, l: (b * nl + l, 0))
    tab = pl.BlockSpec((T, DK // 2), lambda b, l: (l, 0))
    st = pl.BlockSpec((1, H, DK, DV), lambda b, l: (b, 0, 0, 0))
    weights = [p["wq"], p["wk"], p["wv"], p["wg"], p["wo"]]
    streamed = 2 * _nbytes((T, D), F32) + 2 * _nbytes((T, DK // 2), F32) + _nbytes((H, DK, DV), F32)
    resident = sum(_nbytes(w.shape, BF16) for w in weights)
    temps = 8 * _nbytes((T, DV), F32) + 4 * _nbytes((DK, DV), F32)
    return pl.pallas_call(
        functools.partial(_ret_mix_body, n_sub=n_sub), grid=(n_seq, nl),
        in_specs=[pl.BlockSpec(memory_space=pltpu.SMEM), tok, _resident((1, D), const), tab, tab]
                 + [_resident(w.shape, const) for w in weights],
        out_specs=[tok, st],
        out_shape=[jax.ShapeDtypeStruct((M, D), F32), jax.ShapeDtypeStruct((n_seq, H, DK, DV), F32)],
        compiler_params=pltpu.CompilerParams(dimension_semantics=("parallel", "arbitrary"),
                                             vmem_limit_bytes=_vmem_limit(streamed, resident, temps)),
        name="ret_mix",
    )(_ret_log_gamma(), x, ln_g.reshape(1, D), cos, sin, *weights)


def _ret_side(lg_ref, q_ref, k_ref, v_ref, s_ref, o_ref, so_ref):
    bt, H, DK = q_ref.shape
    q_cols = q_ref[...].reshape(bt * H, DK).T
    k_cols = k_ref[...].reshape(bt * H, DK).T

    def unit(b, hd):
        gam = jnp.exp(jnp.full((1, RET_DV), lg_ref[hd], F32))
        q = q_cols[:, b * H + hd:b * H + hd + 1]
        k = k_cols[:, b * H + hd:b * H + hd + 1]
        v = v_ref[b, hd:hd + 1, :]
        S = s_ref[b, hd]
        so_ref[b, hd] = S * gam + k * v
        o_ref[b, hd:hd + 1, :] = (gam * jnp.sum(q * S, axis=0, keepdims=True)
                                  + jnp.sum(q * k, axis=0, keepdims=True) * v)

    return [functools.partial(unit, b, hd) for b in range(bt) for hd in range(H)]


def _ret_post_body(x_ref, o_ref, gate_ref, wo_ref, out_ref):
    acc = x_ref[...]
    for hd in range(RET_HEADS):
        sl = slice(hd * RET_DV, (hd + 1) * RET_DV)
        acc = acc + _mm(gate_ref[:, sl] * _head_norm(o_ref[:, sl], RET_GN_EPS), wo_ref[sl, :])
    out_ref[...] = acc


def _ret_post(x, o, gate, wo, *, tm):
    M, D = x.shape
    W = o.shape[1]
    tok = lambda w: pl.BlockSpec((tm, w), lambda i: (i, 0))
    streamed = 2 * _nbytes((tm, D), F32) + 2 * _nbytes((tm, W), F32)
    return pl.pallas_call(
        _ret_post_body, grid=(M // tm,),
        in_specs=[tok(D), tok(W), tok(W), _resident(wo.shape, lambda i: (0, 0))],
        out_specs=tok(D), out_shape=jax.ShapeDtypeStruct((M, D), F32),
        compiler_params=pltpu.CompilerParams(
            dimension_semantics=("parallel",),
            vmem_limit_bytes=_vmem_limit(streamed, _nbytes(wo.shape, BF16), 3 * _nbytes((tm, W), F32))),
        name="ret_post",
    )(x, o, gate, wo)


def _rope_tables(pos):
    half = RET_DK // 2
    inv = ROPE_BASE ** (-jnp.arange(half, dtype=F32) / half)
    ang = pos.astype(F32)[:, None] * inv[None, :]
    return jnp.cos(ang), jnp.sin(ang)


def _rwkv_params(w, i):
    j = i // N_MIXERS
    p = {n: w["rw_" + n][j] for n in ("mu", "w0", "w1", "w2", "a0", "a1", "a2", "g1", "g2", "kk", "ka", "rk",
                                      "wr", "wk", "wv", "wo", "lnx_g", "lnx_b")}
    p["ln"] = w["ln_mix"][i]
    return p


def _ret_params(w, i):
    return {n: w["ret_" + n][i // N_MIXERS] for n in ("wq", "wk", "wv", "wg", "wo")}


def _ffn_of(w, which, i, tiles, tm="ffn_m"):
    ln_final = w["ln_final"] if (which == 2 and i == DEPTH - 1) else None
    return functools.partial(_ffn, ln_g=w[f"ln_ffn{which}"][i], wg=w[f"ff{which}_wg"], wu=w[f"ff{which}_wu"],
                             wd=w[f"ff{which}_wd"], ln_final=ln_final, layer=i, tm=tiles[tm],
                             tf=tiles["ffn_f"])


def _prefill(x, n_seq, w, tiles, side):
    M, D = x.shape
    cos, sin = _rope_tables(jnp.arange(M // n_seq, dtype=F32))
    new_shift, new_wkv, new_ret, side_o = [], [], [], []
    side = dict(side, first=0, buf=None)
    n_side = side["state"].shape[0]

    def ffn(x, which, i):
        if side["first"] == n_side:
            return _ffn_of(w, which, i, tiles)(x)
        x, o, side["buf"] = _ffn_of(w, which, i, tiles, tm="ffn_m_side")(x, side=side)
        side_o.append(o)
        side["first"] += o.shape[0]
        return x

    for i in range(DEPTH):
        x = ffn(x, 1, i)
        if i % N_MIXERS == 0:
            x, sh, st = _rwkv_mix(x, jnp.zeros((n_seq, D), F32), _rwkv_params(w, i), n_seq=n_seq,
                                  n_s=tiles["wkv_seqs"], T=tiles["wkv_tokens"])
            new_shift.append(sh)
            new_wkv.append(st)
        else:
            x, st = _ret_mix(x, w["ln_mix"][i], cos, sin, _ret_params(w, i), n_seq=n_seq, n_sub=tiles["ret_sub"])
            new_ret.append(st)
        x = ffn(x, 2, i)
    assert side["first"] == n_side, "the prompt's FFN calls must carry the whole decode retention state"
    return (x, jnp.stack(new_shift), jnp.stack(new_wkv), jnp.stack(new_ret),
            jnp.concatenate(side_o, axis=0), side["buf"])


def _decode(x, pos, shift0, wkv0, w, tiles):
    B, D = x.shape
    cos, sin = (jnp.tile(t, (B, 1)) for t in _rope_tables(pos))
    new_shift, new_wkv = [], []
    for i in range(DEPTH):
        x = _ffn_of(w, 1, i, tiles)(x)
        if i % N_MIXERS == 0:
            p = _rwkv_params(w, i)
            r, k, v, lw, a, g, sh = _rwkv_pre(x, shift0[i // N_MIXERS], p, tile=tiles["rwkv_pre"])
            y, st = _wkv_step(r, k, v, lw, a, p, wkv0[i // N_MIXERS])
            x = _rwkv_post(x, y, g, p["wo"], tm=tiles["post"])
            new_shift.append(sh)
            new_wkv.append(st)
        else:
            p = _ret_params(w, i)
            q, k, v, gate = _ret_pre(x, w["ln_mix"][i], cos, sin, p, tm=tiles["ret_pre"])
            o = yield q, k, v
            x = _ret_post(x, o, gate, p["wo"], tm=tiles["post"])
        x = _ffn_of(w, 2, i, tiles)(x)
    return x, jnp.stack(new_shift), jnp.stack(new_wkv)


PROMPT_TILES = dict(ffn_m=1024, ffn_m_side=512, ffn_f=256, ret_sub=4, wkv_seqs=2, wkv_tokens=256,
                    ret_side_seqs=2)
SAMPLE_TILES = dict(ffn_m=128, ffn_f=256, rwkv_pre=128, post=128, ret_pre=128)

FFN_WEIGHTS = ("ff1_wg", "ff1_wu", "ff1_wd", "ff2_wg", "ff2_wu", "ff2_wd")
MIXER_WEIGHTS = ("rw_w1", "rw_w2", "rw_a1", "rw_a2", "rw_g1", "rw_g2", "rw_wr", "rw_wk", "rw_wv", "rw_wo",
                 "ret_wq", "ret_wk", "ret_wv", "ret_wg", "ret_wo")


def kernel(x_prompt, x_sample, state_rwkv_shift, state_rwkv_wkv, state_ret, ln_ffn1, ff1_wg, ff1_wu, ff1_wd, ln_mix, ln_ffn2, ff2_wg, ff2_wu, ff2_wd, ln_final, rw_mu, rw_w0, rw_w1, rw_w2, rw_a0, rw_a1, rw_a2, rw_g1, rw_g2, rw_kk, rw_ka, rw_rk, rw_wr, rw_wk, rw_wv, rw_wo, rw_lnx_g, rw_lnx_b, ret_wq, ret_wk, ret_wv, ret_wg, ret_wo):
    w = dict(ln_ffn1=ln_ffn1, ff1_wg=ff1_wg, ff1_wu=ff1_wu, ff1_wd=ff1_wd, ln_mix=ln_mix,
             ln_ffn2=ln_ffn2, ff2_wg=ff2_wg, ff2_wu=ff2_wu, ff2_wd=ff2_wd, ln_final=ln_final,
             rw_mu=rw_mu, rw_w0=rw_w0, rw_w1=rw_w1, rw_w2=rw_w2, rw_a0=rw_a0, rw_a1=rw_a1,
             rw_a2=rw_a2, rw_g1=rw_g1, rw_g2=rw_g2, rw_kk=rw_kk, rw_ka=rw_ka, rw_rk=rw_rk,
             rw_wr=rw_wr, rw_wk=rw_wk, rw_wv=rw_wv, rw_wo=rw_wo, rw_lnx_g=rw_lnx_g,
             rw_lnx_b=rw_lnx_b, ret_wq=ret_wq, ret_wk=ret_wk, ret_wv=ret_wv, ret_wg=ret_wg,
             ret_wo=ret_wo)
    for name in FFN_WEIGHTS:
        w[name] = w[name].astype(BF16)
    for name in MIXER_WEIGHTS:
        w[name] = [w[name][i].astype(BF16) for i in range(w[name].shape[0])]
    B, L, D = x_prompt.shape
    Bs, Ls, _ = x_sample.shape
    assert Ls == 1 and state_ret.shape[0] == 1, "one decode token and one retention layer are supported"
    H, DK, DV = RET_HEADS, RET_DK, RET_DV
    decode = _decode(x_sample.reshape(Bs, D), PAST_LEN + jnp.arange(Ls, dtype=F32), state_rwkv_shift,
                     state_rwkv_wkv, w, SAMPLE_TILES)
    q, k, v = next(decode)
    side = dict(q=q.reshape(Bs, H, DK), k=k.reshape(Bs, H, DK), v=v.reshape(Bs, H, DV), state=state_ret[0],
                seqs=PROMPT_TILES["ret_side_seqs"])
    y_p, p_shift, p_wkv, p_ret, o, s_ret = _prefill(x_prompt.reshape(B * L, D), B, w, PROMPT_TILES, side)
    try:
        decode.send(o.reshape(Bs, H * DV))
        raise AssertionError("the decode group has a single retention layer")
    except StopIteration as done:
        y_s, s_shift, s_wkv = done.value
    return (y_p.reshape(B, L, D), y_s.reshape(Bs, Ls, D), p_shift, p_wkv, p_ret, s_shift, s_wkv, s_ret[None])
```

```python
import functools
import math

import numpy as np
import jax
import jax.numpy as jnp
from jax import lax
from jax.experimental import pallas as pl
from jax.experimental.pallas import tpu as pltpu

F32 = jnp.float32
BF16 = jnp.bfloat16

DEPTH = 2
N_MIXERS = 2
RW_HEADS = 16
RW_HEAD = 64
RW_GN_EPS = 64e-5
RET_HEADS = 4
RET_DK = 256
RET_DV = 512
RET_CHUNK = 128
RET_GN_EPS = 1e-6
ROPE_BASE = 10000.0
FFN_RES = 0.5
RMS_EPS = 1e-6
PAST_LEN = 16384

WKV_CHUNK = 64
V7X_VMEM_CAP_BYTES = 60000 * 1024
SUBLANES = 8
BF16_SUBLANES = 16

NT_DIMS = (((1,), (1,)), ((), ()))
TN_DIMS = (((0,), (0,)), ((), ()))


def _vmem_limit(block_bytes, scratch_bytes=0, temp_bytes=0):
    return int(min(V7X_VMEM_CAP_BYTES, 2 * block_bytes + scratch_bytes + temp_bytes + (4 << 20)))


def _nbytes(shape, dtype):
    return int(np.prod(shape)) * jnp.dtype(dtype).itemsize


def _resident(shape, index_map):
    return pl.BlockSpec(shape, index_map, pipeline_mode=pl.Buffered(1))


def _rms(x, g):
    return x * lax.rsqrt(jnp.mean(x * x, axis=-1, keepdims=True) + RMS_EPS) * g


def _head_norm(y, eps):
    mu = jnp.mean(y, axis=-1, keepdims=True)
    yc = y - mu
    return yc * lax.rsqrt(jnp.mean(yc * yc, axis=-1, keepdims=True) + eps)


def _mm(a, b):
    return jnp.dot(a.astype(BF16), b.astype(BF16), preferred_element_type=F32)


def _mm_nt(a, b):
    return lax.dot_general(a.astype(BF16), b.astype(BF16), NT_DIMS, preferred_element_type=F32)


def _mm_tn(a, b):
    return lax.dot_general(a.astype(BF16), b.astype(BF16), TN_DIMS, preferred_element_type=F32)


def _ffn_body(*refs, ff_tile, final, side, n_cast):
    refs = list(refs)
    x_ref, g_ref, wg_ref, wu_ref, wd_ref = refs[:5]
    del refs[:5]
    gf_ref = refs.pop(0) if final else None
    if side:
        side_in = refs[:5]
        del refs[:6 if side == "chained" else 5]
    cast_in = refs[:n_cast]
    del refs[:n_cast]
    o_ref = refs.pop(0)
    cast_out = refs[len(refs) - n_cast:]
    del refs[len(refs) - n_cast:]
    x = x_ref[...]
    h = _rms(x, g_ref[...]).astype(BF16)
    n_tiles = wg_ref.shape[1] // ff_tile

    def activation(j):
        sl = slice(j * ff_tile, (j + 1) * ff_tile)
        gate = jnp.dot(h, wg_ref[:, sl], preferred_element_type=F32)
        up = jnp.dot(h, wu_ref[:, sl], preferred_element_type=F32)
        return (gate * jax.nn.sigmoid(gate) * up).astype(BF16)

    side_units = _ret_side(*side_in, *refs) if side else []
    per_tile = -(-len(side_units) // n_tiles)
    act = activation(0)
    acc = None
    for j in range(n_tiles):
        nxt = activation(j + 1) if j + 1 < n_tiles else None
        part = jnp.dot(act, wd_ref[j * ff_tile:(j + 1) * ff_tile, :], preferred_element_type=F32)
        acc = part if acc is None else acc + part
        act = nxt
        for unit in side_units[j * per_tile:(j + 1) * per_tile]:
            unit()
    y = x + FFN_RES * acc
    if final:
        y = _rms(y, gf_ref[...])
    o_ref[...] = y
    for src, dst in zip(cast_in, cast_out):
        dst[...] = src[...].astype(BF16)


def _ffn(x, ln_g, wg, wu, wd, ln_final=None, *, layer, tm, tf, side=None, cast=()):
    M, D = x.shape
    FF = wg.shape[2]
    final = ln_final is not None
    steps = M // tm
    row = lambda i: (i, 0)
    const = lambda i: (0, 0)
    of_layer = lambda i: (layer, 0, 0)
    in_specs = [pl.BlockSpec((tm, D), row), _resident((1, D), const), _resident((None, D, FF), of_layer),
                _resident((None, D, FF), of_layer), _resident((None, FF, D), of_layer)]
    args = [x, ln_g.reshape(1, D), wg, wu, wd]
    if final:
        in_specs.append(_resident((1, D), const))
        args.append(ln_final.reshape(1, D))
    out_specs = [pl.BlockSpec((tm, D), row)]
    out_shape = [jax.ShapeDtypeStruct((M, D), F32)]
    streamed = 2 * _nbytes((tm, D), F32)
    aliases = {}
    mode = None
    if side is not None:
        B, H, DK, DV = side["state"].shape
        bt = side["seqs"]
        first = side["first"] // bt
        taken = lambda i: (first + i, 0, 0)
        st = pl.BlockSpec((bt, H, DK, DV), lambda i: (first + i, 0, 0, 0))
        in_specs += [pl.BlockSpec(memory_space=pltpu.SMEM), pl.BlockSpec((bt, H, DK), taken),
                     pl.BlockSpec((bt, H, DK), taken), pl.BlockSpec((bt, H, DV), taken), st]
        args += [_ret_log_gamma(), side["q"], side["k"], side["v"], side["state"]]
        mode = "first"
        if side["buf"] is not None:
            mode = "chained"
            aliases = {len(args): 2}
            in_specs.append(pl.BlockSpec(memory_space=pl.ANY))
            args.append(side["buf"])
        out_specs += [pl.BlockSpec((bt, H, DV), lambda i: (i, 0, 0)), st]
        out_shape += [jax.ShapeDtypeStruct((steps * bt, H, DV), F32), jax.ShapeDtypeStruct((B, H, DK, DV), F32)]
        streamed += 2 * _nbytes((bt, H, DK, DV), F32) + 4 * _nbytes((bt, SUBLANES, DV), F32)
    for t in cast:
        rows, cols = t.shape
        assert rows % (steps * BF16_SUBLANES) == 0, "row blocks must be whole bf16 tiles"
        blk = pl.BlockSpec((rows // steps, cols), row)
        in_specs.append(blk)
        args.append(t)
        out_specs.append(blk)
        out_shape.append(jax.ShapeDtypeStruct(t.shape, BF16))
        streamed += _nbytes((rows // steps, cols), F32) + _nbytes((rows // steps, cols), BF16)
    resident = 3 * _nbytes((D, FF), BF16)
    temps = 6 * _nbytes((tm, tf), F32) + 3 * _nbytes((tm, D), F32)
    out = pl.pallas_call(
        functools.partial(_ffn_body, ff_tile=tf, final=final, side=mode, n_cast=len(cast)),
        grid=(steps,), in_specs=in_specs, out_specs=out_specs, out_shape=out_shape,
        input_output_aliases=aliases,
        compiler_params=pltpu.CompilerParams(dimension_semantics=("parallel",),
                                             vmem_limit_bytes=_vmem_limit(streamed, resident, temps)),
        name="ffn_final" if final else "ffn",
    )(*args)
    if side is None and not cast:
        return out[0]
    o, state = out[1:3] if side is not None else (None, None)
    return out[0], o, state, tuple(out[len(out) - len(cast):])


def _rwkv_project(h, xprev, mu_ref, w0_ref, a0_ref, wr_ref, wk_ref, wv_ref, w1_ref, w2_ref, a1_ref, a2_ref,
                  g1_ref, g2_ref):
    xx = xprev - h
    xr, xw, xk, xv, xa, xg = (h + xx * mu_ref[i:i + 1, :] for i in range(6))
    r = _mm(xr, wr_ref[...])
    k = _mm(xk, wk_ref[...])
    v = _mm(xv, wv_ref[...])
    z = w0_ref[...] + _mm(jnp.tanh(_mm(xw, w1_ref[...])), w2_ref[...])
    lw = -math.exp(-0.5) * jax.nn.sigmoid(z)
    a = jax.nn.sigmoid(a0_ref[...] + _mm(_mm(xa, a1_ref[...]), a2_ref[...]))
    g = _mm(jax.nn.sigmoid(_mm(xg, g1_ref[...])), g2_ref[...])
    return r, k, v, lw, a, g


def _rwkv_pre_body(x_ref, lng_ref, prev_ref, *refs):
    weight_refs, out_refs = refs[:12], refs[12:]
    h = _rms(x_ref[...], lng_ref[...])
    for ref, val in zip(out_refs, _rwkv_project(h, prev_ref[...], *weight_refs) + (h,)):
        ref[...] = val


def _rwkv_pre(x, prev, p, *, tile):
    M, D = x.shape
    vec = lambda a: a.reshape(1, D)
    row = lambda i: (i, 0)
    tok = pl.BlockSpec((tile, D), row)
    full = lambda a: _resident(a.shape, lambda i: (0, 0))
    params = [p["mu"], vec(p["w0"]), vec(p["a0"])]
    weights = [p["wr"], p["wk"], p["wv"], p["w1"], p["w2"], p["a1"], p["a2"], p["g1"], p["g2"]]
    streamed = 9 * _nbytes((tile, D), F32)
    resident = sum(_nbytes(w.shape, BF16) for w in weights)
    return pl.pallas_call(
        _rwkv_pre_body, grid=(M // tile,),
        in_specs=[tok, full(vec(p["ln"])), tok] + [full(t) for t in params + weights],
        out_specs=[tok] * 7,
        out_shape=[jax.ShapeDtypeStruct((M, D), F32)] * 7,
        compiler_params=pltpu.CompilerParams(
            dimension_semantics=("parallel",),
            vmem_limit_bytes=_vmem_limit(streamed, resident, 10 * _nbytes((tile, D), F32))),
        name="rwkv_pre",
    )(x, vec(p["ln"]), prev, *params, *weights)


def _wkv_chunk(r, k, v, lw, a, kkw, kaw, rkw, lnx_g, lnx_b, S, n_s):
    C = WKV_CHUNK
    N = RW_HEAD
    W = 2 * N
    T = n_s * C
    ii = lax.broadcasted_iota(jnp.int32, (T, T), 0)
    jj = lax.broadcasted_iota(jnp.int32, (T, T), 1)
    shift = C.bit_length() - 1
    tri = (ii >= jj) & (jnp.right_shift(ii, shift) == jnp.right_shift(jj, shift))
    lw_hi = lw.astype(BF16)
    rest = lw - lw_hi.astype(F32)
    lw_mid = rest.astype(BF16)
    lw_lo = (rest - lw_mid.astype(F32)).astype(BF16)
    tri = tri.astype(BF16)
    cl = (jnp.dot(tri, lw_hi, preferred_element_type=F32) + jnp.dot(tri, lw_mid, preferred_element_type=F32)
          + jnp.dot(tri, lw_lo, preferred_element_type=F32))
    gls = [cl[s * C + C - 1:s * C + C, :] for s in range(n_s)]
    gl = jnp.concatenate([jnp.broadcast_to(g, (C, g.shape[-1])) for g in gls], axis=0)
    e_cl = jnp.exp(cl)
    e_ncl = jnp.exp(-cl)
    e_ce = jnp.exp(cl - lw)
    e_g = jnp.exp(gl - cl)
    dgs = [jnp.exp(g) for g in gls]
    kkraw = k * kkw
    kmod = k * (1.0 + (a - 1.0) * kaw)
    rkk = r * kmod * rkw
    rt = r * e_cl
    kt = kmod * e_ncl
    kg = kmod * e_g
    row = lax.broadcasted_iota(jnp.int32, (C, W), 0)
    lane = lax.broadcasted_iota(jnp.int32, (C, W), 1)
    left = lane < N
    tok = jnp.where(left, lane, lane - N)
    strict = row > tok
    incl = row >= tok
    same_head = ((lax.broadcasted_iota(jnp.int32, (W, W), 0) < N)
                 == (lax.broadcasted_iota(jnp.int32, (W, W), 1) < N))

    def seg_sum(x):
        s_l = jnp.sum(jnp.where(left, x, 0.0), axis=-1, keepdims=True)
        s_r = jnp.sum(jnp.where(left, 0.0, x), axis=-1, keepdims=True)
        return jnp.where(left, s_l, s_r)

    def bd(x):
        xb = x.astype(BF16)
        zero = jnp.zeros_like(xb)
        return jnp.concatenate([jnp.where(left, xb, zero), jnp.where(left, zero, xb)], axis=0)

    units = list(S)
    at = {u: (slice(u[0] * C, (u[0] + 1) * C), slice(u[1] * W, (u[1] + 1) * W)) for u in units}
    V2 = {u: v[at[u]] for u in units}
    b2, lhs2, rhs_nt = {}, {}, {}
    for u in units:
        kk2 = kkraw[at[u]]
        kkn = kk2 / jnp.maximum(jnp.sqrt(seg_sum(kk2 * kk2)), 1e-12)
        b2[u] = kkn * a[at[u]]
        lhs2[u] = jnp.concatenate([-kkn * e_ce[at[u]], rt[at[u]]], axis=0).astype(BF16)
        rhs_nt[u] = jnp.concatenate([bd(b2[u] * e_ncl[at[u]]), bd(kt[at[u]])], axis=0)
    P = {u: lax.dot_general(lhs2[u], rhs_nt[u], NT_DIMS, preferred_element_type=F32) for u in units}
    LS = {u: lax.dot_general(lhs2[u], S[u].astype(BF16), NT_DIMS, preferred_element_type=F32) for u in units}
    bd_v = {u: bd(V2[u]) for u in units}
    X = {u: LS[u][:C] + jnp.dot(jnp.where(strict, P[u][:C, W:], 0.0).astype(BF16), bd_v[u],
                                preferred_element_type=F32) for u in units}
    Pk = {u: jnp.where(strict, P[u][:C, :W], 0.0) for u in units}
    span = 1
    while span < C:
        span *= 2
        if span < C:
            Z = {u: jnp.dot(Pk[u].astype(BF16), jnp.concatenate([bd(X[u]), bd(Pk[u])], axis=1),
                            preferred_element_type=F32) for u in units}
            X = {u: X[u] + Z[u][:, :W] for u in units}
            Pk = {u: Z[u][:, W:] for u in units}
        else:
            X = {u: X[u] + jnp.dot(Pk[u].astype(BF16), bd(X[u]), preferred_element_type=F32) for u in units}
    incl2 = jnp.concatenate([incl, incl], axis=1)
    Y = {u: LS[u][C:] + jnp.dot(jnp.where(incl2, P[u][C:], 0.0).astype(BF16),
                                jnp.concatenate([bd(X[u]), bd_v[u]], axis=0), preferred_element_type=F32)
         for u in units}
    s_new = {u: S[u] * dgs[u[0]][:, at[u][1]]
             + jnp.where(same_head,
                         _mm_tn(jnp.concatenate([X[u], V2[u]], axis=0),
                                jnp.concatenate([b2[u] * e_g[at[u]], kg[at[u]]], axis=0)), 0.0)
             for u in units}
    y = {}
    for u in units:
        sl = at[u][1]
        yc = Y[u] - seg_sum(Y[u]) * (1.0 / N)
        yn = yc * lax.rsqrt(seg_sum(yc * yc) * (1.0 / N) + RW_GN_EPS)
        y[u] = yn * lnx_g[:, sl] + lnx_b[:, sl] + seg_sum(rkk[at[u]]) * V2[u]
    return y, s_new


def _rwkv_mix_body(x_ref, lng_ref, prev_ref, mu_ref, w0_ref, a0_ref, kkw_ref, kaw_ref, rk_ref, lg_ref, lb_ref,
                   wr_ref, wk_ref, wv_ref, wo_ref, w1_ref, w2_ref, a1_ref, a2_ref, g1_ref, g2_ref,
                   out_ref, so_ref, s_ref,
                   carry, sbd_scr, r_scr, k_scr, v_scr, lw_scr, a_scr, y_scr, *, n_steps):
    C, N, W = WKV_CHUNK, RW_HEAD, 2 * RW_HEAD
    n_s, T, D = x_ref.shape
    step = pl.program_id(1)

    @pl.when(step == 0)
    def _():
        carry[...] = prev_ref[...]
        sbd_scr[...] = jnp.zeros_like(sbd_scr)

    x = x_ref[...].reshape(n_s * T, D)
    h = _rms(x, lng_ref[...])
    row = lax.broadcasted_iota(jnp.int32, h.shape, 0)
    xprev = pltpu.roll(h, 1, 0)
    for s in range(n_s):
        xprev = jnp.where(row == s * T, carry[s], xprev)
        last = h[s * T + T - 1:s * T + T, :]
        carry[s] = last
        so_ref[s] = last
    r_scr[...], k_scr[...], v_scr[...], lw_scr[...], a_scr[...], g = _rwkv_project(
        h, xprev, mu_ref, w0_ref, a0_ref, wr_ref, wk_ref, wv_ref, w1_ref, w2_ref, a1_ref, a2_ref, g1_ref, g2_ref)
    kkw, kaw, rkw, lnx_g, lnx_b = kkw_ref[...], kaw_ref[...], rk_ref[...], lg_ref[...], lb_ref[...]
    units = [(s, p) for s in range(n_s) for p in range(RW_HEADS // 2)]

    def chunk(c, loop_carry):
        starts = [pl.multiple_of(s * T + c * C, C) for s in range(n_s)]
        rows_of = lambda ref: jnp.concatenate([ref[pl.ds(st, C), :] for st in starts], axis=0)
        S = {u: sbd_scr[u[0], u[1]] for u in units}
        y, s_new = _wkv_chunk(rows_of(r_scr), rows_of(k_scr), rows_of(v_scr), rows_of(lw_scr), rows_of(a_scr),
                              kkw, kaw, rkw, lnx_g, lnx_b, S, n_s)
        for u in units:
            y_scr[pl.ds(starts[u[0]], C), u[1] * W:(u[1] + 1) * W] = y[u]
            sbd_scr[u[0], u[1]] = s_new[u]
        return loop_carry

    lax.fori_loop(0, T // C, chunk, 0)
    out_ref[...] = (x + _mm(y_scr[...] * g, wo_ref[...])).reshape(n_s, T, D)

    @pl.when(step == n_steps - 1)
    def _():
        for s, p in units:
            S = sbd_scr[s, p]
            s_ref[s, 2 * p] = S[:N, :N]
            s_ref[s, 2 * p + 1] = S[N:, N:]


def _rwkv_mix(x, prev, p, *, n_seq, n_s, T):
    M, D = x.shape
    L = M // n_seq
    nl = L // T
    H, N = RW_HEADS, RW_HEAD
    vec = lambda t: t.reshape(1, D)
    const = lambda b, l: (0, 0)
    tok = pl.BlockSpec((n_s, T, D), lambda b, l: (b, l, 0))
    row1 = pl.BlockSpec((n_s, 1, D), lambda b, l: (b, 0, 0))
    st = pl.BlockSpec((n_s, H, N, N), lambda b, l: (b, 0, 0, 0))
    params = [vec(p["ln"]), p["mu"], vec(p["w0"]), vec(p["a0"]), vec(p["kk"]), vec(p["ka"]), vec(p["rk"]),
              vec(p["lnx_g"]), vec(p["lnx_b"])]
    weights = [p["wr"], p["wk"], p["wv"], p["wo"], p["w1"], p["w2"], p["a1"], p["a2"], p["g1"], p["g2"]]
    full = lambda t: _resident(t.shape, const)
    in_specs = [tok, full(params[0]), row1] + [full(t) for t in params[1:]] + [full(t) for t in weights]
    rows = (n_s * T, D)
    pair_state = (n_s, H // 2, 2 * N, 2 * N)
    scratch = [pltpu.VMEM((n_s, 1, D), F32), pltpu.VMEM(pair_state, F32)] + [pltpu.VMEM(rows, F32)] * 6
    streamed = 2 * _nbytes((n_s, T, D), F32) + _nbytes((n_s, H, N, 128), F32)
    resident = sum(_nbytes(t.shape, BF16) for t in weights) + _nbytes(pair_state, F32) + 6 * _nbytes(rows, F32)
    out, so, state = pl.pallas_call(
        functools.partial(_rwkv_mix_body, n_steps=nl), grid=(n_seq // n_s, nl),
        in_specs=in_specs, out_specs=[tok, row1, st],
        out_shape=[jax.ShapeDtypeStruct((n_seq, L, D), F32), jax.ShapeDtypeStruct((n_seq, 1, D), F32),
                   jax.ShapeDtypeStruct((n_seq, H, N, N), F32)],
        scratch_shapes=scratch,
        compiler_params=pltpu.CompilerParams(dimension_semantics=("parallel", "arbitrary"),
                                             vmem_limit_bytes=_vmem_limit(streamed, resident, 12 * _nbytes(rows, F32))),
        name="rwkv_mix",
    )(x.reshape(n_seq, L, D), params[0], prev.reshape(n_seq, 1, D), *params[1:], *weights)
    return out.reshape(M, D), so.reshape(n_seq, D), state


def _wkv_step_body(r_ref, k_ref, v_ref, lw_ref, a_ref, kkw_ref, kaw_ref, rk_ref, lg_ref, lb_ref, s_ref,
                   y_ref, so_ref, y_scr):
    r = r_ref[0]
    k = k_ref[0]
    v = v_ref[0]
    a = a_ref[0]
    d = jnp.exp(lw_ref[0])
    kkh = k * kkw_ref[0]
    nrm = jnp.sqrt(jnp.sum(kkh * kkh, axis=0, keepdims=True))
    kkn = kkh / jnp.maximum(nrm, 1e-12)
    kmod = k * (1.0 + (a - 1.0) * kaw_ref[0])
    bh = kkn * a

    def value_channel(i, carry):
        S = s_ref[0, i]
        sa = -jnp.sum(S * kkn, axis=0, keepdims=True)
        s_new = S * d + sa * bh + v_ref[0, pl.ds(i, 1), :] * kmod
        so_ref[0, i] = s_new
        y_scr[pl.ds(i, 1), :] = jnp.sum(s_new * r, axis=0, keepdims=True)
        return carry

    lax.fori_loop(0, RW_HEAD, value_channel, 0, unroll=8)
    y = y_scr[...]
    mu = jnp.mean(y, axis=0, keepdims=True)
    yc = y - mu
    yn = yc * lax.rsqrt(jnp.mean(yc * yc, axis=0, keepdims=True) + RW_GN_EPS)
    y_ref[0] = yn * lg_ref[0] + lb_ref[0] + jnp.sum(r * kmod * rk_ref[0], axis=0, keepdims=True) * v


def _wkv_step(r, k, v, lw, a, p, state):
    B, D = r.shape
    H, N = RW_HEADS, RW_HEAD
    lanes = lambda t: t.T.reshape(H, N, B)
    par = lambda t: jnp.broadcast_to(t.reshape(H, N, 1), (H, N, B))
    vec = pl.BlockSpec((1, N, B), lambda h: (h, 0, 0))
    st = pl.BlockSpec((1, N, N, B), lambda h: (h, 0, 0, 0))
    blocks = 2 * _nbytes((N, N, B), F32) + 11 * _nbytes((N, B), F32)
    y, s_new = pl.pallas_call(
        _wkv_step_body, grid=(H,),
        in_specs=[vec] * 10 + [st],
        out_specs=[vec, st],
        out_shape=[jax.ShapeDtypeStruct((H, N, B), F32), jax.ShapeDtypeStruct((H, N, N, B), F32)],
        scratch_shapes=[pltpu.VMEM((N, B), F32)],
        compiler_params=pltpu.CompilerParams(dimension_semantics=("parallel",),
                                             vmem_limit_bytes=_vmem_limit(blocks, _nbytes((N, B), F32), 4 << 20)),
        name="wkv_step",
    )(lanes(r), lanes(k), lanes(v), lanes(lw), lanes(a),
      par(p["kk"]), par(p["ka"]), par(p["rk"]), par(p["lnx_g"]), par(p["lnx_b"]),
      jnp.transpose(state, (1, 2, 3, 0)))
    return y.reshape(D, B).T, jnp.transpose(s_new, (3, 0, 1, 2))


def _rwkv_post_body(x_ref, y_ref, g_ref, wo_ref, o_ref):
    o_ref[...] = x_ref[...] + _mm(y_ref[...] * g_ref[...], wo_ref[...])


def _rwkv_post(x, y, g, wo, *, tm):
    M, D = x.shape
    tok = pl.BlockSpec((tm, D), lambda i: (i, 0))
    streamed = 4 * _nbytes((tm, D), F32)
    return pl.pallas_call(
        _rwkv_post_body, grid=(M // tm,),
        in_specs=[tok, tok, tok, _resident(wo.shape, lambda i: (0, 0))],
        out_specs=tok, out_shape=jax.ShapeDtypeStruct((M, D), F32),
        compiler_params=pltpu.CompilerParams(
            dimension_semantics=("parallel",),
            vmem_limit_bytes=_vmem_limit(streamed, _nbytes(wo.shape, BF16), 3 * _nbytes((tm, D), F32))),
        name="rwkv_post",
    )(x, y, g, wo)


def _ret_pre_body(x_ref, lng_ref, cos_ref, sin_ref, wq_ref, wk_ref, wv_ref, wg_ref,
                  q_ref, k_ref, v_ref, gate_ref, h_scr):
    @pl.when(pl.program_id(1) == 0)
    def _():
        h_scr[...] = _rms(x_ref[...], lng_ref[...]).astype(BF16)

    h = h_scr[...]
    cos = cos_ref[...]
    sin = sin_ref[...]
    half = RET_DK // 2

    def rotate(t, out_ref):
        t1, t2 = t[:, :half], t[:, half:]
        out_ref[:, :half] = t1 * cos - t2 * sin
        out_ref[:, half:] = t1 * sin + t2 * cos

    rotate(jnp.dot(h, wq_ref[...], preferred_element_type=F32), q_ref)
    rotate(jnp.dot(h, wk_ref[...], preferred_element_type=F32) * (RET_DK ** -0.5), k_ref)
    v_ref[...] = jnp.dot(h, wv_ref[...], preferred_element_type=F32)
    gz = jnp.dot(h, wg_ref[...], preferred_element_type=F32)
    gate_ref[...] = gz * jax.nn.sigmoid(gz)


def _ret_pre(x, ln_g, cos, sin, p, *, tm):
    M, D = x.shape
    H, DK, DV = RET_HEADS, RET_DK, RET_DV
    n_tab = cos.shape[0] // tm
    tab = pl.BlockSpec((tm, DK // 2), lambda i, h: (i % n_tab, 0))
    col = lambda w: pl.BlockSpec((D, w), lambda i, h: (0, h))
    out = lambda w: pl.BlockSpec((tm, w), lambda i, h: (i, h))
    blocks = (_nbytes((tm, D), F32) + 2 * _nbytes((tm, DK // 2), F32) + 2 * _nbytes((D, DK), BF16)
              + 2 * _nbytes((D, DV), BF16) + 2 * _nbytes((tm, DK), F32) + 2 * _nbytes((tm, DV), F32))
    return pl.pallas_call(
        _ret_pre_body, grid=(M // tm, H),
        in_specs=[pl.BlockSpec((tm, D), lambda i, h: (i, 0)), pl.BlockSpec((1, D), lambda i, h: (0, 0)),
                  tab, tab, col(DK), col(DK), col(DV), col(DV)],
        out_specs=[out(DK), out(DK), out(DV), out(DV)],
        out_shape=[jax.ShapeDtypeStruct((M, H * DK), F32), jax.ShapeDtypeStruct((M, H * DK), F32),
                   jax.ShapeDtypeStruct((M, H * DV), F32), jax.ShapeDtypeStruct((M, H * DV), F32)],
        scratch_shapes=[pltpu.VMEM((tm, D), BF16)],
        compiler_params=pltpu.CompilerParams(
            dimension_semantics=("parallel", "arbitrary"),
            vmem_limit_bytes=_vmem_limit(blocks, _nbytes((tm, D), BF16), 4 * _nbytes((tm, DV), F32))),
        name="ret_pre",
    )(x, ln_g.reshape(1, D), cos, sin, p["wq"], p["wk"], p["wv"], p["wg"])


def _ret_log_gamma():
    return jnp.log1p(-jnp.exp2(-5.0 - jnp.arange(RET_HEADS, dtype=F32)))


def _ret_mix_body(lg_ref, x_ref, lng_ref, cos_ref, sin_ref, wq_ref, wk_ref, wv_ref, wg_ref, wo_ref,
                  out_ref, s_ref, *, n_sub):
    C, DK, DV, H = RET_CHUNK, RET_DK, RET_DV, RET_HEADS
    half = DK // 2

    @pl.when(pl.program_id(1) == 0)
    def _():
        s_ref[...] = jnp.zeros_like(s_ref)

    x = x_ref[...]
    h = _rms(x, lng_ref[...]).astype(BF16)
    cos = cos_ref[...]
    sin = sin_ref[...]
    ii = lax.broadcasted_iota(jnp.int32, (C, C), 0)
    jj = lax.broadcasted_iota(jnp.int32, (C, C), 1)
    diff = jnp.maximum((ii - jj).astype(F32), 0.0)
    idx = lax.broadcasted_iota(jnp.int32, (C, 1), 0).astype(F32)
    rows = lambda cc: slice(cc * C, (cc + 1) * C)

    def rotate(t):
        t1, t2 = t[:, :half], t[:, half:]
        return jnp.concatenate([t1 * cos - t2 * sin, t1 * sin + t2 * cos], axis=-1)

    def project(hd):
        qs, vs = slice(hd * DK, (hd + 1) * DK), slice(hd * DV, (hd + 1) * DV)
        q = rotate(jnp.dot(h, wq_ref[:, qs], preferred_element_type=F32))
        k = rotate(jnp.dot(h, wk_ref[:, qs], preferred_element_type=F32) * (DK ** -0.5))
        v = jnp.dot(h, wv_ref[:, vs], preferred_element_type=F32).astype(BF16)
        gz = jnp.dot(h, wg_ref[:, vs], preferred_element_type=F32)
        return q, k, v, gz * jax.nn.sigmoid(gz)

    def retain(hd, q, k, v):
        lg = lg_ref[hd]
        dmat = jnp.where(ii >= jj, jnp.exp(lg * diff), 0.0)
        q_dec = jnp.exp(lg * (idx + 1.0))
        k_dec = jnp.exp(lg * (C - 1.0 - idx))
        s_dec = jnp.exp(jnp.full((1, DV), lg * C, F32))
        subs = range(n_sub)
        inner = [_mm_nt(q[rows(cc)], k[rows(cc)]) * dmat for cc in subs]
        kv = [lax.dot_general((k[rows(cc)] * k_dec).astype(BF16), v[rows(cc)], TN_DIMS,
                              preferred_element_type=F32) for cc in subs]
        S = s_ref[0, hd]
        outs = []
        for cc in subs:
            outs.append(jnp.dot(inner[cc].astype(BF16), v[rows(cc)], preferred_element_type=F32)
                        + _mm(q[rows(cc)] * q_dec, S))
            S = S * s_dec + kv[cc]
        s_ref[0, hd] = S
        return jnp.concatenate(outs, axis=0)

    acc = x
    proj = project(0)
    for hd in range(H):
        nxt = project(hd + 1) if hd + 1 < H else None
        q, k, v, gate = proj
        o = retain(hd, q, k, v)
        acc = acc + _mm(gate * _head_norm(o, RET_GN_EPS), wo_ref[hd * DV:(hd + 1) * DV, :])
        proj = nxt
    out_ref[...] = acc


def _ret_mix(x, ln_g, cos, sin, p, *, n_seq, n_sub):
    M, D = x.shape
    H, DK, DV, C = RET_HEADS, RET_DK, RET_DV, RET_CHUNK
    T = C * n_sub
    nl = M // n_seq // T
    const = lambda b, l: (0, 0)
    tok = pl.BlockSpec((T, D), lambda b, l: (b * nl + l, 0))
    tab = pl.BlockSpec((T, DK // 2), lambda b, l: (l, 0))
    st = pl.BlockSpec((1, H, DK, DV), lambda b, l: (b, 0, 0, 0))
    weights = [p["wq"], p["wk"], p["wv"], p["wg"], p["wo"]]
    streamed = 2 * _nbytes((T, D), F32) + 2 * _nbytes((T, DK // 2), F32) + _nbytes((H, DK, DV), F32)
    resident = sum(_nbytes(w.shape, BF16) for w in weights)
    temps = 8 * _nbytes((T, DV), F32) + 4 * _nbytes((DK, DV), F32)
    return pl.pallas_call(
        functools.partial(_ret_mix_body, n_sub=n_sub), grid=(n_seq, nl),
        in_specs=[pl.BlockSpec(memory_space=pltpu.SMEM), tok, _resident((1, D), const), tab, tab]
                 + [_resident(w.shape, const) for w in weights],
        out_specs=[tok, st],
        out_shape=[jax.ShapeDtypeStruct((M, D), F32), jax.ShapeDtypeStruct((n_seq, H, DK, DV), F32)],
        compiler_params=pltpu.CompilerParams(dimension_semantics=("parallel", "arbitrary"),
                                             vmem_limit_bytes=_vmem_limit(streamed, resident, temps)),
        name="ret_mix",
    )(_ret_log_gamma(), x, ln_g.reshape(1, D), cos, sin, *weights)


def _ret_side(lg_ref, q_ref, k_ref, v_ref, s_ref, o_ref, so_ref):
    bt, H, DK = q_ref.shape
    q_cols = q_ref[...].reshape(bt * H, DK).T
    k_cols = k_ref[...].reshape(bt * H, DK).T

    def unit(b, hd):
        gam = jnp.exp(jnp.full((1, RET_DV), lg_ref[hd], F32))
        q = q_cols[:, b * H + hd:b * H + hd + 1]
        k = k_cols[:, b * H + hd:b * H + hd + 1]
        v = v_ref[b, hd:hd + 1, :]
        S = s_ref[b, hd]
        so_ref[b, hd] = S * gam + k * v
        o_ref[b, hd:hd + 1, :] = (gam * jnp.sum(q * S, axis=0, keepdims=True)
                                  + jnp.sum(q * k, axis=0, keepdims=True) * v)

    return [functools.partial(unit, b, hd) for b in range(bt) for hd in range(H)]


def _ret_post_body(x_ref, o_ref, gate_ref, wo_ref, out_ref):
    acc = x_ref[...]
    for hd in range(RET_HEADS):
        sl = slice(hd * RET_DV, (hd + 1) * RET_DV)
        acc = acc + _mm(gate_ref[:, sl] * _head_norm(o_ref[:, sl], RET_GN_EPS), wo_ref[sl, :])
    out_ref[...] = acc


def _ret_post(x, o, gate, wo, *, tm):
    M, D = x.shape
    W = o.shape[1]
    tok = lambda w: pl.BlockSpec((tm, w), lambda i: (i, 0))
    streamed = 2 * _nbytes((tm, D), F32) + 2 * _nbytes((tm, W), F32)
    return pl.pallas_call(
        _ret_post_body, grid=(M // tm,),
        in_specs=[tok(D), tok(W), tok(W), _resident(wo.shape, lambda i: (0, 0))],
        out_specs=tok(D), out_shape=jax.ShapeDtypeStruct((M, D), F32),
        compiler_params=pltpu.CompilerParams(
            dimension_semantics=("parallel",),
            vmem_limit_bytes=_vmem_limit(streamed, _nbytes(wo.shape, BF16), 3 * _nbytes((tm, W), F32))),
        name="ret_post",
    )(x, o, gate, wo)


def _rope_tables(pos):
    half = RET_DK // 2
    inv = ROPE_BASE ** (-jnp.arange(half, dtype=F32) / half)
    ang = pos.astype(F32)[:, None] * inv[None, :]
    return jnp.cos(ang), jnp.sin(ang)


def _rwkv_params(w, i):
    j = i // N_MIXERS
    p = {n: w["rw_" + n][j] for n in ("mu", "w0", "w1", "w2", "a0", "a1", "a2", "g1", "g2", "kk", "ka", "rk",
                                      "wr", "wk", "wv", "wo", "lnx_g", "lnx_b")}
    p["ln"] = w["ln_mix"][i]
    return p


def _ret_params(w, i):
    return {n: w["ret_" + n][i // N_MIXERS] for n in ("wq", "wk", "wv", "wg", "wo")}


def _ffn_of(w, which, i, tiles, tm="ffn_m"):
    ln_final = w["ln_final"] if (which == 2 and i == DEPTH - 1) else None
    return functools.partial(_ffn, ln_g=w[f"ln_ffn{which}"][i], wg=w[f"ff{which}_wg"], wu=w[f"ff{which}_wu"],
                             wd=w[f"ff{which}_wd"], ln_final=ln_final, layer=i, tm=tiles[tm],
                             tf=tiles["ffn_f"])


def _cast_late_weights(x, w, tiles):
    flat = lambda t: t.reshape(-1, t.shape[-1])
    todo = [(n, None, flat(w[n])) for n in LATE_STACKS] + [(n, i, t) for n in LATE_MIXER for i, t in enumerate(w[n])]
    x, _, _, done = _ffn_of(w, 1, 0, tiles, tm="ffn_m_side")(x, cast=[t for _, _, t in todo])
    for (name, i, _), t in zip(todo, done):
        if i is None:
            w[name] = t.reshape(w[name].shape)
        else:
            w[name] = [*w[name][:i], t, *w[name][i + 1:]]
    return x


def _prefill(x, n_seq, w, tiles, start_side):
    M, D = x.shape
    cos, sin = _rope_tables(jnp.arange(M // n_seq, dtype=F32))
    new_shift, new_wkv, new_ret, side_o = [], [], [], []
    x = _cast_late_weights(x, w, tiles)
    side = dict(start_side(w), first=0, buf=None)
    n_side = side["state"].shape[0]

    def ffn(x, which, i):
        if side["first"] == n_side:
            return _ffn_of(w, which, i, tiles)(x)
        x, o, side["buf"], _ = _ffn_of(w, which, i, tiles, tm="ffn_m_side")(x, side=side)
        side_o.append(o)
        side["first"] += o.shape[0]
        return x

    for i in range(DEPTH):
        if i > 0:
            x = ffn(x, 1, i)
        if i % N_MIXERS == 0:
            x, sh, st = _rwkv_mix(x, jnp.zeros((n_seq, D), F32), _rwkv_params(w, i), n_seq=n_seq,
                                  n_s=tiles["wkv_seqs"], T=tiles["wkv_tokens"])
            new_shift.append(sh)
            new_wkv.append(st)
        else:
            x, st = _ret_mix(x, w["ln_mix"][i], cos, sin, _ret_params(w, i), n_seq=n_seq, n_sub=tiles["ret_sub"])
            new_ret.append(st)
        x = ffn(x, 2, i)
    assert side["first"] == n_side, "the prompt's FFN calls must carry the whole decode retention state"
    return (x, jnp.stack(new_shift), jnp.stack(new_wkv), jnp.stack(new_ret),
            jnp.concatenate(side_o, axis=0), side["buf"])


def _decode(x, pos, shift0, wkv0, w, tiles):
    B, D = x.shape
    cos, sin = (jnp.tile(t, (B, 1)) for t in _rope_tables(pos))
    new_shift, new_wkv = [], []
    for i in range(DEPTH):
        x = _ffn_of(w, 1, i, tiles)(x)
        if i % N_MIXERS == 0:
            p = _rwkv_params(w, i)
            r, k, v, lw, a, g, sh = _rwkv_pre(x, shift0[i // N_MIXERS], p, tile=tiles["rwkv_pre"])
            y, st = _wkv_step(r, k, v, lw, a, p, wkv0[i // N_MIXERS])
            x = _rwkv_post(x, y, g, p["wo"], tm=tiles["post"])
            new_shift.append(sh)
            new_wkv.append(st)
        else:
            p = _ret_params(w, i)
            q, k, v, gate = _ret_pre(x, w["ln_mix"][i], cos, sin, p, tm=tiles["ret_pre"])
            o = yield q, k, v
            x = _ret_post(x, o, gate, p["wo"], tm=tiles["post"])
        x = _ffn_of(w, 2, i, tiles)(x)
    return x, jnp.stack(new_shift), jnp.stack(new_wkv)


PROMPT_TILES = dict(ffn_m=1024, ffn_m_side=512, ffn_f=256, ret_sub=4, wkv_seqs=2, wkv_tokens=256,
                    ret_side_seqs=2)
SAMPLE_TILES = dict(ffn_m=128, ffn_f=256, rwkv_pre=128, post=128, ret_pre=128)

EARLY_STACKS = ("ff1_wg", "ff1_wu", "ff1_wd")
EARLY_MIXER = ("rw_w1", "rw_w2", "rw_a1", "rw_a2", "rw_g1", "rw_g2")
LATE_STACKS = ("ff2_wg", "ff2_wu", "ff2_wd")
LATE_MIXER = ("rw_wr", "rw_wk", "rw_wv", "rw_wo", "ret_wq", "ret_wk", "ret_wv", "ret_wg", "ret_wo")


def kernel(x_prompt, x_sample, state_rwkv_shift, state_rwkv_wkv, state_ret, ln_ffn1, ff1_wg, ff1_wu, ff1_wd, ln_mix, ln_ffn2, ff2_wg, ff2_wu, ff2_wd, ln_final, rw_mu, rw_w0, rw_w1, rw_w2, rw_a0, rw_a1, rw_a2, rw_g1, rw_g2, rw_kk, rw_ka, rw_rk, rw_wr, rw_wk, rw_wv, rw_wo, rw_lnx_g, rw_lnx_b, ret_wq, ret_wk, ret_wv, ret_wg, ret_wo):
    w = dict(ln_ffn1=ln_ffn1, ff1_wg=ff1_wg, ff1_wu=ff1_wu, ff1_wd=ff1_wd, ln_mix=ln_mix,
             ln_ffn2=ln_ffn2, ff2_wg=ff2_wg, ff2_wu=ff2_wu, ff2_wd=ff2_wd, ln_final=ln_final,
             rw_mu=rw_mu, rw_w0=rw_w0, rw_w1=rw_w1, rw_w2=rw_w2, rw_a0=rw_a0, rw_a1=rw_a1,
             rw_a2=rw_a2, rw_g1=rw_g1, rw_g2=rw_g2, rw_kk=rw_kk, rw_ka=rw_ka, rw_rk=rw_rk,
             rw_wr=rw_wr, rw_wk=rw_wk, rw_wv=rw_wv, rw_wo=rw_wo, rw_lnx_g=rw_lnx_g,
             rw_lnx_b=rw_lnx_b, ret_wq=ret_wq, ret_wk=ret_wk, ret_wv=ret_wv, ret_wg=ret_wg,
             ret_wo=ret_wo)
    for name in EARLY_STACKS:
        w[name] = w[name].astype(BF16)
    for name in EARLY_MIXER:
        w[name] = [t.astype(BF16) for t in w[name]]
    for name in LATE_MIXER:
        w[name] = list(w[name])
    B, L, D = x_prompt.shape
    Bs, Ls, _ = x_sample.shape
    assert Ls == 1 and state_ret.shape[0] == 1, "one decode token and one retention layer are supported"
    H, DK, DV = RET_HEADS, RET_DK, RET_DV
    decode = None

    def start_decode(w):
        nonlocal decode
        decode = _decode(x_sample.reshape(Bs, D), PAST_LEN + jnp.arange(Ls, dtype=F32), state_rwkv_shift,
                         state_rwkv_wkv, w, SAMPLE_TILES)
        q, k, v = next(decode)
        return dict(q=q.reshape(Bs, H, DK), k=k.reshape(Bs, H, DK), v=v.reshape(Bs, H, DV), state=state_ret[0],
                    seqs=PROMPT_TILES["ret_side_seqs"])

    y_p, p_shift, p_wkv, p_ret, o, s_ret = _prefill(x_prompt.reshape(B * L, D), B, w, PROMPT_TILES, start_decode)
    try:
        decode.send(o.reshape(Bs, H * DV))
        raise AssertionError("the decode group has a single retention layer")
    except StopIteration as done:
        y_s, s_shift, s_wkv = done.value
    return (y_p.reshape(B, L, D), y_s.reshape(Bs, Ls, D), p_shift, p_wkv, p_ret, s_shift, s_wkv, s_ret[None])
```

```python
import functools
import math

import numpy as np
import jax
import jax.numpy as jnp
from jax import lax
from jax.experimental import pallas as pl
from jax.experimental.pallas import tpu as pltpu

F32 = jnp.float32
BF16 = jnp.bfloat16

DEPTH = 2
N_MIXERS = 2
RW_HEADS = 16
RW_HEAD = 64
RW_GN_EPS = 64e-5
RET_HEADS = 4
RET_DK = 256
RET_DV = 512
RET_CHUNK = 128
RET_GN_EPS = 1e-6
ROPE_BASE = 10000.0
FFN_RES = 0.5
RMS_EPS = 1e-6
PAST_LEN = 16384

WKV_CHUNK = 64
V7X_VMEM_CAP_BYTES = 60000 * 1024
SUBLANES = 8
BF16_SUBLANES = 16

NT_DIMS = (((1,), (1,)), ((), ()))
TN_DIMS = (((0,), (0,)), ((), ()))


def _vmem_limit(block_bytes, scratch_bytes=0, temp_bytes=0):
    return int(min(V7X_VMEM_CAP_BYTES, 2 * block_bytes + scratch_bytes + temp_bytes + (4 << 20)))


def _nbytes(shape, dtype):
    return int(np.prod(shape)) * jnp.dtype(dtype).itemsize


def _resident(shape, index_map):
    return pl.BlockSpec(shape, index_map, pipeline_mode=pl.Buffered(1))


def _rms(x, g):
    return x * lax.rsqrt(jnp.mean(x * x, axis=-1, keepdims=True) + RMS_EPS) * g


def _head_norm(y, eps):
    mu = jnp.mean(y, axis=-1, keepdims=True)
    yc = y - mu
    return yc * lax.rsqrt(jnp.mean(yc * yc, axis=-1, keepdims=True) + eps)


def _mm(a, b):
    return jnp.dot(a.astype(BF16), b.astype(BF16), preferred_element_type=F32)


def _mm_nt(a, b):
    return lax.dot_general(a.astype(BF16), b.astype(BF16), NT_DIMS, preferred_element_type=F32)


def _mm_tn(a, b):
    return lax.dot_general(a.astype(BF16), b.astype(BF16), TN_DIMS, preferred_element_type=F32)


def _ffn_body(*refs, ff_tile, final, side, n_cast):
    refs = list(refs)
    x_ref, g_ref, wg_ref, wu_ref, wd_ref = refs[:5]
    del refs[:5]
    gf_ref = refs.pop(0) if final else None
    if side:
        side_in = refs[:5]
        del refs[:5]
    cast_in = refs[:n_cast]
    del refs[:n_cast]
    o_ref = refs.pop(0)
    cast_out = refs[len(refs) - n_cast:]
    del refs[len(refs) - n_cast:]
    x = x_ref[...]
    h = _rms(x, g_ref[...]).astype(BF16)
    n_tiles = wg_ref.shape[1] // ff_tile

    def activation(j):
        sl = slice(j * ff_tile, (j + 1) * ff_tile)
        gate = jnp.dot(h, wg_ref[:, sl], preferred_element_type=F32)
        up = jnp.dot(h, wu_ref[:, sl], preferred_element_type=F32)
        return (gate * jax.nn.sigmoid(gate) * up).astype(BF16)

    side_units = _ret_side(*side_in, *refs) if side else []
    per_tile = -(-len(side_units) // n_tiles)
    act = activation(0)
    acc = None
    for j in range(n_tiles):
        nxt = activation(j + 1) if j + 1 < n_tiles else None
        part = jnp.dot(act, wd_ref[j * ff_tile:(j + 1) * ff_tile, :], preferred_element_type=F32)
        acc = part if acc is None else acc + part
        act = nxt
        for unit in side_units[j * per_tile:(j + 1) * per_tile]:
            unit()
    y = x + FFN_RES * acc
    if final:
        y = _rms(y, gf_ref[...])
    o_ref[...] = y
    for src, dst in zip(cast_in, cast_out):
        dst[...] = src[...].astype(BF16)


def _ffn(x, ln_g, wg, wu, wd, ln_final=None, *, layer, tm, tf, side=None, cast=()):
    M, D = x.shape
    FF = wg.shape[2]
    final = ln_final is not None
    steps = M // tm
    row = lambda i: (i, 0)
    const = lambda i: (0, 0)
    of_layer = lambda i: (layer, 0, 0)
    in_specs = [pl.BlockSpec((tm, D), row), _resident((1, D), const), _resident((None, D, FF), of_layer),
                _resident((None, D, FF), of_layer), _resident((None, FF, D), of_layer)]
    args = [x, ln_g.reshape(1, D), wg, wu, wd]
    if final:
        in_specs.append(_resident((1, D), const))
        args.append(ln_final.reshape(1, D))
    out_specs = [pl.BlockSpec((tm, D), row)]
    out_shape = [jax.ShapeDtypeStruct((M, D), F32)]
    streamed = 2 * _nbytes((tm, D), F32)
    if side is not None:
        B, H, DK, DV = side["state"].shape
        bt = B // steps
        assert bt * steps == B, "every grid step takes the same number of sequences"
        taken = lambda i: (i, 0, 0)
        st = pl.BlockSpec((bt, H, DK, DV), lambda i: (i, 0, 0, 0))
        in_specs += [pl.BlockSpec(memory_space=pltpu.SMEM), pl.BlockSpec((bt, H, DK), taken),
                     pl.BlockSpec((bt, H, DK), taken), pl.BlockSpec((bt, H, DV), taken), st]
        args += [_ret_log_gamma(), side["q"], side["k"], side["v"], side["state"]]
        out_specs += [pl.BlockSpec((bt, H, DV), taken), st]
        out_shape += [jax.ShapeDtypeStruct((B, H, DV), F32), jax.ShapeDtypeStruct((B, H, DK, DV), F32)]
        streamed += 2 * _nbytes((bt, H, DK, DV), F32) + 4 * _nbytes((bt, SUBLANES, DV), F32)
    for t, first_row, n_rows in cast:
        span = next(s for s in (1, 2, 4) if (n_rows * s) % (steps * BF16_SUBLANES) == 0)
        rb = n_rows * span // steps
        cols = t.shape[1]
        assert first_row % rb == 0
        in_specs.append(pl.BlockSpec((rb, cols), lambda i, first=first_row // rb, span=span: (first + i // span, 0)))
        args.append(t)
        out_specs.append(pl.BlockSpec((rb, cols), lambda i, span=span: (i // span, 0)))
        out_shape.append(jax.ShapeDtypeStruct((n_rows, cols), BF16))
        streamed += _nbytes((rb, cols), F32) + _nbytes((rb, cols), BF16)
    resident = 3 * _nbytes((D, FF), BF16)
    temps = 6 * _nbytes((tm, tf), F32) + 3 * _nbytes((tm, D), F32)
    out = pl.pallas_call(
        functools.partial(_ffn_body, ff_tile=tf, final=final, side=side is not None, n_cast=len(cast)),
        grid=(steps,), in_specs=in_specs, out_specs=out_specs, out_shape=out_shape,
        compiler_params=pltpu.CompilerParams(dimension_semantics=("arbitrary",) if cast else ("parallel",),
                                             vmem_limit_bytes=_vmem_limit(streamed, resident, temps)),
        name="ffn_final" if final else "ffn",
    )(*args)
    if side is None and not cast:
        return out[0]
    o, state = out[1:3] if side is not None else (None, None)
    return out[0], o, state, tuple(out[len(out) - len(cast):])


def _rwkv_project(h, xprev, mu_ref, w0_ref, a0_ref, wr_ref, wk_ref, wv_ref, w1_ref, w2_ref, a1_ref, a2_ref,
                  g1_ref, g2_ref):
    xx = xprev - h
    xr, xw, xk, xv, xa, xg = (h + xx * mu_ref[i:i + 1, :] for i in range(6))
    r = _mm(xr, wr_ref[...])
    k = _mm(xk, wk_ref[...])
    v = _mm(xv, wv_ref[...])
    z = w0_ref[...] + _mm(jnp.tanh(_mm(xw, w1_ref[...])), w2_ref[...])
    lw = -math.exp(-0.5) * jax.nn.sigmoid(z)
    a = jax.nn.sigmoid(a0_ref[...] + _mm(_mm(xa, a1_ref[...]), a2_ref[...]))
    g = _mm(jax.nn.sigmoid(_mm(xg, g1_ref[...])), g2_ref[...])
    return r, k, v, lw, a, g


def _rwkv_pre_body(x_ref, lng_ref, prev_ref, *refs):
    weight_refs, out_refs = refs[:12], refs[12:]
    h = _rms(x_ref[...], lng_ref[...])
    for ref, val in zip(out_refs, _rwkv_project(h, prev_ref[...], *weight_refs) + (h,)):
        ref[...] = val


def _rwkv_pre(x, prev, p, *, tile):
    M, D = x.shape
    vec = lambda a: a.reshape(1, D)
    row = lambda i: (i, 0)
    tok = pl.BlockSpec((tile, D), row)
    full = lambda a: _resident(a.shape, lambda i: (0, 0))
    params = [p["mu"], vec(p["w0"]), vec(p["a0"])]
    weights = [p["wr"], p["wk"], p["wv"], p["w1"], p["w2"], p["a1"], p["a2"], p["g1"], p["g2"]]
    streamed = 9 * _nbytes((tile, D), F32)
    resident = sum(_nbytes(w.shape, BF16) for w in weights)
    return pl.pallas_call(
        _rwkv_pre_body, grid=(M // tile,),
        in_specs=[tok, full(vec(p["ln"])), tok] + [full(t) for t in params + weights],
        out_specs=[tok] * 7,
        out_shape=[jax.ShapeDtypeStruct((M, D), F32)] * 7,
        compiler_params=pltpu.CompilerParams(
            dimension_semantics=("parallel",),
            vmem_limit_bytes=_vmem_limit(streamed, resident, 10 * _nbytes((tile, D), F32))),
        name="rwkv_pre",
    )(x, vec(p["ln"]), prev, *params, *weights)


def _wkv_chunk(r, k, v, lw, a, kkw, kaw, rkw, lnx_g, lnx_b, S, n_s):
    C = WKV_CHUNK
    N = RW_HEAD
    W = 2 * N
    T = n_s * C
    ii = lax.broadcasted_iota(jnp.int32, (T, T), 0)
    jj = lax.broadcasted_iota(jnp.int32, (T, T), 1)
    shift = C.bit_length() - 1
    tri = (ii >= jj) & (jnp.right_shift(ii, shift) == jnp.right_shift(jj, shift))
    lw_hi = lw.astype(BF16)
    rest = lw - lw_hi.astype(F32)
    lw_mid = rest.astype(BF16)
    lw_lo = (rest - lw_mid.astype(F32)).astype(BF16)
    tri = tri.astype(BF16)
    cl = (jnp.dot(tri, lw_hi, preferred_element_type=F32) + jnp.dot(tri, lw_mid, preferred_element_type=F32)
          + jnp.dot(tri, lw_lo, preferred_element_type=F32))
    gls = [cl[s * C + C - 1:s * C + C, :] for s in range(n_s)]
    gl = jnp.concatenate([jnp.broadcast_to(g, (C, g.shape[-1])) for g in gls], axis=0)
    e_cl = jnp.exp(cl)
    e_ncl = jnp.exp(-cl)
    e_ce = jnp.exp(cl - lw)
    e_g = jnp.exp(gl - cl)
    dgs = [jnp.exp(g) for g in gls]
    kkraw = k * kkw
    kmod = k * (1.0 + (a - 1.0) * kaw)
    rkk = r * kmod * rkw
    rt = r * e_cl
    kt = kmod * e_ncl
    kg = kmod * e_g
    row = lax.broadcasted_iota(jnp.int32, (C, W), 0)
    lane = lax.broadcasted_iota(jnp.int32, (C, W), 1)
    left = lane < N
    tok = jnp.where(left, lane, lane - N)
    strict = row > tok
    incl = row >= tok
    same_head = ((lax.broadcasted_iota(jnp.int32, (W, W), 0) < N)
                 == (lax.broadcasted_iota(jnp.int32, (W, W), 1) < N))

    def seg_sum(x):
        s_l = jnp.sum(jnp.where(left, x, 0.0), axis=-1, keepdims=True)
        s_r = jnp.sum(jnp.where(left, 0.0, x), axis=-1, keepdims=True)
        return jnp.where(left, s_l, s_r)

    def bd(x):
        xb = x.astype(BF16)
        zero = jnp.zeros_like(xb)
        return jnp.concatenate([jnp.where(left, xb, zero), jnp.where(left, zero, xb)], axis=0)

    units = list(S)
    at = {u: (slice(u[0] * C, (u[0] + 1) * C), slice(u[1] * W, (u[1] + 1) * W)) for u in units}
    V2 = {u: v[at[u]] for u in units}
    b2, lhs2, rhs_nt = {}, {}, {}
    for u in units:
        kk2 = kkraw[at[u]]
        kkn = kk2 / jnp.maximum(jnp.sqrt(seg_sum(kk2 * kk2)), 1e-12)
        b2[u] = kkn * a[at[u]]
        lhs2[u] = jnp.concatenate([-kkn * e_ce[at[u]], rt[at[u]]], axis=0).astype(BF16)
        rhs_nt[u] = jnp.concatenate([bd(b2[u] * e_ncl[at[u]]), bd(kt[at[u]])], axis=0)
    P = {u: lax.dot_general(lhs2[u], rhs_nt[u], NT_DIMS, preferred_element_type=F32) for u in units}
    LS = {u: lax.dot_general(lhs2[u], S[u].astype(BF16), NT_DIMS, preferred_element_type=F32) for u in units}
    bd_v = {u: bd(V2[u]) for u in units}
    X = {u: LS[u][:C] + jnp.dot(jnp.where(strict, P[u][:C, W:], 0.0).astype(BF16), bd_v[u],
                                preferred_element_type=F32) for u in units}
    Pk = {u: jnp.where(strict, P[u][:C, :W], 0.0) for u in units}
    span = 1
    while span < C:
        span *= 2
        if span < C:
            Z = {u: jnp.dot(Pk[u].astype(BF16), jnp.concatenate([bd(X[u]), bd(Pk[u])], axis=1),
                            preferred_element_type=F32) for u in units}
            X = {u: X[u] + Z[u][:, :W] for u in units}
            Pk = {u: Z[u][:, W:] for u in units}
        else:
            X = {u: X[u] + jnp.dot(Pk[u].astype(BF16), bd(X[u]), preferred_element_type=F32) for u in units}
    incl2 = jnp.concatenate([incl, incl], axis=1)
    Y = {u: LS[u][C:] + jnp.dot(jnp.where(incl2, P[u][C:], 0.0).astype(BF16),
                                jnp.concatenate([bd(X[u]), bd_v[u]], axis=0), preferred_element_type=F32)
         for u in units}
    s_new = {u: S[u] * dgs[u[0]][:, at[u][1]]
             + jnp.where(same_head,
                         _mm_tn(jnp.concatenate([X[u], V2[u]], axis=0),
                                jnp.concatenate([b2[u] * e_g[at[u]], kg[at[u]]], axis=0)), 0.0)
             for u in units}
    y = {}
    for u in units:
        sl = at[u][1]
        yc = Y[u] - seg_sum(Y[u]) * (1.0 / N)
        yn = yc * lax.rsqrt(seg_sum(yc * yc) * (1.0 / N) + RW_GN_EPS)
        y[u] = yn * lnx_g[:, sl] + lnx_b[:, sl] + seg_sum(rkk[at[u]]) * V2[u]
    return y, s_new


def _rwkv_mix_body(x_ref, lng_ref, prev_ref, mu_ref, w0_ref, a0_ref, kkw_ref, kaw_ref, rk_ref, lg_ref, lb_ref,
                   wr_ref, wk_ref, wv_ref, wo_ref, w1_ref, w2_ref, a1_ref, a2_ref, g1_ref, g2_ref,
                   out_ref, so_ref, s_ref,
                   carry, sbd_scr, r_scr, k_scr, v_scr, lw_scr, a_scr, y_scr, *, n_steps):
    C, N, W = WKV_CHUNK, RW_HEAD, 2 * RW_HEAD
    n_s, T, D = x_ref.shape
    step = pl.program_id(1)

    @pl.when(step == 0)
    def _():
        carry[...] = prev_ref[...]
        sbd_scr[...] = jnp.zeros_like(sbd_scr)

    x = x_ref[...].reshape(n_s * T, D)
    h = _rms(x, lng_ref[...])
    row = lax.broadcasted_iota(jnp.int32, h.shape, 0)
    xprev = pltpu.roll(h, 1, 0)
    for s in range(n_s):
        xprev = jnp.where(row == s * T, carry[s], xprev)
        last = h[s * T + T - 1:s * T + T, :]
        carry[s] = last
        so_ref[s] = last
    r_scr[...], k_scr[...], v_scr[...], lw_scr[...], a_scr[...], g = _rwkv_project(
        h, xprev, mu_ref, w0_ref, a0_ref, wr_ref, wk_ref, wv_ref, w1_ref, w2_ref, a1_ref, a2_ref, g1_ref, g2_ref)
    kkw, kaw, rkw, lnx_g, lnx_b = kkw_ref[...], kaw_ref[...], rk_ref[...], lg_ref[...], lb_ref[...]
    units = [(s, p) for s in range(n_s) for p in range(RW_HEADS // 2)]

    def chunk(c, loop_carry):
        starts = [pl.multiple_of(s * T + c * C, C) for s in range(n_s)]
        rows_of = lambda ref: jnp.concatenate([ref[pl.ds(st, C), :] for st in starts], axis=0)
        S = {u: sbd_scr[u[0], u[1]] for u in units}
        y, s_new = _wkv_chunk(rows_of(r_scr), rows_of(k_scr), rows_of(v_scr), rows_of(lw_scr), rows_of(a_scr),
                              kkw, kaw, rkw, lnx_g, lnx_b, S, n_s)
        for u in units:
            y_scr[pl.ds(starts[u[0]], C), u[1] * W:(u[1] + 1) * W] = y[u]
            sbd_scr[u[0], u[1]] = s_new[u]
        return loop_carry

    lax.fori_loop(0, T // C, chunk, 0)
    out_ref[...] = (x + _mm(y_scr[...] * g, wo_ref[...])).reshape(n_s, T, D)

    @pl.when(step == n_steps - 1)
    def _():
        for s, p in units:
            S = sbd_scr[s, p]
            s_ref[s, 2 * p] = S[:N, :N]
            s_ref[s, 2 * p + 1] = S[N:, N:]


def _rwkv_mix(x, prev, p, *, n_seq, n_s, T):
    M, D = x.shape
    L = M // n_seq
    nl = L // T
    H, N = RW_HEADS, RW_HEAD
    vec = lambda t: t.reshape(1, D)
    const = lambda b, l: (0, 0)
    tok = pl.BlockSpec((n_s, T, D), lambda b, l: (b, l, 0))
    row1 = pl.BlockSpec((n_s, 1, D), lambda b, l: (b, 0, 0))
    st = pl.BlockSpec((n_s, H, N, N), lambda b, l: (b, 0, 0, 0))
    params = [vec(p["ln"]), p["mu"], vec(p["w0"]), vec(p["a0"]), vec(p["kk"]), vec(p["ka"]), vec(p["rk"]),
              vec(p["lnx_g"]), vec(p["lnx_b"])]
    weights = [p["wr"], p["wk"], p["wv"], p["wo"], p["w1"], p["w2"], p["a1"], p["a2"], p["g1"], p["g2"]]
    full = lambda t: _resident(t.shape, const)
    in_specs = [tok, full(params[0]), row1] + [full(t) for t in params[1:]] + [full(t) for t in weights]
    rows = (n_s * T, D)
    pair_state = (n_s, H // 2, 2 * N, 2 * N)
    scratch = [pltpu.VMEM((n_s, 1, D), F32), pltpu.VMEM(pair_state, F32)] + [pltpu.VMEM(rows, F32)] * 6
    streamed = 2 * _nbytes((n_s, T, D), F32) + _nbytes((n_s, H, N, 128), F32)
    resident = sum(_nbytes(t.shape, BF16) for t in weights) + _nbytes(pair_state, F32) + 6 * _nbytes(rows, F32)
    out, so, state = pl.pallas_call(
        functools.partial(_rwkv_mix_body, n_steps=nl), grid=(n_seq // n_s, nl),
        in_specs=in_specs, out_specs=[tok, row1, st],
        out_shape=[jax.ShapeDtypeStruct((n_seq, L, D), F32), jax.ShapeDtypeStruct((n_seq, 1, D), F32),
                   jax.ShapeDtypeStruct((n_seq, H, N, N), F32)],
        scratch_shapes=scratch,
        compiler_params=pltpu.CompilerParams(dimension_semantics=("parallel", "arbitrary"),
                                             vmem_limit_bytes=_vmem_limit(streamed, resident, 12 * _nbytes(rows, F32))),
        name="rwkv_mix",
    )(x.reshape(n_seq, L, D), params[0], prev.reshape(n_seq, 1, D), *params[1:], *weights)
    return out.reshape(M, D), so.reshape(n_seq, D), state


def _wkv_step_body(r_ref, k_ref, v_ref, lw_ref, a_ref, kkw_ref, kaw_ref, rk_ref, lg_ref, lb_ref, s_ref,
                   y_ref, so_ref, y_scr):
    r = r_ref[0]
    k = k_ref[0]
    v = v_ref[0]
    a = a_ref[0]
    d = jnp.exp(lw_ref[0])
    kkh = k * kkw_ref[0]
    nrm = jnp.sqrt(jnp.sum(kkh * kkh, axis=0, keepdims=True))
    kkn = kkh / jnp.maximum(nrm, 1e-12)
    kmod = k * (1.0 + (a - 1.0) * kaw_ref[0])
    bh = kkn * a

    def value_channel(i, carry):
        S = s_ref[0, i]
        sa = -jnp.sum(S * kkn, axis=0, keepdims=True)
        s_new = S * d + sa * bh + v_ref[0, pl.ds(i, 1), :] * kmod
        so_ref[0, i] = s_new
        y_scr[pl.ds(i, 1), :] = jnp.sum(s_new * r, axis=0, keepdims=True)
        return carry

    lax.fori_loop(0, RW_HEAD, value_channel, 0, unroll=8)
    y = y_scr[...]
    mu = jnp.mean(y, axis=0, keepdims=True)
    yc = y - mu
    yn = yc * lax.rsqrt(jnp.mean(yc * yc, axis=0, keepdims=True) + RW_GN_EPS)
    y_ref[0] = yn * lg_ref[0] + lb_ref[0] + jnp.sum(r * kmod * rk_ref[0], axis=0, keepdims=True) * v


def _wkv_step(r, k, v, lw, a, p, state):
    B, D = r.shape
    H, N = RW_HEADS, RW_HEAD
    lanes = lambda t: t.T.reshape(H, N, B)
    par = lambda t: jnp.broadcast_to(t.reshape(H, N, 1), (H, N, B))
    vec = pl.BlockSpec((1, N, B), lambda h: (h, 0, 0))
    st = pl.BlockSpec((1, N, N, B), lambda h: (h, 0, 0, 0))
    blocks = 2 * _nbytes((N, N, B), F32) + 11 * _nbytes((N, B), F32)
    y, s_new = pl.pallas_call(
        _wkv_step_body, grid=(H,),
        in_specs=[vec] * 10 + [st],
        out_specs=[vec, st],
        out_shape=[jax.ShapeDtypeStruct((H, N, B), F32), jax.ShapeDtypeStruct((H, N, N, B), F32)],
        scratch_shapes=[pltpu.VMEM((N, B), F32)],
        compiler_params=pltpu.CompilerParams(dimension_semantics=("parallel",),
                                             vmem_limit_bytes=_vmem_limit(blocks, _nbytes((N, B), F32), 4 << 20)),
        name="wkv_step",
    )(lanes(r), lanes(k), lanes(v), lanes(lw), lanes(a),
      par(p["kk"]), par(p["ka"]), par(p["rk"]), par(p["lnx_g"]), par(p["lnx_b"]),
      jnp.transpose(state, (1, 2, 3, 0)))
    return y.reshape(D, B).T, jnp.transpose(s_new, (3, 0, 1, 2))


def _rwkv_post_body(x_ref, y_ref, g_ref, wo_ref, o_ref):
    o_ref[...] = x_ref[...] + _mm(y_ref[...] * g_ref[...], wo_ref[...])


def _rwkv_post(x, y, g, wo, *, tm):
    M, D = x.shape
    tok = pl.BlockSpec((tm, D), lambda i: (i, 0))
    streamed = 4 * _nbytes((tm, D), F32)
    return pl.pallas_call(
        _rwkv_post_body, grid=(M // tm,),
        in_specs=[tok, tok, tok, _resident(wo.shape, lambda i: (0, 0))],
        out_specs=tok, out_shape=jax.ShapeDtypeStruct((M, D), F32),
        compiler_params=pltpu.CompilerParams(
            dimension_semantics=("parallel",),
            vmem_limit_bytes=_vmem_limit(streamed, _nbytes(wo.shape, BF16), 3 * _nbytes((tm, D), F32))),
        name="rwkv_post",
    )(x, y, g, wo)


def _ret_pre_body(x_ref, lng_ref, cos_ref, sin_ref, wq_ref, wk_ref, wv_ref, wg_ref,
                  q_ref, k_ref, v_ref, gate_ref, h_scr):
    @pl.when(pl.program_id(1) == 0)
    def _():
        h_scr[...] = _rms(x_ref[...], lng_ref[...]).astype(BF16)

    h = h_scr[...]
    cos = cos_ref[...]
    sin = sin_ref[...]
    half = RET_DK // 2

    def rotate(t, out_ref):
        t1, t2 = t[:, :half], t[:, half:]
        out_ref[:, :half] = t1 * cos - t2 * sin
        out_ref[:, half:] = t1 * sin + t2 * cos

    rotate(jnp.dot(h, wq_ref[...], preferred_element_type=F32), q_ref)
    rotate(jnp.dot(h, wk_ref[...], preferred_element_type=F32) * (RET_DK ** -0.5), k_ref)
    v_ref[...] = jnp.dot(h, wv_ref[...], preferred_element_type=F32)
    gz = jnp.dot(h, wg_ref[...], preferred_element_type=F32)
    gate_ref[...] = gz * jax.nn.sigmoid(gz)


def _ret_pre(x, ln_g, cos, sin, p, *, tm):
    M, D = x.shape
    H, DK, DV = RET_HEADS, RET_DK, RET_DV
    n_tab = cos.shape[0] // tm
    tab = pl.BlockSpec((tm, DK // 2), lambda i, h: (i % n_tab, 0))
    col = lambda w: pl.BlockSpec((D, w), lambda i, h: (0, h))
    out = lambda w: pl.BlockSpec((tm, w), lambda i, h: (i, h))
    blocks = (_nbytes((tm, D), F32) + 2 * _nbytes((tm, DK // 2), F32) + 2 * _nbytes((D, DK), BF16)
              + 2 * _nbytes((D, DV), BF16) + 2 * _nbytes((tm, DK), F32) + 2 * _nbytes((tm, DV), F32))
    return pl.pallas_call(
        _ret_pre_body, grid=(M // tm, H),
        in_specs=[pl.BlockSpec((tm, D), lambda i, h: (i, 0)), pl.BlockSpec((1, D), lambda i, h: (0, 0)),
                  tab, tab, col(DK), col(DK), col(DV), col(DV)],
        out_specs=[out(DK), out(DK), out(DV), out(DV)],
        out_shape=[jax.ShapeDtypeStruct((M, H * DK), F32), jax.ShapeDtypeStruct((M, H * DK), F32),
                   jax.ShapeDtypeStruct((M, H * DV), F32), jax.ShapeDtypeStruct((M, H * DV), F32)],
        scratch_shapes=[pltpu.VMEM((tm, D), BF16)],
        compiler_params=pltpu.CompilerParams(
            dimension_semantics=("parallel", "arbitrary"),
            vmem_limit_bytes=_vmem_limit(blocks, _nbytes((tm, D), BF16), 4 * _nbytes((tm, DV), F32))),
        name="ret_pre",
    )(x, ln_g.reshape(1, D), cos, sin, p["wq"], p["wk"], p["wv"], p["wg"])


def _ret_log_gamma():
    return jnp.log1p(-jnp.exp2(-5.0 - jnp.arange(RET_HEADS, dtype=F32)))


def _ret_mix_body(lg_ref, x_ref, lng_ref, cos_ref, sin_ref, wq_ref, wk_ref, wv_ref, wg_ref, wo_ref,
                  out_ref, s_ref, *, n_sub):
    C, DK, DV, H = RET_CHUNK, RET_DK, RET_DV, RET_HEADS
    half = DK // 2

    @pl.when(pl.program_id(1) == 0)
    def _():
        s_ref[...] = jnp.zeros_like(s_ref)

    x = x_ref[...]
    h = _rms(x, lng_ref[...]).astype(BF16)
    cos = cos_ref[...]
    sin = sin_ref[...]
    ii = lax.broadcasted_iota(jnp.int32, (C, C), 0)
    jj = lax.broadcasted_iota(jnp.int32, (C, C), 1)
    diff = jnp.maximum((ii - jj).astype(F32), 0.0)
    idx = lax.broadcasted_iota(jnp.int32, (C, 1), 0).astype(F32)
    rows = lambda cc: slice(cc * C, (cc + 1) * C)

    def rotate(t):
        t1, t2 = t[:, :half], t[:, half:]
        return jnp.concatenate([t1 * cos - t2 * sin, t1 * sin + t2 * cos], axis=-1)

    def project(hd):
        qs, vs = slice(hd * DK, (hd + 1) * DK), slice(hd * DV, (hd + 1) * DV)
        q = rotate(jnp.dot(h, wq_ref[:, qs], preferred_element_type=F32))
        k = rotate(jnp.dot(h, wk_ref[:, qs], preferred_element_type=F32) * (DK ** -0.5))
        v = jnp.dot(h, wv_ref[:, vs], preferred_element_type=F32).astype(BF16)
        gz = jnp.dot(h, wg_ref[:, vs], preferred_element_type=F32)
        return q, k, v, gz * jax.nn.sigmoid(gz)

    def retain(hd, q, k, v):
        lg = lg_ref[hd]
        dmat = jnp.where(ii >= jj, jnp.exp(lg * diff), 0.0)
        q_dec = jnp.exp(lg * (idx + 1.0))
        k_dec = jnp.exp(lg * (C - 1.0 - idx))
        s_dec = jnp.exp(jnp.full((1, DV), lg * C, F32))
        subs = range(n_sub)
        inner = [_mm_nt(q[rows(cc)], k[rows(cc)]) * dmat for cc in subs]
        kv = [lax.dot_general((k[rows(cc)] * k_dec).astype(BF16), v[rows(cc)], TN_DIMS,
                              preferred_element_type=F32) for cc in subs]
        S = s_ref[0, hd]
        outs = []
        for cc in subs:
            outs.append(jnp.dot(inner[cc].astype(BF16), v[rows(cc)], preferred_element_type=F32)
                        + _mm(q[rows(cc)] * q_dec, S))
            S = S * s_dec + kv[cc]
        s_ref[0, hd] = S
        return jnp.concatenate(outs, axis=0)

    acc = x
    proj = project(0)
    for hd in range(H):
        nxt = project(hd + 1) if hd + 1 < H else None
        q, k, v, gate = proj
        o = retain(hd, q, k, v)
        acc = acc + _mm(gate * _head_norm(o, RET_GN_EPS), wo_ref[hd * DV:(hd + 1) * DV, :])
        proj = nxt
    out_ref[...] = acc


def _ret_mix(x, ln_g, cos, sin, p, *, n_seq, n_sub):
    M, D = x.shape
    H, DK, DV, C = RET_HEADS, RET_DK, RET_DV, RET_CHUNK
    T = C * n_sub
    nl = M // n_seq // T
    const = lambda b, l: (0, 0)
    tok = pl.BlockSpec((T, D), lambda b, l: (b * nl + l, 0))
    tab = pl.BlockSpec((T, DK // 2), lambda b, l: (l, 0))
    st = pl.BlockSpec((1, H, DK, DV), lambda b, l: (b, 0, 0, 0))
    weights = [p["wq"], p["wk"], p["wv"], p["wg"], p["wo"]]
    streamed = 2 * _nbytes((T, D), F32) + 2 * _nbytes((T, DK // 2), F32) + _nbytes((H, DK, DV), F32)
    resident = sum(_nbytes(w.shape, BF16) for w in weights)
    temps = 8 * _nbytes((T, DV), F32) + 4 * _nbytes((DK, DV), F32)
    return pl.pallas_call(
        functools.partial(_ret_mix_body, n_sub=n_sub), grid=(n_seq, nl),
        in_specs=[pl.BlockSpec(memory_space=pltpu.SMEM), tok, _resident((1, D), const), tab, tab]
                 + [_resident(w.shape, const) for w in weights],
        out_specs=[tok, st],
        out_shape=[jax.ShapeDtypeStruct((M, D), F32), jax.ShapeDtypeStruct((n_seq, H, DK, DV), F32)],
        compiler_params=pltpu.CompilerParams(dimension_semantics=("parallel", "arbitrary"),
                                             vmem_limit_bytes=_vmem_limit(streamed, resident, temps)),
        name="ret_mix",
    )(_ret_log_gamma(), x, ln_g.reshape(1, D), cos, sin, *weights)


def _ret_side(lg_ref, q_ref, k_ref, v_ref, s_ref, o_ref, so_ref):
    bt, H, DK = q_ref.shape
    q_cols = q_ref[...].reshape(bt * H, DK).T
    k_cols = k_ref[...].reshape(bt * H, DK).T

    def unit(b, hd):
        gam = jnp.exp(jnp.full((1, RET_DV), lg_ref[hd], F32))
        q = q_cols[:, b * H + hd:b * H + hd + 1]
        k = k_cols[:, b * H + hd:b * H + hd + 1]
        v = v_ref[b, hd:hd + 1, :]
        S = s_ref[b, hd]
        so_ref[b, hd] = S * gam + k * v
        o_ref[b, hd:hd + 1, :] = (gam * jnp.sum(q * S, axis=0, keepdims=True)
                                  + jnp.sum(q * k, axis=0, keepdims=True) * v)

    return [functools.partial(unit, b, hd) for b in range(bt) for hd in range(H)]


def _ret_post_body(x_ref, o_ref, gate_ref, wo_ref, out_ref):
    acc = x_ref[...]
    for hd in range(RET_HEADS):
        sl = slice(hd * RET_DV, (hd + 1) * RET_DV)
        acc = acc + _mm(gate_ref[:, sl] * _head_norm(o_ref[:, sl], RET_GN_EPS), wo_ref[sl, :])
    out_ref[...] = acc


def _ret_post(x, o, gate, wo, *, tm):
    M, D = x.shape
    W = o.shape[1]
    tok = lambda w: pl.BlockSpec((tm, w), lambda i: (i, 0))
    streamed = 2 * _nbytes((tm, D), F32) + 2 * _nbytes((tm, W), F32)
    return pl.pallas_call(
        _ret_post_body, grid=(M // tm,),
        in_specs=[tok(D), tok(W), tok(W), _resident(wo.shape, lambda i: (0, 0))],
        out_specs=tok(D), out_shape=jax.ShapeDtypeStruct((M, D), F32),
        compiler_params=pltpu.CompilerParams(
            dimension_semantics=("parallel",),
            vmem_limit_bytes=_vmem_limit(streamed, _nbytes(wo.shape, BF16), 3 * _nbytes((tm, W), F32))),
        name="ret_post",
    )(x, o, gate, wo)


def _rope_tables(pos):
    half = RET_DK // 2
    inv = ROPE_BASE ** (-jnp.arange(half, dtype=F32) / half)
    ang = pos.astype(F32)[:, None] * inv[None, :]
    return jnp.cos(ang), jnp.sin(ang)


def _rwkv_params(w, i):
    j = i // N_MIXERS
    p = {n: w["rw_" + n][j] for n in ("mu", "w0", "w1", "w2", "a0", "a1", "a2", "g1", "g2", "kk", "ka", "rk",
                                      "wr", "wk", "wv", "wo", "lnx_g", "lnx_b")}
    p["ln"] = w["ln_mix"][i]
    return p


def _ret_params(w, i):
    return {n: w["ret_" + n][i // N_MIXERS] for n in ("wq", "wk", "wv", "wg", "wo")}


def _ffn_of(w, which, i, tiles, tm="ffn_m"):
    (wg, layer), (wu, _), (wd, _) = (w[f"ff{which}_{n}"][i] for n in ("wg", "wu", "wd"))
    ln_final = w["ln_final"] if (which == 2 and i == DEPTH - 1) else None
    return functools.partial(_ffn, ln_g=w[f"ln_ffn{which}"][i], wg=wg, wu=wu, wd=wd, ln_final=ln_final,
                             layer=layer, tm=tiles[tm], tf=tiles["ffn_f"])


def _cast_late_weights(x, w, f32, tiles):
    flat = lambda t: t.reshape(-1, t.shape[-1])
    todo = []
    for n in FF2_STACKS:
        todo.append((n, None, (flat(f32[n]), 0, f32[n].shape[0] * f32[n].shape[1])))
    for n in FF1_STACKS:
        rows = f32[n].shape[1]
        todo += [(n, i, (flat(f32[n]), i * rows, rows)) for i in range(1, DEPTH)]
    for n in LATE_MIXER:
        todo += [(n, j, (t, 0, t.shape[0])) for j, t in enumerate(f32[n])]
    x, _, _, done = _ffn_of(w, 1, 0, tiles, tm="ffn_m_cast")(x, cast=[item for _, _, item in todo])
    for (n, i, _), t in zip(todo, done):
        if n in FF2_STACKS:
            stack = t.reshape(f32[n].shape)
            w[n] = [(stack, i) for i in range(DEPTH)]
        elif n in FF1_STACKS:
            w[n][i] = (t[None], 0)
        else:
            w[n][i] = t
    return x


def _prefill(x, n_seq, w, f32, tiles, start_side):
    M, D = x.shape
    cos, sin = _rope_tables(jnp.arange(M // n_seq, dtype=F32))
    new_shift, new_wkv, new_ret = [], [], []
    x = _cast_late_weights(x, w, f32, tiles)
    side = start_side(w)
    side_out = None

    def ffn(x, which, i):
        nonlocal side_out
        if side_out is not None:
            return _ffn_of(w, which, i, tiles)(x)
        x, *side_out, _ = _ffn_of(w, which, i, tiles, tm="ffn_m_side")(x, side=side)
        return x

    for i in range(DEPTH):
        if i > 0:
            x = ffn(x, 1, i)
        if i % N_MIXERS == 0:
            x, sh, st = _rwkv_mix(x, jnp.zeros((n_seq, D), F32), _rwkv_params(w, i), n_seq=n_seq,
                                  n_s=tiles["wkv_seqs"], T=tiles["wkv_tokens"])
            new_shift.append(sh)
            new_wkv.append(st)
        else:
            x, st = _ret_mix(x, w["ln_mix"][i], cos, sin, _ret_params(w, i), n_seq=n_seq, n_sub=tiles["ret_sub"])
            new_ret.append(st)
        x = ffn(x, 2, i)
    return (x, jnp.stack(new_shift), jnp.stack(new_wkv), jnp.stack(new_ret), *side_out)


def _decode(x, pos, shift0, wkv0, w, tiles):
    B, D = x.shape
    cos, sin = (jnp.tile(t, (B, 1)) for t in _rope_tables(pos))
    new_shift, new_wkv = [], []
    for i in range(DEPTH):
        x = _ffn_of(w, 1, i, tiles)(x)
        if i % N_MIXERS == 0:
            p = _rwkv_params(w, i)
            r, k, v, lw, a, g, sh = _rwkv_pre(x, shift0[i // N_MIXERS], p, tile=tiles["rwkv_pre"])
            y, st = _wkv_step(r, k, v, lw, a, p, wkv0[i // N_MIXERS])
            x = _rwkv_post(x, y, g, p["wo"], tm=tiles["post"])
            new_shift.append(sh)
            new_wkv.append(st)
        else:
            p = _ret_params(w, i)
            q, k, v, gate = _ret_pre(x, w["ln_mix"][i], cos, sin, p, tm=tiles["ret_pre"])
            o = yield q, k, v
            x = _ret_post(x, o, gate, p["wo"], tm=tiles["post"])
        x = _ffn_of(w, 2, i, tiles)(x)
    return x, jnp.stack(new_shift), jnp.stack(new_wkv)


PROMPT_TILES = dict(ffn_m=1024, ffn_m_cast=512, ffn_m_side=256, ffn_f=256, ret_sub=4, wkv_seqs=2, wkv_tokens=256)
SAMPLE_TILES = dict(ffn_m=128, ffn_f=256, rwkv_pre=128, post=128, ret_pre=128)

FF1_STACKS = ("ff1_wg", "ff1_wu", "ff1_wd")
FF2_STACKS = ("ff2_wg", "ff2_wu", "ff2_wd")
EARLY_MIXER = ("rw_w1", "rw_w2", "rw_a1", "rw_a2", "rw_g1", "rw_g2")
LATE_MIXER = ("rw_wr", "rw_wk", "rw_wv", "rw_wo", "ret_wq", "ret_wk", "ret_wv", "ret_wg", "ret_wo")


def kernel(x_prompt, x_sample, state_rwkv_shift, state_rwkv_wkv, state_ret, ln_ffn1, ff1_wg, ff1_wu, ff1_wd, ln_mix, ln_ffn2, ff2_wg, ff2_wu, ff2_wd, ln_final, rw_mu, rw_w0, rw_w1, rw_w2, rw_a0, rw_a1, rw_a2, rw_g1, rw_g2, rw_kk, rw_ka, rw_rk, rw_wr, rw_wk, rw_wv, rw_wo, rw_lnx_g, rw_lnx_b, ret_wq, ret_wk, ret_wv, ret_wg, ret_wo):
    w = dict(ln_ffn1=ln_ffn1, ff1_wg=ff1_wg, ff1_wu=ff1_wu, ff1_wd=ff1_wd, ln_mix=ln_mix,
             ln_ffn2=ln_ffn2, ff2_wg=ff2_wg, ff2_wu=ff2_wu, ff2_wd=ff2_wd, ln_final=ln_final,
             rw_mu=rw_mu, rw_w0=rw_w0, rw_w1=rw_w1, rw_w2=rw_w2, rw_a0=rw_a0, rw_a1=rw_a1,
             rw_a2=rw_a2, rw_g1=rw_g1, rw_g2=rw_g2, rw_kk=rw_kk, rw_ka=rw_ka, rw_rk=rw_rk,
             rw_wr=rw_wr, rw_wk=rw_wk, rw_wv=rw_wv, rw_wo=rw_wo, rw_lnx_g=rw_lnx_g,
             rw_lnx_b=rw_lnx_b, ret_wq=ret_wq, ret_wk=ret_wk, ret_wv=ret_wv, ret_wg=ret_wg,
             ret_wo=ret_wo)
    f32 = {n: w[n] for n in FF1_STACKS + FF2_STACKS}
    f32.update({n: list(w[n]) for n in LATE_MIXER})
    for n in FF1_STACKS:
        w[n] = [(w[n][:1].astype(BF16), 0)] + [None] * (DEPTH - 1)
    for n in LATE_MIXER:
        w[n] = [None] * len(f32[n])
    for n in EARLY_MIXER:
        w[n] = [t.astype(BF16) for t in w[n]]
    B, L, D = x_prompt.shape
    Bs, Ls, _ = x_sample.shape
    assert Ls == 1 and state_ret.shape[0] == 1, "one decode token and one retention layer are supported"
    H, DK, DV = RET_HEADS, RET_DK, RET_DV
    decode = None

    def start_decode(w):
        nonlocal decode
        decode = _decode(x_sample.reshape(Bs, D), PAST_LEN + jnp.arange(Ls, dtype=F32), state_rwkv_shift,
                         state_rwkv_wkv, w, SAMPLE_TILES)
        q, k, v = next(decode)
        return dict(q=q.reshape(Bs, H, DK), k=k.reshape(Bs, H, DK), v=v.reshape(Bs, H, DV), state=state_ret[0])

    y_p, p_shift, p_wkv, p_ret, o, s_ret = _prefill(x_prompt.reshape(B * L, D), B, w, f32, PROMPT_TILES,
                                                    start_decode)
    try:
        decode.send(o.reshape(Bs, H * DV))
        raise AssertionError("the decode group has a single retention layer")
    except StopIteration as done:
        y_s, s_shift, s_wkv = done.value
    return (y_p.reshape(B, L, D), y_s.reshape(Bs, Ls, D), p_shift, p_wkv, p_ret, s_shift, s_wkv, s_ret[None])
```

```python
import functools
import math

import numpy as np
import jax
import jax.numpy as jnp
from jax import lax
from jax.experimental import pallas as pl
from jax.experimental.pallas import tpu as pltpu

F32 = jnp.float32
BF16 = jnp.bfloat16

DEPTH = 2
N_MIXERS = 2
RW_HEADS = 16
RW_HEAD = 64
RW_GN_EPS = 64e-5
RET_HEADS = 4
RET_DK = 256
RET_DV = 512
RET_CHUNK = 128
RET_GN_EPS = 1e-6
ROPE_BASE = 10000.0
FFN_RES = 0.5
RMS_EPS = 1e-6
PAST_LEN = 16384

WKV_CHUNK = 64
V7X_VMEM_CAP_BYTES = 60000 * 1024
SUBLANES = 8
BF16_SUBLANES = 16

NT_DIMS = (((1,), (1,)), ((), ()))
TN_DIMS = (((0,), (0,)), ((), ()))


def _vmem_limit(block_bytes, scratch_bytes=0, temp_bytes=0):
    return int(min(V7X_VMEM_CAP_BYTES, 2 * block_bytes + scratch_bytes + temp_bytes + (4 << 20)))


def _nbytes(shape, dtype):
    return int(np.prod(shape)) * jnp.dtype(dtype).itemsize


def _resident(shape, index_map):
    return pl.BlockSpec(shape, index_map, pipeline_mode=pl.Buffered(1))


def _rms(x, g):
    return x * lax.rsqrt(jnp.mean(x * x, axis=-1, keepdims=True) + RMS_EPS) * g


def _head_norm(y, eps):
    mu = jnp.mean(y, axis=-1, keepdims=True)
    yc = y - mu
    return yc * lax.rsqrt(jnp.mean(yc * yc, axis=-1, keepdims=True) + eps)


def _mm(a, b):
    return jnp.dot(a.astype(BF16), b.astype(BF16), preferred_element_type=F32)


def _mm_nt(a, b):
    return lax.dot_general(a.astype(BF16), b.astype(BF16), NT_DIMS, preferred_element_type=F32)


def _mm_tn(a, b):
    return lax.dot_general(a.astype(BF16), b.astype(BF16), TN_DIMS, preferred_element_type=F32)


def _ffn_body(*refs, ff_tile, final, side, n_cast):
    refs = list(refs)
    x_ref, g_ref, wg_ref, wu_ref, wd_ref = refs[:5]
    del refs[:5]
    gf_ref = refs.pop(0) if final else None
    if side:
        side_in = refs[:5]
        del refs[:5]
    cast_in = refs[:n_cast]
    del refs[:n_cast]
    o_ref = refs.pop(0)
    cast_out = refs[len(refs) - n_cast:]
    del refs[len(refs) - n_cast:]
    x = x_ref[...]
    h = _rms(x, g_ref[...]).astype(BF16)
    n_tiles = wg_ref.shape[1] // ff_tile

    def activation(j):
        sl = slice(j * ff_tile, (j + 1) * ff_tile)
        gate = jnp.dot(h, wg_ref[:, sl], preferred_element_type=F32)
        up = jnp.dot(h, wu_ref[:, sl], preferred_element_type=F32)
        return (gate * jax.nn.sigmoid(gate) * up).astype(BF16)

    side_units = _ret_side(*side_in, *refs) if side else []
    per_tile = -(-len(side_units) // n_tiles)
    act = activation(0)
    acc = None
    for j in range(n_tiles):
        nxt = activation(j + 1) if j + 1 < n_tiles else None
        part = jnp.dot(act, wd_ref[j * ff_tile:(j + 1) * ff_tile, :], preferred_element_type=F32)
        acc = part if acc is None else acc + part
        act = nxt
        for unit in side_units[j * per_tile:(j + 1) * per_tile]:
            unit()
    y = x + FFN_RES * acc
    if final:
        y = _rms(y, gf_ref[...])
    o_ref[...] = y
    for src, dst in zip(cast_in, cast_out):
        dst[...] = src[...].astype(BF16)


def _ffn(x, ln_g, wg, wu, wd, ln_final=None, *, layer, tm, tf, side=None, cast=()):
    M, D = x.shape
    FF = wg.shape[2]
    final = ln_final is not None
    steps = M // tm
    row = lambda i: (i, 0)
    const = lambda i: (0, 0)
    of_layer = lambda i: (layer, 0, 0)
    in_specs = [pl.BlockSpec((tm, D), row), _resident((1, D), const), _resident((None, D, FF), of_layer),
                _resident((None, D, FF), of_layer), _resident((None, FF, D), of_layer)]
    args = [x, ln_g.reshape(1, D), wg, wu, wd]
    if final:
        in_specs.append(_resident((1, D), const))
        args.append(ln_final.reshape(1, D))
    out_specs = [pl.BlockSpec((tm, D), row)]
    out_shape = [jax.ShapeDtypeStruct((M, D), F32)]
    streamed = 2 * _nbytes((tm, D), F32)
    if side is not None:
        B, H, DK, DV = side["state"].shape
        bt = B // steps
        assert bt * steps == B, "every grid step takes the same number of sequences"
        taken = lambda i: (i, 0, 0)
        st = pl.BlockSpec((bt, H, DK, DV), lambda i: (i, 0, 0, 0))
        in_specs += [pl.BlockSpec(memory_space=pltpu.SMEM), pl.BlockSpec((bt, H, DK), taken),
                     pl.BlockSpec((bt, H, DK), taken), pl.BlockSpec((bt, H, DV), taken), st]
        args += [_ret_log_gamma(), side["q"], side["k"], side["v"], side["state"]]
        out_specs += [pl.BlockSpec((bt, H, DV), taken), st]
        out_shape += [jax.ShapeDtypeStruct((B, H, DV), F32), jax.ShapeDtypeStruct((B, H, DK, DV), F32)]
        streamed += 2 * _nbytes((bt, H, DK, DV), F32) + 4 * _nbytes((bt, SUBLANES, DV), F32)
    for t, first_row, n_rows in cast:
        span = next(s for s in (1, 2, 4) if (n_rows * s) % (steps * BF16_SUBLANES) == 0)
        rb = n_rows * span // steps
        cols = t.shape[1]
        assert first_row % rb == 0
        in_specs.append(pl.BlockSpec((rb, cols), lambda i, first=first_row // rb, span=span: (first + i // span, 0)))
        args.append(t)
        out_specs.append(pl.BlockSpec((rb, cols), lambda i, span=span: (i // span, 0)))
        out_shape.append(jax.ShapeDtypeStruct((n_rows, cols), BF16))
        streamed += _nbytes((rb, cols), F32) + _nbytes((rb, cols), BF16)
    resident = 3 * _nbytes((D, FF), BF16)
    temps = 6 * _nbytes((tm, tf), F32) + 3 * _nbytes((tm, D), F32)
    out = pl.pallas_call(
        functools.partial(_ffn_body, ff_tile=tf, final=final, side=side is not None, n_cast=len(cast)),
        grid=(steps,), in_specs=in_specs, out_specs=out_specs, out_shape=out_shape,
        compiler_params=pltpu.CompilerParams(dimension_semantics=("arbitrary",) if cast else ("parallel",),
                                             vmem_limit_bytes=_vmem_limit(streamed, resident, temps)),
        name="ffn_final" if final else "ffn",
    )(*args)
    if side is None and not cast:
        return out[0]
    o, state = out[1:3] if side is not None else (None, None)
    return out[0], o, state, tuple(out[len(out) - len(cast):])


def _rwkv_project(h, xprev, mu_ref, w0_ref, a0_ref, wr_ref, wk_ref, wv_ref, w1_ref, w2_ref, a1_ref, a2_ref,
                  g1_ref, g2_ref):
    xx = xprev - h
    xr, xw, xk, xv, xa, xg = (h + xx * mu_ref[i:i + 1, :] for i in range(6))
    r = _mm(xr, wr_ref[...])
    k = _mm(xk, wk_ref[...])
    v = _mm(xv, wv_ref[...])
    z = w0_ref[...] + _mm(jnp.tanh(_mm(xw, w1_ref[...])), w2_ref[...])
    lw = -math.exp(-0.5) * jax.nn.sigmoid(z)
    a = jax.nn.sigmoid(a0_ref[...] + _mm(_mm(xa, a1_ref[...]), a2_ref[...]))
    g = _mm(jax.nn.sigmoid(_mm(xg, g1_ref[...])), g2_ref[...])
    return r, k, v, lw, a, g


def _rwkv_pre_body(x_ref, lng_ref, prev_ref, *refs):
    weight_refs, out_refs = refs[:12], refs[12:]
    h = _rms(x_ref[...], lng_ref[...])
    for ref, val in zip(out_refs, _rwkv_project(h, prev_ref[...], *weight_refs) + (h,)):
        ref[...] = val


def _rwkv_pre(x, prev, p, *, tile):
    M, D = x.shape
    vec = lambda a: a.reshape(1, D)
    row = lambda i: (i, 0)
    tok = pl.BlockSpec((tile, D), row)
    full = lambda a: _resident(a.shape, lambda i: (0, 0))
    params = [p["mu"], vec(p["w0"]), vec(p["a0"])]
    weights = [p["wr"], p["wk"], p["wv"], p["w1"], p["w2"], p["a1"], p["a2"], p["g1"], p["g2"]]
    streamed = 9 * _nbytes((tile, D), F32)
    resident = sum(_nbytes(w.shape, BF16) for w in weights)
    return pl.pallas_call(
        _rwkv_pre_body, grid=(M // tile,),
        in_specs=[tok, full(vec(p["ln"])), tok] + [full(t) for t in params + weights],
        out_specs=[tok] * 7,
        out_shape=[jax.ShapeDtypeStruct((M, D), F32)] * 7,
        compiler_params=pltpu.CompilerParams(
            dimension_semantics=("parallel",),
            vmem_limit_bytes=_vmem_limit(streamed, resident, 10 * _nbytes((tile, D), F32))),
        name="rwkv_pre",
    )(x, vec(p["ln"]), prev, *params, *weights)


def _wkv_chunks(r, k, v, lw, a, kkw, kaw, rkw, lnx_g, lnx_b, S, n_s):
    C = WKV_CHUNK
    N = RW_HEAD
    W = 2 * N
    n_g = r.shape[0] // C
    n_c = n_g // n_s
    ii = lax.broadcasted_iota(jnp.int32, (2 * C, 2 * C), 0)
    jj = lax.broadcasted_iota(jnp.int32, (2 * C, 2 * C), 1)
    tri = ((ii >= jj) & ((ii < C) == (jj < C))).astype(BF16)
    lw_hi = lw.astype(BF16)
    rest = lw - lw_hi.astype(F32)
    lw_mid = rest.astype(BF16)
    lw_lo = (rest - lw_mid.astype(F32)).astype(BF16)
    cl = jnp.concatenate(
        [sum(jnp.dot(tri, piece[b:b + 2 * C], preferred_element_type=F32) for piece in (lw_hi, lw_mid, lw_lo))
         for b in range(0, n_g * C, 2 * C)], axis=0)
    gls = [cl[g * C + C - 1:g * C + C, :] for g in range(n_g)]
    gl = jnp.concatenate([jnp.broadcast_to(t, (C, t.shape[-1])) for t in gls], axis=0)
    e_cl = jnp.exp(cl)
    e_ncl = jnp.exp(-cl)
    e_ce = jnp.exp(cl - lw)
    e_g = jnp.exp(gl - cl)
    dgs = [jnp.exp(t) for t in gls]
    kkraw = k * kkw
    kmod = k * (1.0 + (a - 1.0) * kaw)
    rkk = r * kmod * rkw
    rt = r * e_cl
    kt = kmod * e_ncl
    kg = kmod * e_g
    row = lax.broadcasted_iota(jnp.int32, (C, W), 0)
    lane = lax.broadcasted_iota(jnp.int32, (C, W), 1)
    left = lane < N
    tok = jnp.where(left, lane, lane - N)
    strict = row > tok
    incl2 = jnp.concatenate([row >= tok] * 2, axis=1)
    eye = (row == tok).astype(F32)
    same_head = ((lax.broadcasted_iota(jnp.int32, (W, W), 0) < N)
                 == (lax.broadcasted_iota(jnp.int32, (W, W), 1) < N))

    def seg_sum(x):
        s_l = jnp.sum(jnp.where(left, x, 0.0), axis=-1, keepdims=True)
        s_r = jnp.sum(jnp.where(left, 0.0, x), axis=-1, keepdims=True)
        return jnp.where(left, s_l, s_r)

    def bd(x):
        xb = x.astype(BF16)
        zero = jnp.zeros_like(xb)
        return jnp.concatenate([jnp.where(left, xb, zero), jnp.where(left, zero, xb)], axis=0)

    def mm(x, y):
        return jnp.dot(x.astype(BF16), y, preferred_element_type=F32)

    pairs = range(RW_HEADS // 2)
    units = [(g, p) for g in range(n_g) for p in pairs]
    at = {u: (slice(u[0] * C, (u[0] + 1) * C), slice(u[1] * W, (u[1] + 1) * W)) for u in units}
    V2 = {u: v[at[u]] for u in units}
    lhs2, rhs_nt, bk, bonus = {}, {}, {}, {}
    for u in units:
        kk2 = kkraw[at[u]]
        kkn = kk2 / jnp.maximum(jnp.sqrt(seg_sum(kk2 * kk2)), 1e-12)
        b2 = kkn * a[at[u]]
        lhs2[u] = jnp.concatenate([-kkn * e_ce[at[u]], rt[at[u]]], axis=0).astype(BF16)
        rhs_nt[u] = jnp.concatenate([bd(b2 * e_ncl[at[u]]), bd(kt[at[u]])], axis=0)
        bk[u] = jnp.concatenate([b2 * e_g[at[u]], kg[at[u]]], axis=0).astype(BF16)
        bonus[u] = seg_sum(rkk[at[u]]) * V2[u]
    P = {u: lax.dot_general(lhs2[u], rhs_nt[u], NT_DIMS, preferred_element_type=F32) for u in units}
    bd_v = {u: bd(V2[u]) for u in units}
    nak_v = {u: mm(jnp.where(strict, P[u][:C, W:], 0.0), bd_v[u]) for u in units}
    p_bot = {u: jnp.where(incl2, P[u][C:], 0.0).astype(BF16) for u in units}
    Pk = {u: jnp.where(strict, P[u][:C, :W], 0.0) for u in units}
    inv = {u: eye + Pk[u] for u in units}
    Pk = {u: mm(Pk[u], bd(Pk[u])) for u in units}
    span = 2
    while span < C:
        span *= 2
        if span < C:
            Z = {u: mm(Pk[u], jnp.concatenate([bd(inv[u]), bd(Pk[u])], axis=1)) for u in units}
            inv = {u: inv[u] + Z[u][:, :W] for u in units}
            Pk = {u: Z[u][:, W:] for u in units}
        else:
            inv = {u: inv[u] + mm(Pk[u], bd(inv[u])) for u in units}
    inv = {u: inv[u].astype(BF16) for u in units}
    y = {}
    for c in range(n_c):
        now = [((s, p), (s * n_c + c, p)) for s in range(n_s) for p in pairs]
        LS = {u: lax.dot_general(lhs2[gu], S[u].astype(BF16), NT_DIMS, preferred_element_type=F32)
              for u, gu in now}
        U = {u: jnp.dot(inv[gu], bd(LS[u][:C] + nak_v[gu]), preferred_element_type=F32) for u, gu in now}
        Y = {u: LS[u][C:] + jnp.dot(p_bot[gu], jnp.concatenate([bd(U[u]), bd_v[gu]], axis=0),
                                    preferred_element_type=F32) for u, gu in now}
        S = dict(S)
        for u, gu in now:
            S[u] = (S[u] * dgs[gu[0]][:, at[gu][1]]
                    + jnp.where(same_head, _mm_tn(jnp.concatenate([U[u], V2[gu]], axis=0), bk[gu]), 0.0))
        for u, gu in now:
            sl = at[gu][1]
            yc = Y[u] - seg_sum(Y[u]) * (1.0 / N)
            yn = yc * lax.rsqrt(seg_sum(yc * yc) * (1.0 / N) + RW_GN_EPS)
            y[(u[0], c, u[1])] = yn * lnx_g[:, sl] + lnx_b[:, sl] + bonus[gu]
    return y, S


def _rwkv_mix_body(x_ref, lng_ref, prev_ref, mu_ref, w0_ref, a0_ref, kkw_ref, kaw_ref, rk_ref, lg_ref, lb_ref,
                   wr_ref, wk_ref, wv_ref, wo_ref, w1_ref, w2_ref, a1_ref, a2_ref, g1_ref, g2_ref,
                   out_ref, so_ref, s_ref,
                   carry, sbd_scr, y_scr, *, n_steps):
    C, N, W = WKV_CHUNK, RW_HEAD, 2 * RW_HEAD
    n_s, T, D = x_ref.shape
    step = pl.program_id(1)

    @pl.when(step == 0)
    def _():
        carry[...] = prev_ref[...]
        sbd_scr[...] = jnp.zeros_like(sbd_scr)

    x = x_ref[...].reshape(n_s * T, D)
    h = _rms(x, lng_ref[...])
    row = lax.broadcasted_iota(jnp.int32, h.shape, 0)
    xprev = pltpu.roll(h, 1, 0)
    for s in range(n_s):
        xprev = jnp.where(row == s * T, carry[s], xprev)
        last = h[s * T + T - 1:s * T + T, :]
        carry[s] = last
        so_ref[s] = last
    r, k, v, lw, a, g = _rwkv_project(
        h, xprev, mu_ref, w0_ref, a0_ref, wr_ref, wk_ref, wv_ref, w1_ref, w2_ref, a1_ref, a2_ref, g1_ref, g2_ref)
    units = [(s, p) for s in range(n_s) for p in range(RW_HEADS // 2)]
    S = {u: sbd_scr[u[0], u[1]] for u in units}
    y, S = _wkv_chunks(r, k, v, lw, a, kkw_ref[...], kaw_ref[...], rk_ref[...], lg_ref[...], lb_ref[...], S, n_s)
    for (s, c, p), t in y.items():
        y_scr[s * T + c * C:s * T + (c + 1) * C, p * W:(p + 1) * W] = t
    for u in units:
        sbd_scr[u[0], u[1]] = S[u]
    out_ref[...] = (x + _mm(y_scr[...] * g, wo_ref[...])).reshape(n_s, T, D)

    @pl.when(step == n_steps - 1)
    def _():
        for s, p in units:
            S = sbd_scr[s, p]
            s_ref[s, 2 * p] = S[:N, :N]
            s_ref[s, 2 * p + 1] = S[N:, N:]


def _rwkv_mix(x, prev, p, *, n_seq, n_s, T):
    M, D = x.shape
    L = M // n_seq
    nl = L // T
    H, N = RW_HEADS, RW_HEAD
    vec = lambda t: t.reshape(1, D)
    const = lambda b, l: (0, 0)
    tok = pl.BlockSpec((n_s, T, D), lambda b, l: (b, l, 0))
    row1 = pl.BlockSpec((n_s, 1, D), lambda b, l: (b, 0, 0))
    st = pl.BlockSpec((n_s, H, N, N), lambda b, l: (b, 0, 0, 0))
    params = [vec(p["ln"]), p["mu"], vec(p["w0"]), vec(p["a0"]), vec(p["kk"]), vec(p["ka"]), vec(p["rk"]),
              vec(p["lnx_g"]), vec(p["lnx_b"])]
    weights = [p["wr"], p["wk"], p["wv"], p["wo"], p["w1"], p["w2"], p["a1"], p["a2"], p["g1"], p["g2"]]
    full = lambda t: _resident(t.shape, const)
    in_specs = [tok, full(params[0]), row1] + [full(t) for t in params[1:]] + [full(t) for t in weights]
    rows = (n_s * T, D)
    pair_state = (n_s, H // 2, 2 * N, 2 * N)
    scratch = [pltpu.VMEM((n_s, 1, D), F32), pltpu.VMEM(pair_state, F32), pltpu.VMEM(rows, F32)]
    streamed = 2 * _nbytes((n_s, T, D), F32) + _nbytes((n_s, H, N, 128), F32)
    resident = sum(_nbytes(t.shape, BF16) for t in weights) + _nbytes(pair_state, F32) + _nbytes(rows, F32)
    out, so, state = pl.pallas_call(
        functools.partial(_rwkv_mix_body, n_steps=nl), grid=(n_seq // n_s, nl),
        in_specs=in_specs, out_specs=[tok, row1, st],
        out_shape=[jax.ShapeDtypeStruct((n_seq, L, D), F32), jax.ShapeDtypeStruct((n_seq, 1, D), F32),
                   jax.ShapeDtypeStruct((n_seq, H, N, N), F32)],
        scratch_shapes=scratch,
        compiler_params=pltpu.CompilerParams(dimension_semantics=("parallel", "arbitrary"),
                                             vmem_limit_bytes=_vmem_limit(streamed, resident, 12 * _nbytes(rows, F32))),
        name="rwkv_mix",
    )(x.reshape(n_seq, L, D), params[0], prev.reshape(n_seq, 1, D), *params[1:], *weights)
    return out.reshape(M, D), so.reshape(n_seq, D), state


def _wkv_step_body(r_ref, k_ref, v_ref, lw_ref, a_ref, kkw_ref, kaw_ref, rk_ref, lg_ref, lb_ref, s_ref,
                   y_ref, so_ref, y_scr):
    r = r_ref[0]
    k = k_ref[0]
    v = v_ref[0]
    a = a_ref[0]
    d = jnp.exp(lw_ref[0])
    kkh = k * kkw_ref[0]
    nrm = jnp.sqrt(jnp.sum(kkh * kkh, axis=0, keepdims=True))
    kkn = kkh / jnp.maximum(nrm, 1e-12)
    kmod = k * (1.0 + (a - 1.0) * kaw_ref[0])
    bh = kkn * a

    def value_channel(i, carry):
        S = s_ref[0, i]
        sa = -jnp.sum(S * kkn, axis=0, keepdims=True)
        s_new = S * d + sa * bh + v_ref[0, pl.ds(i, 1), :] * kmod
        so_ref[0, i] = s_new
        y_scr[pl.ds(i, 1), :] = jnp.sum(s_new * r, axis=0, keepdims=True)
        return carry

    lax.fori_loop(0, RW_HEAD, value_channel, 0, unroll=8)
    y = y_scr[...]
    mu = jnp.mean(y, axis=0, keepdims=True)
    yc = y - mu
    yn = yc * lax.rsqrt(jnp.mean(yc * yc, axis=0, keepdims=True) + RW_GN_EPS)
    y_ref[0] = yn * lg_ref[0] + lb_ref[0] + jnp.sum(r * kmod * rk_ref[0], axis=0, keepdims=True) * v


def _wkv_step(r, k, v, lw, a, p, state):
    B, D = r.shape
    H, N = RW_HEADS, RW_HEAD
    lanes = lambda t: t.T.reshape(H, N, B)
    par = lambda t: jnp.broadcast_to(t.reshape(H, N, 1), (H, N, B))
    vec = pl.BlockSpec((1, N, B), lambda h: (h, 0, 0))
    st = pl.BlockSpec((1, N, N, B), lambda h: (h, 0, 0, 0))
    blocks = 2 * _nbytes((N, N, B), F32) + 11 * _nbytes((N, B), F32)
    y, s_new = pl.pallas_call(
        _wkv_step_body, grid=(H,),
        in_specs=[vec] * 10 + [st],
        out_specs=[vec, st],
        out_shape=[jax.ShapeDtypeStruct((H, N, B), F32), jax.ShapeDtypeStruct((H, N, N, B), F32)],
        scratch_shapes=[pltpu.VMEM((N, B), F32)],
        compiler_params=pltpu.CompilerParams(dimension_semantics=("parallel",),
                                             vmem_limit_bytes=_vmem_limit(blocks, _nbytes((N, B), F32), 4 << 20)),
        name="wkv_step",
    )(lanes(r), lanes(k), lanes(v), lanes(lw), lanes(a),
      par(p["kk"]), par(p["ka"]), par(p["rk"]), par(p["lnx_g"]), par(p["lnx_b"]),
      jnp.transpose(state, (1, 2, 3, 0)))
    return y.reshape(D, B).T, jnp.transpose(s_new, (3, 0, 1, 2))


def _rwkv_post_body(x_ref, y_ref, g_ref, wo_ref, o_ref):
    o_ref[...] = x_ref[...] + _mm(y_ref[...] * g_ref[...], wo_ref[...])


def _rwkv_post(x, y, g, wo, *, tm):
    M, D = x.shape
    tok = pl.BlockSpec((tm, D), lambda i: (i, 0))
    streamed = 4 * _nbytes((tm, D), F32)
    return pl.pallas_call(
        _rwkv_post_body, grid=(M // tm,),
        in_specs=[tok, tok, tok, _resident(wo.shape, lambda i: (0, 0))],
        out_specs=tok, out_shape=jax.ShapeDtypeStruct((M, D), F32),
        compiler_params=pltpu.CompilerParams(
            dimension_semantics=("parallel",),
            vmem_limit_bytes=_vmem_limit(streamed, _nbytes(wo.shape, BF16), 3 * _nbytes((tm, D), F32))),
        name="rwkv_post",
    )(x, y, g, wo)


def _ret_pre_body(x_ref, lng_ref, cos_ref, sin_ref, wq_ref, wk_ref, wv_ref, wg_ref,
                  q_ref, k_ref, v_ref, gate_ref, h_scr):
    @pl.when(pl.program_id(1) == 0)
    def _():
        h_scr[...] = _rms(x_ref[...], lng_ref[...]).astype(BF16)

    h = h_scr[...]
    cos = cos_ref[...]
    sin = sin_ref[...]
    half = RET_DK // 2

    def rotate(t, out_ref):
        t1, t2 = t[:, :half], t[:, half:]
        out_ref[:, :half] = t1 * cos - t2 * sin
        out_ref[:, half:] = t1 * sin + t2 * cos

    rotate(jnp.dot(h, wq_ref[...], preferred_element_type=F32), q_ref)
    rotate(jnp.dot(h, wk_ref[...], preferred_element_type=F32) * (RET_DK ** -0.5), k_ref)
    v_ref[...] = jnp.dot(h, wv_ref[...], preferred_element_type=F32)
    gz = jnp.dot(h, wg_ref[...], preferred_element_type=F32)
    gate_ref[...] = gz * jax.nn.sigmoid(gz)


def _ret_pre(x, ln_g, cos, sin, p, *, tm):
    M, D = x.shape
    H, DK, DV = RET_HEADS, RET_DK, RET_DV
    n_tab = cos.shape[0] // tm
    tab = pl.BlockSpec((tm, DK // 2), lambda i, h: (i % n_tab, 0))
    col = lambda w: pl.BlockSpec((D, w), lambda i, h: (0, h))
    out = lambda w: pl.BlockSpec((tm, w), lambda i, h: (i, h))
    blocks = (_nbytes((tm, D), F32) + 2 * _nbytes((tm, DK // 2), F32) + 2 * _nbytes((D, DK), BF16)
              + 2 * _nbytes((D, DV), BF16) + 2 * _nbytes((tm, DK), F32) + 2 * _nbytes((tm, DV), F32))
    return pl.pallas_call(
        _ret_pre_body, grid=(M // tm, H),
        in_specs=[pl.BlockSpec((tm, D), lambda i, h: (i, 0)), pl.BlockSpec((1, D), lambda i, h: (0, 0)),
                  tab, tab, col(DK), col(DK), col(DV), col(DV)],
        out_specs=[out(DK), out(DK), out(DV), out(DV)],
        out_shape=[jax.ShapeDtypeStruct((M, H * DK), F32), jax.ShapeDtypeStruct((M, H * DK), F32),
                   jax.ShapeDtypeStruct((M, H * DV), F32), jax.ShapeDtypeStruct((M, H * DV), F32)],
        scratch_shapes=[pltpu.VMEM((tm, D), BF16)],
        compiler_params=pltpu.CompilerParams(
            dimension_semantics=("parallel", "arbitrary"),
            vmem_limit_bytes=_vmem_limit(blocks, _nbytes((tm, D), BF16), 4 * _nbytes((tm, DV), F32))),
        name="ret_pre",
    )(x, ln_g.reshape(1, D), cos, sin, p["wq"], p["wk"], p["wv"], p["wg"])


def _ret_log_gamma():
    return jnp.log1p(-jnp.exp2(-5.0 - jnp.arange(RET_HEADS, dtype=F32)))


def _ret_mix_body(lg_ref, x_ref, lng_ref, cos_ref, sin_ref, wq_ref, wk_ref, wv_ref, wg_ref, wo_ref,
                  out_ref, s_ref, *, n_sub):
    C, DK, DV, H = RET_CHUNK, RET_DK, RET_DV, RET_HEADS
    half = DK // 2

    @pl.when(pl.program_id(1) == 0)
    def _():
        s_ref[...] = jnp.zeros_like(s_ref)

    x = x_ref[...]
    h = _rms(x, lng_ref[...]).astype(BF16)
    cos = cos_ref[...]
    sin = sin_ref[...]
    ii = lax.broadcasted_iota(jnp.int32, (C, C), 0)
    jj = lax.broadcasted_iota(jnp.int32, (C, C), 1)
    diff = jnp.maximum((ii - jj).astype(F32), 0.0)
    idx = lax.broadcasted_iota(jnp.int32, (C, 1), 0).astype(F32)
    rows = lambda cc: slice(cc * C, (cc + 1) * C)

    def rotate(t):
        t1, t2 = t[:, :half], t[:, half:]
        return jnp.concatenate([t1 * cos - t2 * sin, t1 * sin + t2 * cos], axis=-1)

    def project(hd):
        qs, vs = slice(hd * DK, (hd + 1) * DK), slice(hd * DV, (hd + 1) * DV)
        q = rotate(jnp.dot(h, wq_ref[:, qs], preferred_element_type=F32))
        k = rotate(jnp.dot(h, wk_ref[:, qs], preferred_element_type=F32) * (DK ** -0.5))
        v = jnp.dot(h, wv_ref[:, vs], preferred_element_type=F32).astype(BF16)
        gz = jnp.dot(h, wg_ref[:, vs], preferred_element_type=F32)
        return q, k, v, gz * jax.nn.sigmoid(gz)

    def retain(hd, q, k, v):
        lg = lg_ref[hd]
        dmat = jnp.where(ii >= jj, jnp.exp(lg * diff), 0.0)
        q_dec = jnp.exp(lg * (idx + 1.0))
        k_dec = jnp.exp(lg * (C - 1.0 - idx))
        s_dec = jnp.exp(jnp.full((1, DV), lg * C, F32))
        subs = range(n_sub)
        inner = [_mm_nt(q[rows(cc)], k[rows(cc)]) * dmat for cc in subs]
        kv = [lax.dot_general((k[rows(cc)] * k_dec).astype(BF16), v[rows(cc)], TN_DIMS,
                              preferred_element_type=F32) for cc in subs]
        S = s_ref[0, hd]
        outs = []
        for cc in subs:
            outs.append(jnp.dot(inner[cc].astype(BF16), v[rows(cc)], preferred_element_type=F32)
                        + _mm(q[rows(cc)] * q_dec, S))
            S = S * s_dec + kv[cc]
        s_ref[0, hd] = S
        return jnp.concatenate(outs, axis=0)

    acc = x
    proj = project(0)
    for hd in range(H):
        nxt = project(hd + 1) if hd + 1 < H else None
        q, k, v, gate = proj
        o = retain(hd, q, k, v)
        acc = acc + _mm(gate * _head_norm(o, RET_GN_EPS), wo_ref[hd * DV:(hd + 1) * DV, :])
        proj = nxt
    out_ref[...] = acc


def _ret_mix(x, ln_g, cos, sin, p, *, n_seq, n_sub):
    M, D = x.shape
    H, DK, DV, C = RET_HEADS, RET_DK, RET_DV, RET_CHUNK
    T = C * n_sub
    nl = M // n_seq // T
    const = lambda b, l: (0, 0)
    tok = pl.BlockSpec((T, D), lambda b, l: (b * nl + l, 0))
    tab = pl.BlockSpec((T, DK // 2), lambda b, l: (l, 0))
    st = pl.BlockSpec((1, H, DK, DV), lambda b, l: (b, 0, 0, 0))
    weights = [p["wq"], p["wk"], p["wv"], p["wg"], p["wo"]]
    streamed = 2 * _nbytes((T, D), F32) + 2 * _nbytes((T, DK // 2), F32) + _nbytes((H, DK, DV), F32)
    resident = sum(_nbytes(w.shape, BF16) for w in weights)
    temps = 8 * _nbytes((T, DV), F32) + 4 * _nbytes((DK, DV), F32)
    return pl.pallas_call(
        functools.partial(_ret_mix_body, n_sub=n_sub), grid=(n_seq, nl),
        in_specs=[pl.BlockSpec(memory_space=pltpu.SMEM), tok, _resident((1, D), const), tab, tab]
                 + [_resident(w.shape, const) for w in weights],
        out_specs=[tok, st],
        out_shape=[jax.ShapeDtypeStruct((M, D), F32), jax.ShapeDtypeStruct((n_seq, H, DK, DV), F32)],
        compiler_params=pltpu.CompilerParams(dimension_semantics=("parallel", "arbitrary"),
                                             vmem_limit_bytes=_vmem_limit(streamed, resident, temps)),
        name="ret_mix",
    )(_ret_log_gamma(), x, ln_g.reshape(1, D), cos, sin, *weights)


def _ret_side(lg_ref, q_ref, k_ref, v_ref, s_ref, o_ref, so_ref):
    bt, H, DK = q_ref.shape
    q_cols = q_ref[...].reshape(bt * H, DK).T
    k_cols = k_ref[...].reshape(bt * H, DK).T

    def unit(b, hd):
        gam = jnp.exp(jnp.full((1, RET_DV), lg_ref[hd], F32))
        q = q_cols[:, b * H + hd:b * H + hd + 1]
        k = k_cols[:, b * H + hd:b * H + hd + 1]
        v = v_ref[b, hd:hd + 1, :]
        S = s_ref[b, hd]
        so_ref[b, hd] = S * gam + k * v
        o_ref[b, hd:hd + 1, :] = (gam * jnp.sum(q * S, axis=0, keepdims=True)
                                  + jnp.sum(q * k, axis=0, keepdims=True) * v)

    return [functools.partial(unit, b, hd) for b in range(bt) for hd in range(H)]


def _ret_post_body(x_ref, o_ref, gate_ref, wo_ref, out_ref):
    acc = x_ref[...]
    for hd in range(RET_HEADS):
        sl = slice(hd * RET_DV, (hd + 1) * RET_DV)
        acc = acc + _mm(gate_ref[:, sl] * _head_norm(o_ref[:, sl], RET_GN_EPS), wo_ref[sl, :])
    out_ref[...] = acc


def _ret_post(x, o, gate, wo, *, tm):
    M, D = x.shape
    W = o.shape[1]
    tok = lambda w: pl.BlockSpec((tm, w), lambda i: (i, 0))
    streamed = 2 * _nbytes((tm, D), F32) + 2 * _nbytes((tm, W), F32)
    return pl.pallas_call(
        _ret_post_body, grid=(M // tm,),
        in_specs=[tok(D), tok(W), tok(W), _resident(wo.shape, lambda i: (0, 0))],
        out_specs=tok(D), out_shape=jax.ShapeDtypeStruct((M, D), F32),
        compiler_params=pltpu.CompilerParams(
            dimension_semantics=("parallel",),
            vmem_limit_bytes=_vmem_limit(streamed, _nbytes(wo.shape, BF16), 3 * _nbytes((tm, W), F32))),
        name="ret_post",
    )(x, o, gate, wo)


def _rope_tables(pos):
    half = RET_DK // 2
    inv = ROPE_BASE ** (-jnp.arange(half, dtype=F32) / half)
    ang = pos.astype(F32)[:, None] * inv[None, :]
    return jnp.cos(ang), jnp.sin(ang)


def _rwkv_params(w, i):
    j = i // N_MIXERS
    p = {n: w["rw_" + n][j] for n in ("mu", "w0", "w1", "w2", "a0", "a1", "a2", "g1", "g2", "kk", "ka", "rk",
                                      "wr", "wk", "wv", "wo", "lnx_g", "lnx_b")}
    p["ln"] = w["ln_mix"][i]
    return p


def _ret_params(w, i):
    return {n: w["ret_" + n][i // N_MIXERS] for n in ("wq", "wk", "wv", "wg", "wo")}


def _ffn_of(w, which, i, tiles, tm="ffn_m"):
    (wg, layer), (wu, _), (wd, _) = (w[f"ff{which}_{n}"][i] for n in ("wg", "wu", "wd"))
    ln_final = w["ln_final"] if (which == 2 and i == DEPTH - 1) else None
    return functools.partial(_ffn, ln_g=w[f"ln_ffn{which}"][i], wg=wg, wu=wu, wd=wd, ln_final=ln_final,
                             layer=layer, tm=tiles[tm], tf=tiles["ffn_f"])


def _cast_late_weights(x, w, f32, tiles):
    flat = lambda t: t.reshape(-1, t.shape[-1])
    todo = []
    for n in FF2_STACKS:
        todo.append((n, None, (flat(f32[n]), 0, f32[n].shape[0] * f32[n].shape[1])))
    for n in FF1_STACKS:
        rows = f32[n].shape[1]
        todo += [(n, i, (flat(f32[n]), i * rows, rows)) for i in range(1, DEPTH)]
    for n in LATE_MIXER:
        todo += [(n, j, (t, 0, t.shape[0])) for j, t in enumerate(f32[n])]
    x, _, _, done = _ffn_of(w, 1, 0, tiles, tm="ffn_m_cast")(x, cast=[item for _, _, item in todo])
    for (n, i, _), t in zip(todo, done):
        if n in FF2_STACKS:
            stack = t.reshape(f32[n].shape)
            w[n] = [(stack, i) for i in range(DEPTH)]
        elif n in FF1_STACKS:
            w[n][i] = (t[None], 0)
        else:
            w[n][i] = t
    return x


def _prefill(x, n_seq, w, f32, tiles, start_side):
    M, D = x.shape
    cos, sin = _rope_tables(jnp.arange(M // n_seq, dtype=F32))
    new_shift, new_wkv, new_ret = [], [], []
    x = _cast_late_weights(x, w, f32, tiles)
    side = start_side(w)
    side_out = None

    def ffn(x, which, i):
        nonlocal side_out
        if side_out is not None:
            return _ffn_of(w, which, i, tiles)(x)
        x, *side_out, _ = _ffn_of(w, which, i, tiles, tm="ffn_m_side")(x, side=side)
        return x

    for i in range(DEPTH):
        if i > 0:
            x = ffn(x, 1, i)
        if i % N_MIXERS == 0:
            x, sh, st = _rwkv_mix(x, jnp.zeros((n_seq, D), F32), _rwkv_params(w, i), n_seq=n_seq,
                                  n_s=tiles["wkv_seqs"], T=tiles["wkv_tokens"])
            new_shift.append(sh)
            new_wkv.append(st)
        else:
            x, st = _ret_mix(x, w["ln_mix"][i], cos, sin, _ret_params(w, i), n_seq=n_seq, n_sub=tiles["ret_sub"])
            new_ret.append(st)
        x = ffn(x, 2, i)
    return (x, jnp.stack(new_shift), jnp.stack(new_wkv), jnp.stack(new_ret), *side_out)


def _decode(x, pos, shift0, wkv0, w, tiles):
    B, D = x.shape
    cos, sin = (jnp.tile(t, (B, 1)) for t in _rope_tables(pos))
    new_shift, new_wkv = [], []
    for i in range(DEPTH):
        x = _ffn_of(w, 1, i, tiles)(x)
        if i % N_MIXERS == 0:
            p = _rwkv_params(w, i)
            r, k, v, lw, a, g, sh = _rwkv_pre(x, shift0[i // N_MIXERS], p, tile=tiles["rwkv_pre"])
            y, st = _wkv_step(r, k, v, lw, a, p, wkv0[i // N_MIXERS])
            x = _rwkv_post(x, y, g, p["wo"], tm=tiles["post"])
            new_shift.append(sh)
            new_wkv.append(st)
        else:
            p = _ret_params(w, i)
            q, k, v, gate = _ret_pre(x, w["ln_mix"][i], cos, sin, p, tm=tiles["ret_pre"])
            o = yield q, k, v
            x = _ret_post(x, o, gate, p["wo"], tm=tiles["post"])
        x = _ffn_of(w, 2, i, tiles)(x)
    return x, jnp.stack(new_shift), jnp.stack(new_wkv)


PROMPT_TILES = dict(ffn_m=1024, ffn_m_cast=512, ffn_m_side=256, ffn_f=256, ret_sub=4, wkv_seqs=2, wkv_tokens=256)
SAMPLE_TILES = dict(ffn_m=128, ffn_f=256, rwkv_pre=128, post=128, ret_pre=128)

FF1_STACKS = ("ff1_wg", "ff1_wu", "ff1_wd")
FF2_STACKS = ("ff2_wg", "ff2_wu", "ff2_wd")
EARLY_MIXER = ("rw_w1", "rw_w2", "rw_a1", "rw_a2", "rw_g1", "rw_g2")
LATE_MIXER = ("rw_wr", "rw_wk", "rw_wv", "rw_wo", "ret_wq", "ret_wk", "ret_wv", "ret_wg", "ret_wo")


def kernel(x_prompt, x_sample, state_rwkv_shift, state_rwkv_wkv, state_ret, ln_ffn1, ff1_wg, ff1_wu, ff1_wd, ln_mix, ln_ffn2, ff2_wg, ff2_wu, ff2_wd, ln_final, rw_mu, rw_w0, rw_w1, rw_w2, rw_a0, rw_a1, rw_a2, rw_g1, rw_g2, rw_kk, rw_ka, rw_rk, rw_wr, rw_wk, rw_wv, rw_wo, rw_lnx_g, rw_lnx_b, ret_wq, ret_wk, ret_wv, ret_wg, ret_wo):
    w = dict(ln_ffn1=ln_ffn1, ff1_wg=ff1_wg, ff1_wu=ff1_wu, ff1_wd=ff1_wd, ln_mix=ln_mix,
             ln_ffn2=ln_ffn2, ff2_wg=ff2_wg, ff2_wu=ff2_wu, ff2_wd=ff2_wd, ln_final=ln_final,
             rw_mu=rw_mu, rw_w0=rw_w0, rw_w1=rw_w1, rw_w2=rw_w2, rw_a0=rw_a0, rw_a1=rw_a1,
             rw_a2=rw_a2, rw_g1=rw_g1, rw_g2=rw_g2, rw_kk=rw_kk, rw_ka=rw_ka, rw_rk=rw_rk,
             rw_wr=rw_wr, rw_wk=rw_wk, rw_wv=rw_wv, rw_wo=rw_wo, rw_lnx_g=rw_lnx_g,
             rw_lnx_b=rw_lnx_b, ret_wq=ret_wq, ret_wk=ret_wk, ret_wv=ret_wv, ret_wg=ret_wg,
             ret_wo=ret_wo)
    f32 = {n: w[n] for n in FF1_STACKS + FF2_STACKS}
    f32.update({n: list(w[n]) for n in LATE_MIXER})
    for n in FF1_STACKS:
        w[n] = [(w[n][:1].astype(BF16), 0)] + [None] * (DEPTH - 1)
    for n in LATE_MIXER:
        w[n] = [None] * len(f32[n])
    for n in EARLY_MIXER:
        w[n] = [t.astype(BF16) for t in w[n]]
    B, L, D = x_prompt.shape
    Bs, Ls, _ = x_sample.shape
    assert Ls == 1 and state_ret.shape[0] == 1, "one decode token and one retention layer are supported"
    H, DK, DV = RET_HEADS, RET_DK, RET_DV
    decode = None

    def start_decode(w):
        nonlocal decode
        decode = _decode(x_sample.reshape(Bs, D), PAST_LEN + jnp.arange(Ls, dtype=F32), state_rwkv_shift,
                         state_rwkv_wkv, w, SAMPLE_TILES)
        q, k, v = next(decode)
        return dict(q=q.reshape(Bs, H, DK), k=k.reshape(Bs, H, DK), v=v.reshape(Bs, H, DV), state=state_ret[0])

    y_p, p_shift, p_wkv, p_ret, o, s_ret = _prefill(x_prompt.reshape(B * L, D), B, w, f32, PROMPT_TILES,
                                                    start_decode)
    try:
        decode.send(o.reshape(Bs, H * DV))
        raise AssertionError("the decode group has a single retention layer")
    except StopIteration as done:
        y_s, s_shift, s_wkv = done.value
    return (y_p.reshape(B, L, D), y_s.reshape(Bs, Ls, D), p_shift, p_wkv, p_ret, s_shift, s_wkv, s_ret[None])
```

```python
import functools
import math

import numpy as np
import jax
import jax.numpy as jnp
from jax import lax
from jax.experimental import pallas as pl
from jax.experimental.pallas import tpu as pltpu

F32 = jnp.float32
BF16 = jnp.bfloat16

DEPTH = 2
N_MIXERS = 2
RW_HEADS = 16
RW_HEAD = 64
RW_GN_EPS = 64e-5
RET_HEADS = 4
RET_DK = 256
RET_DV = 512
RET_CHUNK = 128
RET_GN_EPS = 1e-6
ROPE_BASE = 10000.0
FFN_RES = 0.5
RMS_EPS = 1e-6
PAST_LEN = 16384

WKV_CHUNK = 64
V7X_VMEM_CAP_BYTES = 60000 * 1024
SUBLANES = 8
BF16_SUBLANES = 16

NT_DIMS = (((1,), (1,)), ((), ()))
TN_DIMS = (((0,), (0,)), ((), ()))


def _vmem_limit(block_bytes, scratch_bytes=0, temp_bytes=0):
    return int(min(V7X_VMEM_CAP_BYTES, 2 * block_bytes + scratch_bytes + temp_bytes + (4 << 20)))


def _nbytes(shape, dtype):
    return int(np.prod(shape)) * jnp.dtype(dtype).itemsize


def _resident(shape, index_map):
    return pl.BlockSpec(shape, index_map, pipeline_mode=pl.Buffered(1))


def _rms(x, g):
    return x * lax.rsqrt(jnp.mean(x * x, axis=-1, keepdims=True) + RMS_EPS) * g


def _head_norm(y, eps):
    mu = jnp.mean(y, axis=-1, keepdims=True)
    yc = y - mu
    return yc * lax.rsqrt(jnp.mean(yc * yc, axis=-1, keepdims=True) + eps)


def _mm(a, b):
    return jnp.dot(a.astype(BF16), b.astype(BF16), preferred_element_type=F32)


def _mm_nt(a, b):
    return lax.dot_general(a.astype(BF16), b.astype(BF16), NT_DIMS, preferred_element_type=F32)


def _mm_tn(a, b):
    return lax.dot_general(a.astype(BF16), b.astype(BF16), TN_DIMS, preferred_element_type=F32)


def _ffn_body(*refs, ff_tile, final, side, n_cast):
    refs = list(refs)
    x_ref, g_ref, wg_ref, wu_ref, wd_ref = refs[:5]
    del refs[:5]
    gf_ref = refs.pop(0) if final else None
    if side:
        side_in = refs[:5]
        del refs[:5]
    cast_in = refs[:n_cast]
    del refs[:n_cast]
    o_ref = refs.pop(0)
    cast_out = refs[len(refs) - n_cast:]
    del refs[len(refs) - n_cast:]
    x = x_ref[...]
    h = _rms(x, g_ref[...]).astype(BF16)
    n_tiles = wg_ref.shape[1] // ff_tile

    def activation(j):
        sl = slice(j * ff_tile, (j + 1) * ff_tile)
        gate = jnp.dot(h, wg_ref[:, sl], preferred_element_type=F32)
        up = jnp.dot(h, wu_ref[:, sl], preferred_element_type=F32)
        return (gate * jax.nn.sigmoid(gate) * up).astype(BF16)

    side_units = _ret_side(*side_in, *refs) if side else []
    per_tile = -(-len(side_units) // n_tiles)
    act = activation(0)
    acc = None
    for j in range(n_tiles):
        nxt = activation(j + 1) if j + 1 < n_tiles else None
        part = jnp.dot(act, wd_ref[j * ff_tile:(j + 1) * ff_tile, :], preferred_element_type=F32)
        acc = part if acc is None else acc + part
        act = nxt
        for unit in side_units[j * per_tile:(j + 1) * per_tile]:
            unit()
    y = x + FFN_RES * acc
    if final:
        y = _rms(y, gf_ref[...])
    o_ref[...] = y
    for src, dst in zip(cast_in, cast_out):
        dst[...] = src[...].astype(BF16)


def _ffn(x, ln_g, wg, wu, wd, ln_final=None, *, layer, tm, tf, side=None, cast=()):
    M, D = x.shape
    FF = wg.shape[2]
    final = ln_final is not None
    steps = M // tm
    row = lambda i: (i, 0)
    const = lambda i: (0, 0)
    of_layer = lambda i: (layer, 0, 0)
    in_specs = [pl.BlockSpec((tm, D), row), _resident((1, D), const), _resident((None, D, FF), of_layer),
                _resident((None, D, FF), of_layer), _resident((None, FF, D), of_layer)]
    args = [x, ln_g.reshape(1, D), wg, wu, wd]
    if final:
        in_specs.append(_resident((1, D), const))
        args.append(ln_final.reshape(1, D))
    out_specs = [pl.BlockSpec((tm, D), row)]
    out_shape = [jax.ShapeDtypeStruct((M, D), F32)]
    streamed = 2 * _nbytes((tm, D), F32)
    if side is not None:
        B, H, DK, DV = side["state"].shape
        bt = B // steps
        assert bt * steps == B, "every grid step takes the same number of sequences"
        taken = lambda i: (i, 0, 0)
        st = pl.BlockSpec((bt, H, DK, DV), lambda i: (i, 0, 0, 0))
        in_specs += [pl.BlockSpec(memory_space=pltpu.SMEM), pl.BlockSpec((bt, H, DK), taken),
                     pl.BlockSpec((bt, H, DK), taken), pl.BlockSpec((bt, H, DV), taken), st]
        args += [_ret_log_gamma(), side["q"], side["k"], side["v"], side["state"]]
        out_specs += [pl.BlockSpec((bt, H, DV), taken), st]
        out_shape += [jax.ShapeDtypeStruct((B, H, DV), F32), jax.ShapeDtypeStruct((B, H, DK, DV), F32)]
        streamed += 2 * _nbytes((bt, H, DK, DV), F32) + 4 * _nbytes((bt, SUBLANES, DV), F32)
    for t, first_row, n_rows in cast:
        span = next(s for s in (1, 2, 4) if (n_rows * s) % (steps * BF16_SUBLANES) == 0)
        rb = n_rows * span // steps
        cols = t.shape[1]
        assert first_row % rb == 0
        in_specs.append(pl.BlockSpec((rb, cols), lambda i, first=first_row // rb, span=span: (first + i // span, 0)))
        args.append(t)
        out_specs.append(pl.BlockSpec((rb, cols), lambda i, span=span: (i // span, 0)))
        out_shape.append(jax.ShapeDtypeStruct((n_rows, cols), BF16))
        streamed += _nbytes((rb, cols), F32) + _nbytes((rb, cols), BF16)
    resident = 3 * _nbytes((D, FF), BF16)
    temps = 6 * _nbytes((tm, tf), F32) + 3 * _nbytes((tm, D), F32)
    out = pl.pallas_call(
        functools.partial(_ffn_body, ff_tile=tf, final=final, side=side is not None, n_cast=len(cast)),
        grid=(steps,), in_specs=in_specs, out_specs=out_specs, out_shape=out_shape,
        compiler_params=pltpu.CompilerParams(dimension_semantics=("arbitrary",) if cast else ("parallel",),
                                             vmem_limit_bytes=_vmem_limit(streamed, resident, temps)),
        name="ffn_final" if final else "ffn",
    )(*args)
    if side is None and not cast:
        return out[0]
    o, state = out[1:3] if side is not None else (None, None)
    return out[0], o, state, tuple(out[len(out) - len(cast):])


def _rwkv_project(h, xprev, mu_ref, w0_ref, a0_ref, wr_ref, wk_ref, wv_ref, w1_ref, w2_ref, a1_ref, a2_ref,
                  g1_ref, g2_ref):
    xx = xprev - h
    xr, xw, xk, xv, xa, xg = (h + xx * mu_ref[i:i + 1, :] for i in range(6))
    r = _mm(xr, wr_ref[...])
    k = _mm(xk, wk_ref[...])
    v = _mm(xv, wv_ref[...])
    z = w0_ref[...] + _mm(jnp.tanh(_mm(xw, w1_ref[...])), w2_ref[...])
    lw = -math.exp(-0.5) * jax.nn.sigmoid(z)
    a = jax.nn.sigmoid(a0_ref[...] + _mm(_mm(xa, a1_ref[...]), a2_ref[...]))
    g = _mm(jax.nn.sigmoid(_mm(xg, g1_ref[...])), g2_ref[...])
    return r, k, v, lw, a, g


def _rwkv_pre_body(x_ref, lng_ref, prev_ref, *refs):
    weight_refs, out_refs = refs[:12], refs[12:]
    h = _rms(x_ref[...], lng_ref[...])
    for ref, val in zip(out_refs, _rwkv_project(h, prev_ref[...], *weight_refs) + (h,)):
        ref[...] = val


def _rwkv_pre(x, prev, p, *, tile):
    M, D = x.shape
    vec = lambda a: a.reshape(1, D)
    row = lambda i: (i, 0)
    tok = pl.BlockSpec((tile, D), row)
    full = lambda a: _resident(a.shape, lambda i: (0, 0))
    params = [p["mu"], vec(p["w0"]), vec(p["a0"])]
    weights = [p["wr"], p["wk"], p["wv"], p["w1"], p["w2"], p["a1"], p["a2"], p["g1"], p["g2"]]
    streamed = 9 * _nbytes((tile, D), F32)
    resident = sum(_nbytes(w.shape, BF16) for w in weights)
    return pl.pallas_call(
        _rwkv_pre_body, grid=(M // tile,),
        in_specs=[tok, full(vec(p["ln"])), tok] + [full(t) for t in params + weights],
        out_specs=[tok] * 7,
        out_shape=[jax.ShapeDtypeStruct((M, D), F32)] * 7,
        compiler_params=pltpu.CompilerParams(
            dimension_semantics=("parallel",),
            vmem_limit_bytes=_vmem_limit(streamed, resident, 10 * _nbytes((tile, D), F32))),
        name="rwkv_pre",
    )(x, vec(p["ln"]), prev, *params, *weights)


def _wkv_chunks(r, k, v, lw, a, kkw, kaw, rkw, lnx_g, lnx_b, S, n_s):
    C = WKV_CHUNK
    N = RW_HEAD
    W = 2 * N
    n_g = r.shape[0] // C
    n_c = n_g // n_s
    ii = lax.broadcasted_iota(jnp.int32, (2 * C, 2 * C), 0)
    jj = lax.broadcasted_iota(jnp.int32, (2 * C, 2 * C), 1)
    tri = ((ii >= jj) & ((ii < C) == (jj < C))).astype(BF16)
    lw_hi = lw.astype(BF16)
    rest = lw - lw_hi.astype(F32)
    lw_mid = rest.astype(BF16)
    lw_lo = (rest - lw_mid.astype(F32)).astype(BF16)
    cl = jnp.concatenate(
        [sum(jnp.dot(tri, piece[b:b + 2 * C], preferred_element_type=F32) for piece in (lw_hi, lw_mid, lw_lo))
         for b in range(0, n_g * C, 2 * C)], axis=0)
    gls = [cl[g * C + C - 1:g * C + C, :] for g in range(n_g)]
    gl = jnp.concatenate([jnp.broadcast_to(t, (C, t.shape[-1])) for t in gls], axis=0)
    e_cl = jnp.exp(cl)
    e_ncl = jnp.exp(-cl)
    e_ce = jnp.exp(cl - lw)
    e_g = jnp.exp(gl - cl)
    dgs = [jnp.exp(t) for t in gls]
    kkraw = k * kkw
    kmod = k * (1.0 + (a - 1.0) * kaw)
    rkk = r * kmod * rkw
    rt = r * e_cl
    kt = kmod * e_ncl
    kg = kmod * e_g
    row = lax.broadcasted_iota(jnp.int32, (C, W), 0)
    lane = lax.broadcasted_iota(jnp.int32, (C, W), 1)
    left = lane < N
    tok = jnp.where(left, lane, lane - N)
    strict = row > tok
    incl2 = jnp.concatenate([row >= tok] * 2, axis=1)
    eye = (row == tok).astype(F32)
    same_head = ((lax.broadcasted_iota(jnp.int32, (W, W), 0) < N)
                 == (lax.broadcasted_iota(jnp.int32, (W, W), 1) < N))

    def seg_sum(x):
        s_l = jnp.sum(jnp.where(left, x, 0.0), axis=-1, keepdims=True)
        s_r = jnp.sum(jnp.where(left, 0.0, x), axis=-1, keepdims=True)
        return jnp.where(left, s_l, s_r)

    def bd(x):
        xb = x.astype(BF16)
        zero = jnp.zeros_like(xb)
        return jnp.concatenate([jnp.where(left, xb, zero), jnp.where(left, zero, xb)], axis=0)

    def mm(x, y):
        return jnp.dot(x.astype(BF16), y, preferred_element_type=F32)

    pairs = range(RW_HEADS // 2)
    at = {(g, p): (slice(g * C, (g + 1) * C), slice(p * W, (p + 1) * W)) for g in range(n_g) for p in pairs}
    V2 = {u: v[at[u]] for u in at}
    lhs2, bk, bonus, bd_v, nak_v, p_bot, inv = {}, {}, {}, {}, {}, {}, {}

    def phase_a(units):
        rhs_nt = {}
        for u in units:
            kk2 = kkraw[at[u]]
            kkn = kk2 / jnp.maximum(jnp.sqrt(seg_sum(kk2 * kk2)), 1e-12)
            b2 = kkn * a[at[u]]
            lhs2[u] = jnp.concatenate([-kkn * e_ce[at[u]], rt[at[u]]], axis=0).astype(BF16)
            rhs_nt[u] = jnp.concatenate([bd(b2 * e_ncl[at[u]]), bd(kt[at[u]])], axis=0)
            bk[u] = jnp.concatenate([b2 * e_g[at[u]], kg[at[u]]], axis=0).astype(BF16)
            bonus[u] = seg_sum(rkk[at[u]]) * V2[u]
        P = {u: lax.dot_general(lhs2[u], rhs_nt[u], NT_DIMS, preferred_element_type=F32) for u in units}
        for u in units:
            bd_v[u] = bd(V2[u])
            nak_v[u] = mm(jnp.where(strict, P[u][:C, W:], 0.0), bd_v[u])
            p_bot[u] = jnp.where(incl2, P[u][C:], 0.0).astype(BF16)
        Pk = {u: jnp.where(strict, P[u][:C, :W], 0.0) for u in units}
        T = {u: eye + Pk[u] for u in units}
        Pk = {u: mm(Pk[u], bd(Pk[u])) for u in units}
        span = 2
        while span < C:
            span *= 2
            if span < C:
                Z = {u: mm(Pk[u], jnp.concatenate([bd(T[u]), bd(Pk[u])], axis=1)) for u in units}
                T = {u: T[u] + Z[u][:, :W] for u in units}
                Pk = {u: Z[u][:, W:] for u in units}
            else:
                T = {u: T[u] + mm(Pk[u], bd(T[u])) for u in units}
        for u in units:
            inv[u] = T[u].astype(BF16)

    for s in range(n_s):
        phase_a([(g, p) for g in range(s * n_c, (s + 1) * n_c) for p in pairs])
    y = {}
    for c in range(n_c):
        now = [((s, p), (s * n_c + c, p)) for s in range(n_s) for p in pairs]
        LS = {u: lax.dot_general(lhs2[gu], S[u].astype(BF16), NT_DIMS, preferred_element_type=F32)
              for u, gu in now}
        U = {u: jnp.dot(inv[gu], bd(LS[u][:C] + nak_v[gu]), preferred_element_type=F32) for u, gu in now}
        Y = {u: LS[u][C:] + jnp.dot(p_bot[gu], jnp.concatenate([bd(U[u]), bd_v[gu]], axis=0),
                                    preferred_element_type=F32) for u, gu in now}
        S = dict(S)
        for u, gu in now:
            S[u] = (S[u] * dgs[gu[0]][:, at[gu][1]]
                    + jnp.where(same_head, _mm_tn(jnp.concatenate([U[u], V2[gu]], axis=0), bk[gu]), 0.0))
        for u, gu in now:
            sl = at[gu][1]
            yc = Y[u] - seg_sum(Y[u]) * (1.0 / N)
            yn = yc * lax.rsqrt(seg_sum(yc * yc) * (1.0 / N) + RW_GN_EPS)
            y[(u[0], c, u[1])] = yn * lnx_g[:, sl] + lnx_b[:, sl] + bonus[gu]
    return y, S


def _rwkv_mix_body(x_ref, lng_ref, prev_ref, mu_ref, w0_ref, a0_ref, kkw_ref, kaw_ref, rk_ref, lg_ref, lb_ref,
                   wr_ref, wk_ref, wv_ref, wo_ref, w1_ref, w2_ref, a1_ref, a2_ref, g1_ref, g2_ref,
                   out_ref, so_ref, s_ref,
                   carry, sbd_scr, y_scr, *, n_steps):
    C, N, W = WKV_CHUNK, RW_HEAD, 2 * RW_HEAD
    n_s, T, D = x_ref.shape
    step = pl.program_id(1)

    @pl.when(step == 0)
    def _():
        carry[...] = prev_ref[...]
        sbd_scr[...] = jnp.zeros_like(sbd_scr)

    x = x_ref[...].reshape(n_s * T, D)
    h = _rms(x, lng_ref[...])
    row = lax.broadcasted_iota(jnp.int32, h.shape, 0)
    xprev = pltpu.roll(h, 1, 0)
    for s in range(n_s):
        xprev = jnp.where(row == s * T, carry[s], xprev)
        last = h[s * T + T - 1:s * T + T, :]
        carry[s] = last
        so_ref[s] = last
    r, k, v, lw, a, g = _rwkv_project(
        h, xprev, mu_ref, w0_ref, a0_ref, wr_ref, wk_ref, wv_ref, w1_ref, w2_ref, a1_ref, a2_ref, g1_ref, g2_ref)
    units = [(s, p) for s in range(n_s) for p in range(RW_HEADS // 2)]
    S = {u: sbd_scr[u[0], u[1]] for u in units}
    y, S = _wkv_chunks(r, k, v, lw, a, kkw_ref[...], kaw_ref[...], rk_ref[...], lg_ref[...], lb_ref[...], S, n_s)
    for (s, c, p), t in y.items():
        y_scr[s * T + c * C:s * T + (c + 1) * C, p * W:(p + 1) * W] = t
    for u in units:
        sbd_scr[u[0], u[1]] = S[u]
    out_ref[...] = (x + _mm(y_scr[...] * g, wo_ref[...])).reshape(n_s, T, D)

    @pl.when(step == n_steps - 1)
    def _():
        for s, p in units:
            S = sbd_scr[s, p]
            s_ref[s, 2 * p] = S[:N, :N]
            s_ref[s, 2 * p + 1] = S[N:, N:]


def _rwkv_mix(x, prev, p, *, n_seq, n_s, T):
    M, D = x.shape
    L = M // n_seq
    nl = L // T
    H, N = RW_HEADS, RW_HEAD
    vec = lambda t: t.reshape(1, D)
    const = lambda b, l: (0, 0)
    tok = pl.BlockSpec((n_s, T, D), lambda b, l: (b, l, 0))
    row1 = pl.BlockSpec((n_s, 1, D), lambda b, l: (b, 0, 0))
    st = pl.BlockSpec((n_s, H, N, N), lambda b, l: (b, 0, 0, 0))
    params = [vec(p["ln"]), p["mu"], vec(p["w0"]), vec(p["a0"]), vec(p["kk"]), vec(p["ka"]), vec(p["rk"]),
              vec(p["lnx_g"]), vec(p["lnx_b"])]
    weights = [p["wr"], p["wk"], p["wv"], p["wo"], p["w1"], p["w2"], p["a1"], p["a2"], p["g1"], p["g2"]]
    full = lambda t: _resident(t.shape, const)
    in_specs = [tok, full(params[0]), row1] + [full(t) for t in params[1:]] + [full(t) for t in weights]
    rows = (n_s * T, D)
    pair_state = (n_s, H // 2, 2 * N, 2 * N)
    scratch = [pltpu.VMEM((n_s, 1, D), F32), pltpu.VMEM(pair_state, F32), pltpu.VMEM(rows, F32)]
    streamed = 2 * _nbytes((n_s, T, D), F32) + _nbytes((n_s, H, N, 128), F32)
    resident = sum(_nbytes(t.shape, BF16) for t in weights) + _nbytes(pair_state, F32) + _nbytes(rows, F32)
    out, so, state = pl.pallas_call(
        functools.partial(_rwkv_mix_body, n_steps=nl), grid=(n_seq // n_s, nl),
        in_specs=in_specs, out_specs=[tok, row1, st],
        out_shape=[jax.ShapeDtypeStruct((n_seq, L, D), F32), jax.ShapeDtypeStruct((n_seq, 1, D), F32),
                   jax.ShapeDtypeStruct((n_seq, H, N, N), F32)],
        scratch_shapes=scratch,
        compiler_params=pltpu.CompilerParams(dimension_semantics=("parallel", "arbitrary"),
                                             vmem_limit_bytes=_vmem_limit(streamed, resident, 12 * _nbytes(rows, F32))),
        name="rwkv_mix",
    )(x.reshape(n_seq, L, D), params[0], prev.reshape(n_seq, 1, D), *params[1:], *weights)
    return out.reshape(M, D), so.reshape(n_seq, D), state


def _wkv_step_body(r_ref, k_ref, v_ref, lw_ref, a_ref, kkw_ref, kaw_ref, rk_ref, lg_ref, lb_ref, s_ref,
                   y_ref, so_ref, y_scr):
    r = r_ref[0]
    k = k_ref[0]
    v = v_ref[0]
    a = a_ref[0]
    d = jnp.exp(lw_ref[0])
    kkh = k * kkw_ref[0]
    nrm = jnp.sqrt(jnp.sum(kkh * kkh, axis=0, keepdims=True))
    kkn = kkh / jnp.maximum(nrm, 1e-12)
    kmod = k * (1.0 + (a - 1.0) * kaw_ref[0])
    bh = kkn * a

    def value_channel(i, carry):
        S = s_ref[0, i]
        sa = -jnp.sum(S * kkn, axis=0, keepdims=True)
        s_new = S * d + sa * bh + v_ref[0, pl.ds(i, 1), :] * kmod
        so_ref[0, i] = s_new
        y_scr[pl.ds(i, 1), :] = jnp.sum(s_new * r, axis=0, keepdims=True)
        return carry

    lax.fori_loop(0, RW_HEAD, value_channel, 0, unroll=8)
    y = y_scr[...]
    mu = jnp.mean(y, axis=0, keepdims=True)
    yc = y - mu
    yn = yc * lax.rsqrt(jnp.mean(yc * yc, axis=0, keepdims=True) + RW_GN_EPS)
    y_ref[0] = yn * lg_ref[0] + lb_ref[0] + jnp.sum(r * kmod * rk_ref[0], axis=0, keepdims=True) * v


def _wkv_step(r, k, v, lw, a, p, state):
    B, D = r.shape
    H, N = RW_HEADS, RW_HEAD
    lanes = lambda t: t.T.reshape(H, N, B)
    par = lambda t: jnp.broadcast_to(t.reshape(H, N, 1), (H, N, B))
    vec = pl.BlockSpec((1, N, B), lambda h: (h, 0, 0))
    st = pl.BlockSpec((1, N, N, B), lambda h: (h, 0, 0, 0))
    blocks = 2 * _nbytes((N, N, B), F32) + 11 * _nbytes((N, B), F32)
    y, s_new = pl.pallas_call(
        _wkv_step_body, grid=(H,),
        in_specs=[vec] * 10 + [st],
        out_specs=[vec, st],
        out_shape=[jax.ShapeDtypeStruct((H, N, B), F32), jax.ShapeDtypeStruct((H, N, N, B), F32)],
        scratch_shapes=[pltpu.VMEM((N, B), F32)],
        compiler_params=pltpu.CompilerParams(dimension_semantics=("parallel",),
                                             vmem_limit_bytes=_vmem_limit(blocks, _nbytes((N, B), F32), 4 << 20)),
        name="wkv_step",
    )(lanes(r), lanes(k), lanes(v), lanes(lw), lanes(a),
      par(p["kk"]), par(p["ka"]), par(p["rk"]), par(p["lnx_g"]), par(p["lnx_b"]),
      jnp.transpose(state, (1, 2, 3, 0)))
    return y.reshape(D, B).T, jnp.transpose(s_new, (3, 0, 1, 2))


def _rwkv_post_body(x_ref, y_ref, g_ref, wo_ref, o_ref):
    o_ref[...] = x_ref[...] + _mm(y_ref[...] * g_ref[...], wo_ref[...])


def _rwkv_post(x, y, g, wo, *, tm):
    M, D = x.shape
    tok = pl.BlockSpec((tm, D), lambda i: (i, 0))
    streamed = 4 * _nbytes((tm, D), F32)
    return pl.pallas_call(
        _rwkv_post_body, grid=(M // tm,),
        in_specs=[tok, tok, tok, _resident(wo.shape, lambda i: (0, 0))],
        out_specs=tok, out_shape=jax.ShapeDtypeStruct((M, D), F32),
        compiler_params=pltpu.CompilerParams(
            dimension_semantics=("parallel",),
            vmem_limit_bytes=_vmem_limit(streamed, _nbytes(wo.shape, BF16), 3 * _nbytes((tm, D), F32))),
        name="rwkv_post",
    )(x, y, g, wo)


def _ret_pre_body(x_ref, lng_ref, cos_ref, sin_ref, wq_ref, wk_ref, wv_ref, wg_ref,
                  q_ref, k_ref, v_ref, gate_ref, h_scr):
    @pl.when(pl.program_id(1) == 0)
    def _():
        h_scr[...] = _rms(x_ref[...], lng_ref[...]).astype(BF16)

    h = h_scr[...]
    cos = cos_ref[...]
    sin = sin_ref[...]
    half = RET_DK // 2

    def rotate(t, out_ref):
        t1, t2 = t[:, :half], t[:, half:]
        out_ref[:, :half] = t1 * cos - t2 * sin
        out_ref[:, half:] = t1 * sin + t2 * cos

    rotate(jnp.dot(h, wq_ref[...], preferred_element_type=F32), q_ref)
    rotate(jnp.dot(h, wk_ref[...], preferred_element_type=F32) * (RET_DK ** -0.5), k_ref)
    v_ref[...] = jnp.dot(h, wv_ref[...], preferred_element_type=F32)
    gz = jnp.dot(h, wg_ref[...], preferred_element_type=F32)
    gate_ref[...] = gz * jax.nn.sigmoid(gz)


def _ret_pre(x, ln_g, cos, sin, p, *, tm):
    M, D = x.shape
    H, DK, DV = RET_HEADS, RET_DK, RET_DV
    n_tab = cos.shape[0] // tm
    tab = pl.BlockSpec((tm, DK // 2), lambda i, h: (i % n_tab, 0))
    col = lambda w: pl.BlockSpec((D, w), lambda i, h: (0, h))
    out = lambda w: pl.BlockSpec((tm, w), lambda i, h: (i, h))
    blocks = (_nbytes((tm, D), F32) + 2 * _nbytes((tm, DK // 2), F32) + 2 * _nbytes((D, DK), BF16)
              + 2 * _nbytes((D, DV), BF16) + 2 * _nbytes((tm, DK), F32) + 2 * _nbytes((tm, DV), F32))
    return pl.pallas_call(
        _ret_pre_body, grid=(M // tm, H),
        in_specs=[pl.BlockSpec((tm, D), lambda i, h: (i, 0)), pl.BlockSpec((1, D), lambda i, h: (0, 0)),
                  tab, tab, col(DK), col(DK), col(DV), col(DV)],
        out_specs=[out(DK), out(DK), out(DV), out(DV)],
        out_shape=[jax.ShapeDtypeStruct((M, H * DK), F32), jax.ShapeDtypeStruct((M, H * DK), F32),
                   jax.ShapeDtypeStruct((M, H * DV), F32), jax.ShapeDtypeStruct((M, H * DV), F32)],
        scratch_shapes=[pltpu.VMEM((tm, D), BF16)],
        compiler_params=pltpu.CompilerParams(
            dimension_semantics=("parallel", "arbitrary"),
            vmem_limit_bytes=_vmem_limit(blocks, _nbytes((tm, D), BF16), 4 * _nbytes((tm, DV), F32))),
        name="ret_pre",
    )(x, ln_g.reshape(1, D), cos, sin, p["wq"], p["wk"], p["wv"], p["wg"])


def _ret_log_gamma():
    return jnp.log1p(-jnp.exp2(-5.0 - jnp.arange(RET_HEADS, dtype=F32)))


def _ret_mix_body(lg_ref, x_ref, lng_ref, cos_ref, sin_ref, wq_ref, wk_ref, wv_ref, wg_ref, wo_ref,
                  out_ref, s_ref, *, n_sub):
    C, DK, DV, H = RET_CHUNK, RET_DK, RET_DV, RET_HEADS
    half = DK // 2

    @pl.when(pl.program_id(1) == 0)
    def _():
        s_ref[...] = jnp.zeros_like(s_ref)

    x = x_ref[...]
    h = _rms(x, lng_ref[...]).astype(BF16)
    cos = cos_ref[...]
    sin = sin_ref[...]
    ii = lax.broadcasted_iota(jnp.int32, (C, C), 0)
    jj = lax.broadcasted_iota(jnp.int32, (C, C), 1)
    diff = jnp.maximum((ii - jj).astype(F32), 0.0)
    idx = lax.broadcasted_iota(jnp.int32, (C, 1), 0).astype(F32)
    rows = lambda cc: slice(cc * C, (cc + 1) * C)

    def rotate(t):
        t1, t2 = t[:, :half], t[:, half:]
        return jnp.concatenate([t1 * cos - t2 * sin, t1 * sin + t2 * cos], axis=-1)

    def project(hd):
        qs, vs = slice(hd * DK, (hd + 1) * DK), slice(hd * DV, (hd + 1) * DV)
        q = rotate(jnp.dot(h, wq_ref[:, qs], preferred_element_type=F32))
        k = rotate(jnp.dot(h, wk_ref[:, qs], preferred_element_type=F32) * (DK ** -0.5))
        v = jnp.dot(h, wv_ref[:, vs], preferred_element_type=F32).astype(BF16)
        gz = jnp.dot(h, wg_ref[:, vs], preferred_element_type=F32)
        return q, k, v, gz * jax.nn.sigmoid(gz)

    def retain(hd, q, k, v):
        lg = lg_ref[hd]
        dmat = jnp.where(ii >= jj, jnp.exp(lg * diff), 0.0)
        q_dec = jnp.exp(lg * (idx + 1.0))
        k_dec = jnp.exp(lg * (C - 1.0 - idx))
        s_dec = jnp.exp(jnp.full((1, DV), lg * C, F32))
        subs = range(n_sub)
        inner = [_mm_nt(q[rows(cc)], k[rows(cc)]) * dmat for cc in subs]
        kv = [lax.dot_general((k[rows(cc)] * k_dec).astype(BF16), v[rows(cc)], TN_DIMS,
                              preferred_element_type=F32) for cc in subs]
        S = s_ref[0, hd]
        outs = []
        for cc in subs:
            outs.append(jnp.dot(inner[cc].astype(BF16), v[rows(cc)], preferred_element_type=F32)
                        + _mm(q[rows(cc)] * q_dec, S))
            S = S * s_dec + kv[cc]
        s_ref[0, hd] = S
        return jnp.concatenate(outs, axis=0)

    acc = x
    proj = {hd: project(hd) for hd in range(min(2, H))}
    for hd in range(H):
        q, k, v, gate = proj.pop(hd)
        o = retain(hd, q, k, v)
        if hd + 2 < H:
            proj[hd + 2] = project(hd + 2)
        acc = acc + _mm(gate * _head_norm(o, RET_GN_EPS), wo_ref[hd * DV:(hd + 1) * DV, :])
    out_ref[...] = acc


def _ret_mix(x, ln_g, cos, sin, p, *, n_seq, n_sub):
    M, D = x.shape
    H, DK, DV, C = RET_HEADS, RET_DK, RET_DV, RET_CHUNK
    T = C * n_sub
    nl = M // n_seq // T
    const = lambda b, l: (0, 0)
    tok = pl.BlockSpec((T, D), lambda b, l: (b * nl + l, 0))
    tab = pl.BlockSpec((T, DK // 2), lambda b, l: (l, 0))
    st = pl.BlockSpec((1, H, DK, DV), lambda b, l: (b, 0, 0, 0))
    weights = [p["wq"], p["wk"], p["wv"], p["wg"], p["wo"]]
    streamed = 2 * _nbytes((T, D), F32) + 2 * _nbytes((T, DK // 2), F32) + _nbytes((H, DK, DV), F32)
    resident = sum(_nbytes(w.shape, BF16) for w in weights)
    temps = 8 * _nbytes((T, DV), F32) + 4 * _nbytes((DK, DV), F32)
    return pl.pallas_call(
        functools.partial(_ret_mix_body, n_sub=n_sub), grid=(n_seq, nl),
        in_specs=[pl.BlockSpec(memory_space=pltpu.SMEM), tok, _resident((1, D), const), tab, tab]
                 + [_resident(w.shape, const) for w in weights],
        out_specs=[tok, st],
        out_shape=[jax.ShapeDtypeStruct((M, D), F32), jax.ShapeDtypeStruct((n_seq, H, DK, DV), F32)],
        compiler_params=pltpu.CompilerParams(dimension_semantics=("parallel", "arbitrary"),
                                             vmem_limit_bytes=_vmem_limit(streamed, resident, temps)),
        name="ret_mix",
    )(_ret_log_gamma(), x, ln_g.reshape(1, D), cos, sin, *weights)


def _ret_side(lg_ref, q_ref, k_ref, v_ref, s_ref, o_ref, so_ref):
    bt, H, DK = q_ref.shape
    q_cols = q_ref[...].reshape(bt * H, DK).T
    k_cols = k_ref[...].reshape(bt * H, DK).T

    def unit(b, hd):
        gam = jnp.exp(jnp.full((1, RET_DV), lg_ref[hd], F32))
        q = q_cols[:, b * H + hd:b * H + hd + 1]
        k = k_cols[:, b * H + hd:b * H + hd + 1]
        v = v_ref[b, hd:hd + 1, :]
        S = s_ref[b, hd]
        so_ref[b, hd] = S * gam + k * v
        o_ref[b, hd:hd + 1, :] = (gam * jnp.sum(q * S, axis=0, keepdims=True)
                                  + jnp.sum(q * k, axis=0, keepdims=True) * v)

    return [functools.partial(unit, b, hd) for b in range(bt) for hd in range(H)]


def _ret_post_body(x_ref, o_ref, gate_ref, wo_ref, out_ref):
    acc = x_ref[...]
    for hd in range(RET_HEADS):
        sl = slice(hd * RET_DV, (hd + 1) * RET_DV)
        acc = acc + _mm(gate_ref[:, sl] * _head_norm(o_ref[:, sl], RET_GN_EPS), wo_ref[sl, :])
    out_ref[...] = acc


def _ret_post(x, o, gate, wo, *, tm):
    M, D = x.shape
    W = o.shape[1]
    tok = lambda w: pl.BlockSpec((tm, w), lambda i: (i, 0))
    streamed = 2 * _nbytes((tm, D), F32) + 2 * _nbytes((tm, W), F32)
    return pl.pallas_call(
        _ret_post_body, grid=(M // tm,),
        in_specs=[tok(D), tok(W), tok(W), _resident(wo.shape, lambda i: (0, 0))],
        out_specs=tok(D), out_shape=jax.ShapeDtypeStruct((M, D), F32),
        compiler_params=pltpu.CompilerParams(
            dimension_semantics=("parallel",),
            vmem_limit_bytes=_vmem_limit(streamed, _nbytes(wo.shape, BF16), 3 * _nbytes((tm, W), F32))),
        name="ret_post",
    )(x, o, gate, wo)


def _rope_tables(pos):
    half = RET_DK // 2
    inv = ROPE_BASE ** (-jnp.arange(half, dtype=F32) / half)
    ang = pos.astype(F32)[:, None] * inv[None, :]
    return jnp.cos(ang), jnp.sin(ang)


def _rwkv_params(w, i):
    j = i // N_MIXERS
    p = {n: w["rw_" + n][j] for n in ("mu", "w0", "w1", "w2", "a0", "a1", "a2", "g1", "g2", "kk", "ka", "rk",
                                      "wr", "wk", "wv", "wo", "lnx_g", "lnx_b")}
    p["ln"] = w["ln_mix"][i]
    return p


def _ret_params(w, i):
    return {n: w["ret_" + n][i // N_MIXERS] for n in ("wq", "wk", "wv", "wg", "wo")}


def _ffn_of(w, which, i, tiles, tm="ffn_m"):
    (wg, layer), (wu, _), (wd, _) = (w[f"ff{which}_{n}"][i] for n in ("wg", "wu", "wd"))
    ln_final = w["ln_final"] if (which == 2 and i == DEPTH - 1) else None
    return functools.partial(_ffn, ln_g=w[f"ln_ffn{which}"][i], wg=wg, wu=wu, wd=wd, ln_final=ln_final,
                             layer=layer, tm=tiles[tm], tf=tiles["ffn_f"])


def _cast_late_weights(x, w, f32, tiles):
    flat = lambda t: t.reshape(-1, t.shape[-1])
    todo = []
    for n in FF2_STACKS:
        todo.append((n, None, (flat(f32[n]), 0, f32[n].shape[0] * f32[n].shape[1])))
    for n in FF1_STACKS:
        rows = f32[n].shape[1]
        todo += [(n, i, (flat(f32[n]), i * rows, rows)) for i in range(1, DEPTH)]
    for n in LATE_MIXER:
        todo += [(n, j, (t, 0, t.shape[0])) for j, t in enumerate(f32[n])]
    x, _, _, done = _ffn_of(w, 1, 0, tiles, tm="ffn_m_cast")(x, cast=[item for _, _, item in todo])
    for (n, i, _), t in zip(todo, done):
        if n in FF2_STACKS:
            stack = t.reshape(f32[n].shape)
            w[n] = [(stack, i) for i in range(DEPTH)]
        elif n in FF1_STACKS:
            w[n][i] = (t[None], 0)
        else:
            w[n][i] = t
    return x


def _prefill(x, n_seq, w, f32, tiles, start_side):
    M, D = x.shape
    cos, sin = _rope_tables(jnp.arange(M // n_seq, dtype=F32))
    new_shift, new_wkv, new_ret = [], [], []
    x = _cast_late_weights(x, w, f32, tiles)
    side = start_side(w)
    side_out = None

    def ffn(x, which, i):
        nonlocal side_out
        if side_out is not None:
            return _ffn_of(w, which, i, tiles)(x)
        x, *side_out, _ = _ffn_of(w, which, i, tiles, tm="ffn_m_side")(x, side=side)
        return x

    for i in range(DEPTH):
        if i > 0:
            x = ffn(x, 1, i)
        if i % N_MIXERS == 0:
            x, sh, st = _rwkv_mix(x, jnp.zeros((n_seq, D), F32), _rwkv_params(w, i), n_seq=n_seq,
                                  n_s=tiles["wkv_seqs"], T=tiles["wkv_tokens"])
            new_shift.append(sh)
            new_wkv.append(st)
        else:
            x, st = _ret_mix(x, w["ln_mix"][i], cos, sin, _ret_params(w, i), n_seq=n_seq, n_sub=tiles["ret_sub"])
            new_ret.append(st)
        x = ffn(x, 2, i)
    return (x, jnp.stack(new_shift), jnp.stack(new_wkv), jnp.stack(new_ret), *side_out)


def _decode(x, pos, shift0, wkv0, w, tiles):
    B, D = x.shape
    cos, sin = (jnp.tile(t, (B, 1)) for t in _rope_tables(pos))
    new_shift, new_wkv = [], []
    for i in range(DEPTH):
        x = _ffn_of(w, 1, i, tiles)(x)
        if i % N_MIXERS == 0:
            p = _rwkv_params(w, i)
            r, k, v, lw, a, g, sh = _rwkv_pre(x, shift0[i // N_MIXERS], p, tile=tiles["rwkv_pre"])
            y, st = _wkv_step(r, k, v, lw, a, p, wkv0[i // N_MIXERS])
            x = _rwkv_post(x, y, g, p["wo"], tm=tiles["post"])
            new_shift.append(sh)
            new_wkv.append(st)
        else:
            p = _ret_params(w, i)
            q, k, v, gate = _ret_pre(x, w["ln_mix"][i], cos, sin, p, tm=tiles["ret_pre"])
            o = yield q, k, v
            x = _ret_post(x, o, gate, p["wo"], tm=tiles["post"])
        x = _ffn_of(w, 2, i, tiles)(x)
    return x, jnp.stack(new_shift), jnp.stack(new_wkv)


PROMPT_TILES = dict(ffn_m=1024, ffn_m_cast=512, ffn_m_side=256, ffn_f=256, ret_sub=4, wkv_seqs=2, wkv_tokens=256)
SAMPLE_TILES = dict(ffn_m=128, ffn_f=256, rwkv_pre=128, post=128, ret_pre=128)

FF1_STACKS = ("ff1_wg", "ff1_wu", "ff1_wd")
FF2_STACKS = ("ff2_wg", "ff2_wu", "ff2_wd")
EARLY_MIXER = ("rw_w1", "rw_w2", "rw_a1", "rw_a2", "rw_g1", "rw_g2")
LATE_MIXER = ("rw_wr", "rw_wk", "rw_wv", "rw_wo", "ret_wq", "ret_wk", "ret_wv", "ret_wg", "ret_wo")


def kernel(x_prompt, x_sample, state_rwkv_shift, state_rwkv_wkv, state_ret, ln_ffn1, ff1_wg, ff1_wu, ff1_wd, ln_mix, ln_ffn2, ff2_wg, ff2_wu, ff2_wd, ln_final, rw_mu, rw_w0, rw_w1, rw_w2, rw_a0, rw_a1, rw_a2, rw_g1, rw_g2, rw_kk, rw_ka, rw_rk, rw_wr, rw_wk, rw_wv, rw_wo, rw_lnx_g, rw_lnx_b, ret_wq, ret_wk, ret_wv, ret_wg, ret_wo):
    w = dict(ln_ffn1=ln_ffn1, ff1_wg=ff1_wg, ff1_wu=ff1_wu, ff1_wd=ff1_wd, ln_mix=ln_mix,
             ln_ffn2=ln_ffn2, ff2_wg=ff2_wg, ff2_wu=ff2_wu, ff2_wd=ff2_wd, ln_final=ln_final,
             rw_mu=rw_mu, rw_w0=rw_w0, rw_w1=rw_w1, rw_w2=rw_w2, rw_a0=rw_a0, rw_a1=rw_a1,
             rw_a2=rw_a2, rw_g1=rw_g1, rw_g2=rw_g2, rw_kk=rw_kk, rw_ka=rw_ka, rw_rk=rw_rk,
             rw_wr=rw_wr, rw_wk=rw_wk, rw_wv=rw_wv, rw_wo=rw_wo, rw_lnx_g=rw_lnx_g,
             rw_lnx_b=rw_lnx_b, ret_wq=ret_wq, ret_wk=ret_wk, ret_wv=ret_wv, ret_wg=ret_wg,
             ret_wo=ret_wo)
    f32 = {n: w[n] for n in FF1_STACKS + FF2_STACKS}
    f32.update({n: list(w[n]) for n in LATE_MIXER})
    for n in FF1_STACKS:
        w[n] = [(w[n][:1].astype(BF16), 0)] + [None] * (DEPTH - 1)
    for n in LATE_MIXER:
        w[n] = [None] * len(f32[n])
    for n in EARLY_MIXER:
        w[n] = [t.astype(BF16) for t in w[n]]
    B, L, D = x_prompt.shape
    Bs, Ls, _ = x_sample.shape
    assert Ls == 1 and state_ret.shape[0] == 1, "one decode token and one retention layer are supported"
    H, DK, DV = RET_HEADS, RET_DK, RET_DV
    decode = None

    def start_decode(w):
        nonlocal decode
        decode = _decode(x_sample.reshape(Bs, D), PAST_LEN + jnp.arange(Ls, dtype=F32), state_rwkv_shift,
                         state_rwkv_wkv, w, SAMPLE_TILES)
        q, k, v = next(decode)
        return dict(q=q.reshape(Bs, H, DK), k=k.reshape(Bs, H, DK), v=v.reshape(Bs, H, DV), state=state_ret[0])

    y_p, p_shift, p_wkv, p_ret, o, s_ret = _prefill(x_prompt.reshape(B * L, D), B, w, f32, PROMPT_TILES,
                                                    start_decode)
    try:
        decode.send(o.reshape(Bs, H * DV))
        raise AssertionError("the decode group has a single retention layer")
    except StopIteration as done:
        y_s, s_shift, s_wkv = done.value
    return (y_p.reshape(B, L, D), y_s.reshape(Bs, Ls, D), p_shift, p_wkv, p_ret, s_shift, s_wkv, s_ret[None])
```
